```python
import jax, jax.numpy as jnp
from jax import lax
import numpy as np

D_MODEL = 4096
BATCH = 8
SEQ = 2048
DEPTH = 1
DEC_BATCH = 4
DEC_SEQ = 4096
PAST_LEN = 128

N_HEADS = 16
QK_NOPE_DIM = 128
QK_ROPE_DIM = 64
QK_HEAD_DIM = QK_NOPE_DIM + QK_ROPE_DIM
V_HEAD_DIM = 128
Q_LORA_RANK = 1024
KV_LORA_RANK = 512
ATTN_WIDTH = N_HEADS * V_HEAD_DIM
ROPE_THETA = 10000.0
Q_BLOCK = 128
FOURIER_WIDTH = D_MODEL - ATTN_WIDTH
FOURIER_GROUPS = 8
FOURIER_GROUP_DIM = FOURIER_WIDTH // FOURIER_GROUPS
IN_WIDTH = Q_LORA_RANK + KV_LORA_RANK + QK_ROPE_DIM + FOURIER_WIDTH
N_EXPERT_GROUPS = 8
EXPERTS_PER_GROUP = 8
N_EXPERTS = N_EXPERT_GROUPS * EXPERTS_PER_GROUP
TOP_K = 2
D_EXPERT = 1024
MOE_BLOCK = 128
EPS = 1e-6

kernel_name = 'hybrid_mla_fnet_hmoe_adaln_encoder'


def rms_norm(x, g):
    x32 = x.astype(jnp.float32)
    y = x32 * lax.rsqrt(jnp.mean(x32 * x32, axis=-1, keepdims=True) + EPS)
    return (y * g.astype(jnp.float32)).astype(x.dtype)


def rope(x, pos):
    half = QK_ROPE_DIM // 2
    inv_freq = ROPE_THETA ** (-2.0 * jnp.arange(half, dtype=jnp.float32) / QK_ROPE_DIM)
    ang = pos[:, None] * inv_freq[None, :]
    cos = jnp.cos(ang)[None, :, None, :]
    sin = jnp.sin(ang)[None, :, None, :]
    x32 = x.astype(jnp.float32)
    x1, x2 = x32[..., :half], x32[..., half:]
    return jnp.concatenate([x1 * cos - x2 * sin, x1 * sin + x2 * cos], axis=-1).astype(x.dtype)


def attend(q, k, v):
    b, s, h, dq = q.shape
    nb = s // Q_BLOCK
    scale = QK_HEAD_DIM ** -0.5
    qb = q.reshape(b, nb, Q_BLOCK, h, dq).transpose(1, 0, 2, 3, 4)

    def one_block(qi):
        logits = jnp.einsum('bqhd,bkhd->bhqk', qi, k).astype(jnp.float32) * scale
        p = jax.nn.softmax(logits, axis=-1).astype(v.dtype)
        return jnp.einsum('bhqk,bkhd->bqhd', p, v)

    o = lax.map(one_block, qb)
    return o.transpose(1, 0, 2, 3, 4).reshape(b, s, h, V_HEAD_DIM)


def fourier_mix(f, w_fmix):
    b, s, _ = f.shape
    fg = f.reshape(b, s, FOURIER_GROUPS, FOURIER_GROUP_DIM).astype(jnp.float32)
    spec = jnp.fft.fft2(fg, axes=(1, 3), norm='ortho').real.astype(f.dtype)
    y = jnp.einsum('bsgc,gcd->bsgd', spec, w_fmix)
    return y.reshape(b, s, FOURIER_WIDTH)


def hier_moe(h, w_group, w_route, w1, w3, w2):
    t, d = h.shape
    h32 = h.astype(jnp.float32)
    group_prob = jax.nn.softmax(h32 @ w_group.astype(jnp.float32), axis=-1)
    p_group, g_idx = lax.top_k(group_prob, 1)
    route_logits = (h32 @ w_route.reshape(d, N_EXPERTS).astype(jnp.float32)).reshape(
        t, N_EXPERT_GROUPS, EXPERTS_PER_GROUP)
    sel_logits = jnp.take_along_axis(route_logits, g_idx[:, :, None], axis=1)[:, 0]
    sec_prob = jax.nn.softmax(sel_logits, axis=-1)
    w_top, e_top = lax.top_k(sec_prob, TOP_K)
    gates = p_group * w_top / jnp.sum(w_top, axis=-1, keepdims=True)
    expert_idx = (g_idx * EXPERTS_PER_GROUP + e_top).astype(jnp.int32)

    a = t * TOP_K
    flat_e = expert_idx.reshape(a)
    flat_gate = gates.reshape(a).astype(h.dtype)
    flat_tok = jnp.arange(a, dtype=jnp.int32) // TOP_K
    order = jnp.argsort(flat_e)
    sorted_e = flat_e[order]
    counts = jnp.bincount(flat_e, length=N_EXPERTS).astype(jnp.int32)
    start = jnp.cumsum(counts) - counts
    padded = (counts + MOE_BLOCK - 1) // MOE_BLOCK * MOE_BLOCK
    pad_end = jnp.cumsum(padded)
    pad_start = pad_end - padded
    dest = pad_start[sorted_e] + jnp.arange(a, dtype=jnp.int32) - start[sorted_e]
    n_blocks = -(-a // MOE_BLOCK) + N_EXPERTS
    n_rows = n_blocks * MOE_BLOCK
    row_tok = jnp.zeros((n_rows,), jnp.int32).at[dest].set(flat_tok[order])
    row_gate = jnp.zeros((n_rows,), h.dtype).at[dest].set(flat_gate[order])
    block_start = jnp.arange(n_blocks, dtype=jnp.int32) * MOE_BLOCK
    block_expert = jnp.minimum(jnp.searchsorted(pad_end, block_start, side='right'),
                               N_EXPERTS - 1).astype(jnp.int32)

    def expert_block(args):
        tok, e = args
        xb = h[tok]
        return (jax.nn.silu(xb @ w1[e]) * (xb @ w3[e])) @ w2[e]

    y_rows = lax.map(expert_block, (row_tok.reshape(n_blocks, MOE_BLOCK), block_expert))
    y_rows = y_rows.reshape(n_rows, d)
    return jnp.zeros_like(h).at[row_tok].add(y_rows * row_gate[:, None])


def encoder_layer(x, c, w_ada, b_ada, g_attn, w_in, g_qa, w_uq, g_kva, w_ukv, g_qn, g_kn,
                  w_fmix, w_o, g_ffn, w_group, w_route, w1, w3, w2):
    b, s, d = x.shape
    mod = (jax.nn.silu(c) @ w_ada + b_ada)[:, None, :]
    sh1, sc1, gt1, sh2, sc2, gt2 = jnp.split(mod, 6, axis=-1)

    h = rms_norm(x, g_attn) * (1 + sc1) + sh1
    z = h @ w_in
    c_q, c_kv, k_pe, f_in = jnp.split(
        z, [Q_LORA_RANK, Q_LORA_RANK + KV_LORA_RANK, Q_LORA_RANK + KV_LORA_RANK + QK_ROPE_DIM], axis=-1)
    q = (rms_norm(c_q, g_qa) @ w_uq).reshape(b, s, N_HEADS, QK_HEAD_DIM)
    kv = (rms_norm(c_kv, g_kva) @ w_ukv).reshape(b, s, N_HEADS, QK_NOPE_DIM + V_HEAD_DIM)
    k_nope, v = kv[..., :QK_NOPE_DIM], kv[..., QK_NOPE_DIM:]
    k = jnp.concatenate(
        [k_nope, jnp.broadcast_to(k_pe[:, :, None, :], (b, s, N_HEADS, QK_ROPE_DIM))], axis=-1)
    q = rms_norm(q, g_qn)
    k = rms_norm(k, g_kn)
    pos = jnp.arange(s, dtype=jnp.float32)
    q = jnp.concatenate([q[..., :QK_NOPE_DIM], rope(q[..., QK_NOPE_DIM:], pos)], axis=-1)
    k = jnp.concatenate([k[..., :QK_NOPE_DIM], rope(k[..., QK_NOPE_DIM:], pos)], axis=-1)
    o_attn = attend(q, k, v).reshape(b, s, ATTN_WIDTH)
    o_four = fourier_mix(f_in, w_fmix)
    mix = jnp.concatenate([o_attn, o_four], axis=-1) @ w_o
    x = x + gt1 * mix

    h2 = rms_norm(x, g_ffn) * (1 + sc2) + sh2
    ffn = hier_moe(h2.reshape(b * s, d), w_group, w_route, w1, w3, w2).reshape(b, s, d)
    return x + gt2 * ffn


def setup_inputs(seed: int = 0) -> dict:
    key = jax.random.key(seed)
    ks = jax.random.split(key, 22)
    D = D_MODEL
    L = DEPTH

    def nrm(k, shape, scale):
        return jax.random.normal(k, shape, jnp.float32) * scale

    return {
        'x_prompt': nrm(ks[0], (BATCH, SEQ, D), 1.0),
        'x_sample': nrm(ks[1], (DEC_BATCH, DEC_SEQ, D), 1.0),
        'c_prompt': nrm(ks[2], (BATCH, D), 1.0),
        'c_sample': nrm(ks[3], (DEC_BATCH, D), 1.0),
        'w_ada': nrm(ks[4], (L, D, 6 * D), 0.5 * D ** -0.5),
        'b_ada': nrm(ks[5], (L, 6 * D), 0.02),
        'g_attn': 1.0 + nrm(ks[6], (L, D), 0.02),
        'w_in': nrm(ks[7], (L, D, IN_WIDTH), D ** -0.5),
        'g_qa': 1.0 + nrm(ks[8], (L, Q_LORA_RANK), 0.02),
        'w_uq': nrm(ks[9], (L, Q_LORA_RANK, N_HEADS * QK_HEAD_DIM), Q_LORA_RANK ** -0.5),
        'g_kva': 1.0 + nrm(ks[10], (L, KV_LORA_RANK), 0.02),
        'w_ukv': nrm(ks[11], (L, KV_LORA_RANK, N_HEADS * (QK_NOPE_DIM + V_HEAD_DIM)), KV_LORA_RANK ** -0.5),
        'g_qn': 1.0 + nrm(ks[12], (L, QK_HEAD_DIM), 0.02),
        'g_kn': 1.0 + nrm(ks[13], (L, QK_HEAD_DIM), 0.02),
        'w_fmix': nrm(ks[14], (L, FOURIER_GROUPS, FOURIER_GROUP_DIM, FOURIER_GROUP_DIM), FOURIER_GROUP_DIM ** -0.5),
        'w_o': nrm(ks[15], (L, D, D), D ** -0.5),
        'g_ffn': 1.0 + nrm(ks[16], (L, D), 0.02),
        'w_group': nrm(ks[17], (L, D, N_EXPERT_GROUPS), D ** -0.5),
        'w_route': nrm(ks[18], (L, D, N_EXPERT_GROUPS, EXPERTS_PER_GROUP), D ** -0.5),
        'w1': nrm(ks[19], (L, N_EXPERTS, D, D_EXPERT), D ** -0.5),
        'w3': nrm(ks[20], (L, N_EXPERTS, D, D_EXPERT), D ** -0.5),
        'w2': nrm(ks[21], (L, N_EXPERTS, D_EXPERT, D), D_EXPERT ** -0.5),
    }


def reference(x_prompt, x_sample, c_prompt, c_sample, w_ada, b_ada, g_attn, w_in, g_qa, w_uq,
              g_kva, w_ukv, g_qn, g_kn, w_fmix, w_o, g_ffn, w_group, w_route, w1, w3, w2):
    y_prompt = x_prompt
    y_sample = x_sample
    for l in range(DEPTH):
        lw = (w_ada[l], b_ada[l], g_attn[l], w_in[l], g_qa[l], w_uq[l], g_kva[l], w_ukv[l],
              g_qn[l], g_kn[l], w_fmix[l], w_o[l], g_ffn[l], w_group[l], w_route[l],
              w1[l], w3[l], w2[l])
        y_prompt = encoder_layer(y_prompt, c_prompt, *lw)
        y_sample = encoder_layer(y_sample, c_sample, *lw)
    return (y_prompt, y_sample)
```

```python
import dataclasses
import functools
import math

import jax
import jax.numpy as jnp
from jax import lax
from jax.experimental import pallas as pl
from jax.experimental.pallas import tpu as pltpu

BF16 = jnp.bfloat16
F32 = jnp.float32
U32 = jnp.uint32
I32 = jnp.int32

LANES = 128
MXU_DIM = 256
VMEM_CAP = 60 << 20
PACK_CHUNK = 2 * LANES


@dataclasses.dataclass(frozen=True)
class Cfg:
    d_model: int = 4096
    streams: tuple = ((8, 2048), (4, 4096))
    n_heads: int = 16
    nope: int = 128
    rope: int = 64
    v_dim: int = 128
    q_rank: int = 1024
    kv_rank: int = 512
    f_gdim: int = 256
    n_groups: int = 8
    e_per_group: int = 8
    d_expert: int = 1024
    theta: float = 10000.0
    eps: float = 1e-6
    moe_block: int = 512

    @property
    def attn_width(self):
        return self.n_heads * self.v_dim

    @property
    def f_width(self):
        return self.d_model - self.attn_width

    @property
    def f_groups(self):
        return self.f_width // self.f_gdim

    @property
    def qk_dim(self):
        return self.nope + self.rope

    @property
    def n_experts(self):
        return self.n_groups * self.e_per_group

    @property
    def tok_offsets(self):
        offs, t = [], 0
        for b, s in self.streams:
            offs.append(t)
            t += b * s
        return tuple(offs)

    @property
    def row_offsets(self):
        offs, r = [], 0
        for b, _ in self.streams:
            offs.append(r)
            r += b
        return tuple(offs)

    @property
    def n_tokens(self):
        return sum(b * s for b, s in self.streams)

    @property
    def n_rows(self):
        return sum(b for b, _ in self.streams)


def _tile(n, want):
    if n <= want:
        return n
    t = want
    while n % t:
        t -= 8
    return t


def _params(sem, vmem_bytes):
    return pltpu.CompilerParams(dimension_semantics=sem,
                                vmem_limit_bytes=int(min(VMEM_CAP, vmem_bytes)))


def _mod_row(cfg, t0):
    row = None
    for k, (b, s) in enumerate(cfg.streams):
        expr = cfg.row_offsets[k] + (t0 - cfg.tok_offsets[k]) // s
        row = expr if row is None else jnp.where(t0 >= cfg.tok_offsets[k], expr, row)
    return row


def _pos_block(cfg, i, tm):
    t0 = i * tm
    blk = None
    for k, (b, s) in enumerate(cfg.streams):
        expr = ((t0 - cfg.tok_offsets[k]) % s) // tm
        blk = expr if blk is None else jnp.where(t0 >= cfg.tok_offsets[k], expr, blk)
    return blk


def _stream_block(cfg, k, i, tm):
    nb = cfg.streams[k][0] * cfg.streams[k][1] // tm
    return jnp.clip(i - cfg.tok_offsets[k] // tm, 0, nb - 1)


def _in_stream(cfg, k, i, tm):
    lo = cfg.tok_offsets[k] // tm
    hi = lo + cfg.streams[k][0] * cfg.streams[k][1] // tm
    return jnp.logical_and(i >= lo, i < hi)


def _rms(x, eps):
    return x * lax.rsqrt(jnp.mean(x * x, axis=-1, keepdims=True) + eps)


def _pack_words(a, b):
    wa = lax.bitcast_convert_type(a.astype(BF16).astype(F32), U32) >> 16
    wb = lax.bitcast_convert_type(b.astype(BF16).astype(F32), U32) & jnp.uint32(0xFFFF0000)
    return wa | wb


def _unpack_words(w):
    lo = lax.bitcast_convert_type(w << 16, F32)
    hi = lax.bitcast_convert_type(w & jnp.uint32(0xFFFF0000), F32)
    return lo, hi


def _ada_kernel(c_ref, w_ref, b_ref, o_ref):
    c = c_ref[...]
    s = (c * jax.nn.sigmoid(c)).astype(BF16)
    o_ref[...] = jnp.dot(s, w_ref[...], preferred_element_type=F32) + b_ref[...]


def _ada(c_pad, w_ada, b_ada):
    r, d = c_pad.shape
    n = w_ada.shape[1]
    tn = _tile(n, 2048)
    return pl.pallas_call(
        _ada_kernel,
        grid=(n // tn,),
        in_specs=[pl.BlockSpec((r, d), lambda j: (0, 0)),
                  pl.BlockSpec((d, tn), lambda j: (0, j)),
                  pl.BlockSpec((1, tn), lambda j: (0, j))],
        out_specs=pl.BlockSpec((r, tn), lambda j: (0, j)),
        out_shape=jax.ShapeDtypeStruct((r, n), F32),
        compiler_params=_params(("arbitrary",), 2 * d * tn * 2 + (8 << 20)),
        name="ada",
    )(c_pad, w_ada, b_ada)


def _in_proj_kernel(cfg, tm, n_blk, *refs):
    ns = len(cfg.streams)
    x_refs = refs[:ns]
    sh_ref, sc_ref, g_ref, w_ref, o_ref, h_ref, xbuf_ref, sem = refs[ns:]
    i = pl.program_id(0)

    def fetch(blk, slot, start):
        for k in range(ns):
            @pl.when(_in_stream(cfg, k, blk, tm))
            def _(k=k):
                row = pl.multiple_of((blk - cfg.tok_offsets[k] // tm) * tm, tm)
                cp = pltpu.make_async_copy(x_refs[k].at[pl.ds(row, tm)], xbuf_ref.at[slot], sem.at[slot])
                if start:
                    cp.start()
                else:
                    cp.wait()

    @pl.when(pl.program_id(1) == 0)
    def _():
        slot = i % 2

        @pl.when(i == 0)
        def _():
            fetch(i, slot, True)

        fetch(i, slot, False)

        @pl.when(i + 1 < n_blk)
        def _():
            fetch(i + 1, 1 - slot, True)

        y = _rms(xbuf_ref[slot], cfg.eps) * g_ref[...]
        h_ref[...] = (y * (1.0 + sc_ref[0]) + sh_ref[0]).astype(BF16)

    o_ref[...] = jnp.dot(h_ref[...], w_ref[...], preferred_element_type=F32).astype(o_ref.dtype)


def _in_proj(cfg, xs, mod3, g_attn, w_cat):
    d = cfg.d_model
    t = cfg.n_tokens
    n = w_cat.shape[1]
    tm = _tile(min(s for _, s in cfg.streams), 512)
    tn = _tile(n, 768)
    in_specs = [pl.BlockSpec(memory_space=pl.ANY) for _ in xs] + [
        pl.BlockSpec((1, 1, d), lambda i, j: (_mod_row(cfg, i * tm), 0, 0)),
        pl.BlockSpec((1, 1, d), lambda i, j: (_mod_row(cfg, i * tm), 0, 1)),
        pl.BlockSpec((1, d), lambda i, j: (0, 0)),
        pl.BlockSpec((d, tn), lambda i, j: (0, j)),
    ]
    vmem = 2 * tm * d * 4 + tm * d * 2 + 2 * d * tn * 2 + 2 * tm * tn * 2 + tm * tn * 4 + 2 * tm * d * 4
    return pl.pallas_call(
        functools.partial(_in_proj_kernel, cfg, tm, t // tm),
        grid=(t // tm, n // tn),
        in_specs=in_specs,
        out_specs=pl.BlockSpec((tm, tn), lambda i, j: (i, j)),
        out_shape=jax.ShapeDtypeStruct((t, n), BF16),
        scratch_shapes=[pltpu.VMEM((tm, d), BF16), pltpu.VMEM((2, tm, d), F32), pltpu.SemaphoreType.DMA((2,))],
        compiler_params=_params(("arbitrary", "arbitrary"), vmem + (6 << 20)),
        name="in_proj",
    )(*xs, mod3, mod3, g_attn, w_cat)


def _rope_half(t):
    lane = lax.broadcasted_iota(I32, t.shape, 1)
    return jnp.where(lane < 64, t + pltpu.roll(t, 64, axis=1), 0.0)


def _q_up_kernel(cfg, c_ref, gl_ref, w_ref, ga_ref, gb_ref, cs_ref, o_ref):
    cn = (_rms(c_ref[...].astype(F32), cfg.eps) * gl_ref[...]).astype(BF16)
    cs = cs_ref[...]
    ga = ga_ref[...]
    gb = gb_ref[...]
    scale = cfg.qk_dim ** -0.5
    for h in range(cfg.n_heads):
        y = jnp.dot(cn, w_ref[h], preferred_element_type=F32)
        a = y[:, :LANES]
        b = y[:, LANES:]
        lane = lax.broadcasted_iota(I32, b.shape, 1)
        ssq = jnp.sum(a * a, axis=-1, keepdims=True) + jnp.sum(
            jnp.where(lane < 64, b * b, 0.0), axis=-1, keepdims=True)
        s = lax.rsqrt(ssq * (1.0 / cfg.qk_dim) + cfg.eps) * scale
        o_ref[h, :, :LANES] = (a * ga * s).astype(o_ref.dtype)
        o_ref[h, :, LANES:] = (_rope_half(b * gb * cs) * s).astype(o_ref.dtype)


def _q_up(cfg, z, g_qa, w_q, ga, gb, cs, tm):
    t = cfg.n_tokens
    h = cfg.n_heads
    r = cfg.q_rank
    col = cfg.f_width // r
    vmem = 2 * tm * r * 2 + 2 * h * r * 256 * 2 + 4 * h * tm * 256 * 2 + tm * 128 * 8 + tm * r * 8 + (8 << 20)
    return pl.pallas_call(
        functools.partial(_q_up_kernel, cfg),
        grid=(t // tm,),
        in_specs=[pl.BlockSpec((tm, r), lambda i: (i, col)),
                  pl.BlockSpec((1, r), lambda i: (0, 0)),
                  pl.BlockSpec((h, r, 256), lambda i: (0, 0, 0)),
                  pl.BlockSpec((1, LANES), lambda i: (0, 0)),
                  pl.BlockSpec((1, LANES), lambda i: (0, 0)),
                  pl.BlockSpec((tm, LANES), lambda i: (_pos_block(cfg, i, tm), 0))],
        out_specs=pl.BlockSpec((h, tm, 256), lambda i: (0, i, 0)),
        out_shape=jax.ShapeDtypeStruct((h, t, 256), BF16),
        compiler_params=_params(("arbitrary",), vmem),
        name="q_up",
    )(z, g_qa, w_q, ga, gb, cs)


def _kv_up_kernel(cfg, c_ref, pe_ref, gl_ref, w_ref, ga_ref, gb_ref, cs_ref, k_ref, v_ref):
    cn = (_rms(c_ref[...].astype(F32), cfg.eps) * gl_ref[...]).astype(BF16)
    pe = pe_ref[...].astype(F32)
    lane = lax.broadcasted_iota(I32, pe.shape, 1)
    ssq_pe = jnp.sum(jnp.where(lane < 64, pe * pe, 0.0), axis=-1, keepdims=True)
    kr = _rope_half(pe * gb_ref[...] * cs_ref[...])
    ga = ga_ref[...]
    for h in range(cfg.n_heads):
        y = jnp.dot(cn, w_ref[h], preferred_element_type=F32)
        kn = y[:, :LANES]
        s = lax.rsqrt((jnp.sum(kn * kn, axis=-1, keepdims=True) + ssq_pe) * (1.0 / cfg.qk_dim) + cfg.eps)
        k_ref[h, :, :LANES] = (kn * ga * s).astype(k_ref.dtype)
        k_ref[h, :, LANES:] = (kr * s).astype(k_ref.dtype)
        v_ref[h] = y[:, LANES:].astype(v_ref.dtype)


def _kv_up(cfg, z, g_kva, w_kv, ga, gb, cs, tm):
    t = cfg.n_tokens
    h = cfg.n_heads
    r = cfg.kv_rank
    col_c = (cfg.f_width + cfg.q_rank) // r
    col_pe = (cfg.f_width + cfg.q_rank + cfg.kv_rank) // LANES
    vmem = 2 * tm * (r + 128) * 2 + 2 * h * r * 256 * 2 + 4 * h * tm * 384 * 2 + tm * r * 8 + (8 << 20)
    return pl.pallas_call(
        functools.partial(_kv_up_kernel, cfg),
        grid=(t // tm,),
        in_specs=[pl.BlockSpec((tm, r), lambda i: (i, col_c)),
                  pl.BlockSpec((tm, LANES), lambda i: (i, col_pe)),
                  pl.BlockSpec((1, r), lambda i: (0, 0)),
                  pl.BlockSpec((h, r, 256), lambda i: (0, 0, 0)),
                  pl.BlockSpec((1, LANES), lambda i: (0, 0)),
                  pl.BlockSpec((1, LANES), lambda i: (0, 0)),
                  pl.BlockSpec((tm, LANES), lambda i: (_pos_block(cfg, i, tm), 0))],
        out_specs=[pl.BlockSpec((h, tm, 256), lambda i: (0, i, 0)),
                   pl.BlockSpec((h, tm, LANES), lambda i: (0, i, 0))],
        out_shape=[jax.ShapeDtypeStruct((h, t, 256), BF16),
                   jax.ShapeDtypeStruct((h, t, LANES), BF16)],
        compiler_params=_params(("arbitrary",), vmem),
        name="kv_up",
    )(z, z, g_kva, w_kv, ga, gb, cs)


def _chunk_len(cfg):
    chunk = max(s for _, s in cfg.streams)
    for (b, s), off in zip(cfg.streams, cfg.tok_offsets):
        assert chunk % s == 0 and (b * s) % chunk == 0 and off % chunk == 0
    return chunk


def _in_stream_chunks(cfg, k, c, chunk):
    lo = cfg.tok_offsets[k] // chunk
    hi = lo + cfg.streams[k][0] * cfg.streams[k][1] // chunk
    return jnp.logical_and(c >= lo, c < hi)


def _attn_kernel(cfg, chunk, tq, q_ref, k_ref, v_ref, o_ref):
    c = pl.program_id(0)
    qi = pl.program_id(2)
    for k, (_, s) in enumerate(cfg.streams):
        @pl.when(_in_stream_chunks(cfg, k, c, chunk))
        def _(s=s):
            start = pl.multiple_of(((qi * tq) // s) * s, s)
            keys = k_ref[0, pl.ds(start, s), :]
            vals = v_ref[0, pl.ds(start, s), :]
            sc = lax.dot_general(q_ref[0], keys, (((1,), (1,)), ((), ())), preferred_element_type=F32)
            m = jnp.max(sc, axis=-1, keepdims=True)
            p = jnp.exp(sc - m)
            l = jnp.sum(p, axis=-1, keepdims=True)
            o = jnp.dot(p.astype(BF16), vals, preferred_element_type=F32)
            o_ref[...] = (o / l).astype(o_ref.dtype)


def _attn(cfg, q, k, v):
    chunk = _chunk_len(cfg)
    h = cfg.n_heads
    t = cfg.n_tokens
    tq = _tile(min(s for _, s in cfg.streams), 512)
    nq = chunk // tq
    vmem = 4 * tq * 256 * 2 + 2 * chunk * 384 * 2 + 3 * tq * chunk * 4 + 4 * tq * 128 * 4 + (8 << 20)
    return pl.pallas_call(
        functools.partial(_attn_kernel, cfg, chunk, tq),
        grid=(t // chunk, h, nq),
        in_specs=[pl.BlockSpec((1, tq, 256), lambda ci, hi, qi: (hi, ci * nq + qi, 0)),
                  pl.BlockSpec((1, chunk, 256), lambda ci, hi, qi: (hi, ci, 0)),
                  pl.BlockSpec((1, chunk, LANES), lambda ci, hi, qi: (hi, ci, 0))],
        out_specs=pl.BlockSpec((tq, LANES), lambda ci, hi, qi: (ci * nq + qi, hi)),
        out_shape=jax.ShapeDtypeStruct((t, h * LANES), BF16),
        compiler_params=_params(("arbitrary", "arbitrary", "arbitrary"), vmem),
        name="attn",
    )(q, k, v)


def _fold_kernel(cc_ref, sc_ref, w_ref, o_ref):
    w = w_ref[0]
    o_ref[0, :, :256] = jnp.dot(cc_ref[...], w, preferred_element_type=F32,
                                precision=lax.Precision.HIGHEST).astype(o_ref.dtype)
    o_ref[0, :, 256:] = jnp.dot(sc_ref[...], w, preferred_element_type=F32,
                                precision=lax.Precision.HIGHEST).astype(o_ref.dtype)


def _fold_fourier_weights(cfg, cc, sc, w_fmix):
    g, c = cfg.f_groups, cfg.f_gdim
    return pl.pallas_call(
        _fold_kernel,
        grid=(g,),
        in_specs=[pl.BlockSpec((c, c), lambda i: (0, 0)),
                  pl.BlockSpec((c, c), lambda i: (0, 0)),
                  pl.BlockSpec((1, c, c), lambda i: (i, 0, 0))],
        out_specs=pl.BlockSpec((1, c, 2 * c), lambda i: (i, 0, 0)),
        out_shape=jax.ShapeDtypeStruct((g, c, 2 * c), BF16),
        compiler_params=_params(("arbitrary",), 16 << 20),
        name="fold_fourier",
    )(cc, sc, w_fmix)


def _four1_kernel(cfg, f_ref, w_ref, o_ref):
    c = cfg.f_gdim
    for g in range(cfg.f_groups):
        y = jnp.dot(f_ref[:, g * c:(g + 1) * c], w_ref[g], preferred_element_type=F32)
        o_ref[0, :, g * c:(g + 1) * c] = y[:, :c].astype(o_ref.dtype)
        o_ref[1, :, g * c:(g + 1) * c] = y[:, c:].astype(o_ref.dtype)


def _four1(cfg, z, w_fold, tm):
    t = cfg.n_tokens
    fw = cfg.f_width
    g, c = cfg.f_groups, cfg.f_gdim
    vmem = 2 * tm * fw * 2 + 2 * g * c * 2 * c * 2 + 4 * tm * fw * 2 + (8 << 20)
    return pl.pallas_call(
        functools.partial(_four1_kernel, cfg),
        grid=(t // tm,),
        in_specs=[pl.BlockSpec((tm, fw), lambda i: (i, 0)),
                  pl.BlockSpec((g, c, 2 * c), lambda i: (0, 0, 0))],
        out_specs=pl.BlockSpec((2, tm, fw), lambda i: (0, i, 0)),
        out_shape=jax.ShapeDtypeStruct((2, t, fw), BF16),
        compiler_params=_params(("arbitrary",), vmem),
        name="four1",
    )(z, w_fold)


def _four2_kernel(cfg, chunk, tm, *refs):
    ns = len(cfg.streams)
    ab_ref, o_ref = refs[2 * ns:]
    c = pl.program_id(0)
    mi = pl.program_id(2)
    for k, (_, s) in enumerate(cfg.streams):
        @pl.when(_in_stream_chunks(cfg, k, c, chunk))
        def _(k=k, s=s):
            start = pl.multiple_of(((mi * tm) // s) * s, s)
            y = jnp.dot(refs[2 * k][...], ab_ref[0, pl.ds(start, s), :], preferred_element_type=F32)
            y = y + jnp.dot(refs[2 * k + 1][...], ab_ref[1, pl.ds(start, s), :], preferred_element_type=F32)
            o_ref[...] = y.astype(o_ref.dtype)


def _four2(cfg, dfts, ab):
    chunk = _chunk_len(cfg)
    t = cfg.n_tokens
    fw = cfg.f_width
    tm = _tile(min(s for _, s in cfg.streams), 512)
    tn = _tile(fw, 512)
    nm = chunk // tm

    def d_spec(k):
        s = cfg.streams[k][1]

        def imap(ci, ni, mi):
            inside = _in_stream_chunks(cfg, k, ci, chunk)
            before = ci < cfg.tok_offsets[k] // chunk
            return (jnp.where(inside, ((mi * tm) % s) // tm, jnp.where(before, 0, s // tm - 1)), 0)
        return pl.BlockSpec((tm, s), imap)

    in_specs, args, vmem = [], [], 0
    for k, (dc, ds) in enumerate(dfts):
        in_specs += [d_spec(k), d_spec(k)]
        args += [dc, ds]
        vmem += 4 * tm * cfg.streams[k][1] * 2
    in_specs.append(pl.BlockSpec((2, chunk, tn), lambda ci, ni, mi: (0, ci, ni)))
    vmem += 4 * chunk * tn * 2 + 2 * tm * tn * 2 + 2 * tm * tn * 4 + (8 << 20)
    return pl.pallas_call(
        functools.partial(_four2_kernel, cfg, chunk, tm),
        grid=(t // chunk, fw // tn, nm),
        in_specs=in_specs,
        out_specs=pl.BlockSpec((tm, tn), lambda ci, ni, mi: (ci * nm + mi, ni)),
        out_shape=jax.ShapeDtypeStruct((t, fw), BF16),
        compiler_params=_params(("arbitrary", "arbitrary", "arbitrary"), vmem),
        name="four2",
    )(*args, ab)


def _out_proj_kernel(cfg, tm, *refs):
    ns = len(cfg.streams)
    x_refs = refs[:ns]
    gt_ref, a_ref, f_ref, wa_ref, wb_ref, o_ref = refs[ns:]
    i = pl.program_id(0)
    mix = jnp.dot(a_ref[...], wa_ref[...], preferred_element_type=F32)
    mix = mix + jnp.dot(f_ref[...], wb_ref[...], preferred_element_type=F32)
    for k in range(ns):
        @pl.when(_in_stream(cfg, k, i, tm))
        def _(k=k):
            o_ref[...] = x_refs[k][...] + gt_ref[0] * mix


def _out_proj(cfg, xs, mod3, o_attn, o_four, wa, wb):
    d = cfg.d_model
    t = cfg.n_tokens
    aw, fw = cfg.attn_width, cfg.f_width
    tm = _tile(min(s for _, s in cfg.streams), 1024)
    tn = _tile(d, 512)
    nj = d // tn

    def x_spec(k):
        def imap(i, j):
            inside = _in_stream(cfg, k, i, tm)
            before = i < cfg.tok_offsets[k] // tm
            return (_stream_block(cfg, k, i, tm), jnp.where(inside, j, jnp.where(before, 0, nj - 1)))
        return pl.BlockSpec((tm, tn), imap)

    in_specs = [x_spec(k) for k in range(len(xs))] + [
        pl.BlockSpec((1, 1, tn), lambda i, j: (_mod_row(cfg, i * tm), 0, 2 * nj + j)),
        pl.BlockSpec((tm, aw), lambda i, j: (i, 0)),
        pl.BlockSpec((tm, fw), lambda i, j: (i, 0)),
        pl.BlockSpec((aw, tn), lambda i, j: (0, j)),
        pl.BlockSpec((fw, tn), lambda i, j: (0, j)),
    ]
    vmem = (2 * tm * tn * 4 * len(xs) + 2 * tm * (aw + fw) * 2 + 2 * (aw + fw) * tn * 2
            + 2 * tm * tn * 4 + 2 * tm * tn * 4 + (8 << 20))
    return pl.pallas_call(
        functools.partial(_out_proj_kernel, cfg, tm),
        grid=(t // tm, nj),
        in_specs=in_specs,
        out_specs=pl.BlockSpec((tm, tn), lambda i, j: (i, j)),
        out_shape=jax.ShapeDtypeStruct((t, d), F32),
        compiler_params=_params(("arbitrary", "arbitrary"), vmem),
        name="out_proj",
    )(*xs, mod3, o_attn, o_four, wa, wb)


def _router_kernel(cfg, tm, x_ref, sh_ref, sc_ref, g_ref, wr_ref, tri_ref,
                   hp_ref, mi_ref, gt_ref, cnt_ref, carry_ref):
    i = pl.program_id(0)
    ng, ne = cfg.n_groups, cfg.e_per_group
    n_exp = cfg.n_experts

    @pl.when(i == 0)
    def _():
        carry_ref[...] = jnp.zeros_like(carry_ref)

    h2 = _rms(x_ref[...], cfg.eps) * g_ref[...] * (1.0 + sc_ref[0]) + sh_ref[0]
    for c in range(cfg.d_model // PACK_CHUNK):
        lo = h2[:, c * PACK_CHUNK:c * PACK_CHUNK + LANES]
        hi = h2[:, c * PACK_CHUNK + LANES:(c + 1) * PACK_CHUNK]
        hp_ref[:, c * LANES:(c + 1) * LANES] = _pack_words(lo, hi)

    h_hi = h2.astype(BF16)
    h_lo = (h2 - h_hi.astype(F32)).astype(BF16)
    p1 = jnp.dot(h_hi, wr_ref[...], preferred_element_type=F32)
    p2 = jnp.dot(h_lo, wr_ref[:, :LANES], preferred_element_type=F32)
    logits = (p1[:, :LANES] + p1[:, LANES:] + p2).T

    gl = logits[0:ng]
    io = lax.broadcasted_iota(I32, (ng, tm), 0)
    gm = jnp.max(gl, axis=0, keepdims=True)
    p_group = 1.0 / jnp.sum(jnp.exp(gl - gm), axis=0, keepdims=True)
    gidx = jnp.min(jnp.where(gl == gm, io, ng), axis=0, keepdims=True)
    sel = jnp.zeros((ne, tm), F32)
    for g in range(ng):
        sel = jnp.where(gidx == g, logits[ng + g * ne:ng + (g + 1) * ne], sel)
    ie = lax.broadcasted_iota(I32, (ne, tm), 0)
    m1 = jnp.max(sel, axis=0, keepdims=True)
    i1 = jnp.min(jnp.where(sel == m1, ie, ne), axis=0, keepdims=True)
    sel2 = jnp.where(ie == i1, -jnp.inf, sel)
    m2 = jnp.max(sel2, axis=0, keepdims=True)
    i2 = jnp.min(jnp.where(sel2 == m2, ie, ne), axis=0, keepdims=True)
    e21 = jnp.exp(m2 - m1)
    gate0 = p_group / (1.0 + e21)
    gate1 = p_group * e21 / (1.0 + e21)
    e0 = gidx * ne + i1
    e1 = gidx * ne + i2

    ix = lax.broadcasted_iota(I32, (n_exp, tm), 0)
    hit0 = ix == e0
    hit1 = ix == e1
    member = jnp.logical_or(hit0, hit1).astype(F32)
    before = jnp.dot(member.astype(BF16), tri_ref[...], preferred_element_type=F32)
    total = before + carry_ref[:, 0:1]
    r0 = jnp.sum(jnp.where(hit0, total, 0.0), axis=0, keepdims=True).astype(I32)
    r1 = jnp.sum(jnp.where(hit1, total, 0.0), axis=0, keepdims=True).astype(I32)
    carry_ref[...] = carry_ref[...] + jnp.sum(member, axis=1, keepdims=True)
    cnt_ref[...] = carry_ref[...].astype(I32)

    i8 = lax.broadcasted_iota(I32, (8, tm), 0)
    mi_ref[...] = jnp.where(i8 == 0, e0, jnp.where(i8 == 1, e1, jnp.where(i8 == 2, r0, jnp.where(i8 == 3, r1, 0))))
    il = lax.broadcasted_iota(I32, (LANES, tm), 0)
    gt_ref[...] = jnp.where(il == 0, gate0, jnp.where(il == 1, gate1, 0.0)).T


def _router(cfg, x1, mod3, g_ffn, wr_cat, tri, tm):
    d = cfg.d_model
    t = cfg.n_tokens
    ne = cfg.n_experts
    vmem = 2 * tm * d * 4 + 2 * tm * d * 2 + 2 * d * 256 * 2 + 6 * tm * d * 4 + 2 * tm * tm * 2 + (8 << 20)
    return pl.pallas_call(
        functools.partial(_router_kernel, cfg, tm),
        grid=(t // tm,),
        in_specs=[pl.BlockSpec((tm, d), lambda i: (i, 0)),
                  pl.BlockSpec((1, 1, d), lambda i: (_mod_row(cfg, i * tm), 0, 3)),
                  pl.BlockSpec((1, 1, d), lambda i: (_mod_row(cfg, i * tm), 0, 4)),
                  pl.BlockSpec((1, d), lambda i: (0, 0)),
                  pl.BlockSpec((d, 256), lambda i: (0, 0)),
                  pl.BlockSpec((tm, tm), lambda i: (0, 0))],
        out_specs=[pl.BlockSpec((tm, d // 2), lambda i: (i, 0)),
                   pl.BlockSpec((8, tm), lambda i: (0, i)),
                   pl.BlockSpec((tm, LANES), lambda i: (i, 0)),
                   pl.BlockSpec((ne, LANES), lambda i: (0, 0))],
        out_shape=[jax.ShapeDtypeStruct((t, d // 2), U32),
                   jax.ShapeDtypeStruct((8, t), I32),
                   jax.ShapeDtypeStruct((t, LANES), F32),
                   jax.ShapeDtypeStruct((ne, LANES), I32)],
        scratch_shapes=[pltpu.VMEM((ne, LANES), F32)],
        compiler_params=_params(("arbitrary",), vmem),
        name="router",
    )(x1, mod3, mod3, g_ffn, wr_cat, tri)


def _row_copy(src, s_row, dst, d_row, sem):
    return pltpu.make_async_copy(src.at[pl.ds(s_row, 1)], dst.at[pl.ds(d_row, 1)], sem)


def _dispatch_kernel(tm, pos_ref, h_ref, xs_in_ref, xs_ref, sem):
    del xs_in_ref
    base = pl.program_id(0) * tm

    def issue(r, carry):
        for kk in range(2):
            _row_copy(h_ref, base + r, xs_ref, pos_ref[kk, r], sem).start()
        return carry

    def drain(r, carry):
        for kk in range(2):
            _row_copy(h_ref, base + r, xs_ref, pos_ref[kk, r], sem).wait()
        return carry

    lax.fori_loop(0, tm, issue, 0)
    lax.fori_loop(0, tm, drain, 0)


def _dispatch(cfg, pos, h2p, xs_init, tm):
    t = cfg.n_tokens
    return pl.pallas_call(
        functools.partial(_dispatch_kernel, tm),
        grid=(t // tm,),
        in_specs=[pl.BlockSpec((2, tm), lambda i: (0, i), memory_space=pltpu.SMEM),
                  pl.BlockSpec(memory_space=pl.ANY),
                  pl.BlockSpec(memory_space=pl.ANY)],
        out_specs=pl.BlockSpec(memory_space=pl.ANY),
        out_shape=jax.ShapeDtypeStruct(xs_init.shape, xs_init.dtype),
        input_output_aliases={2: 0},
        scratch_shapes=[pltpu.SemaphoreType.DMA(())],
        compiler_params=_params(("arbitrary",), 4 << 20),
        name="dispatch",
    )(pos, h2p, xs_init)


def _moe1_kernel(cfg, be_ref, nv_ref, x_ref, w1_ref, w3_ref, o_ref, xb_ref):
    del be_ref
    i = pl.program_id(1)

    @pl.when(i < nv_ref[0])
    def _():
        for c in range(cfg.d_model // PACK_CHUNK):
            lo, hi = _unpack_words(x_ref[:, c * LANES:(c + 1) * LANES])
            xb_ref[:, c * PACK_CHUNK:c * PACK_CHUNK + LANES] = lo.astype(BF16)
            xb_ref[:, c * PACK_CHUNK + LANES:(c + 1) * PACK_CHUNK] = hi.astype(BF16)
        x = xb_ref[...]
        a = jnp.dot(x, w1_ref[0], preferred_element_type=F32)
        b = jnp.dot(x, w3_ref[0], preferred_element_type=F32)
        o_ref[...] = (a * jax.nn.sigmoid(a) * b).astype(o_ref.dtype)

    @pl.when(i >= nv_ref[0])
    def _():
        o_ref[...] = jnp.zeros_like(o_ref)


def _moe1(cfg, blk_e, n_valid, xs, w1, w3):
    d, f = cfg.d_model, cfg.d_expert
    bm = cfg.moe_block
    rows = xs.shape[0]
    nb = rows // bm
    tf = _tile(f, 512)

    def eff(i, nv):
        return jnp.minimum(i, nv[0] - 1)

    grid_spec = pltpu.PrefetchScalarGridSpec(
        num_scalar_prefetch=2,
        grid=(f // tf, nb),
        in_specs=[pl.BlockSpec((bm, d // 2), lambda j, i, be, nv: (eff(i, nv), 0)),
                  pl.BlockSpec((1, d, tf), lambda j, i, be, nv: (be[eff(i, nv)], 0, j)),
                  pl.BlockSpec((1, d, tf), lambda j, i, be, nv: (be[eff(i, nv)], 0, j))],
        out_specs=pl.BlockSpec((bm, tf), lambda j, i, be, nv: (i, j)),
        scratch_shapes=[pltpu.VMEM((bm, d), BF16)],
    )
    vmem = 2 * bm * d * 2 + 4 * d * tf * 2 + bm * d * 2 + 2 * bm * tf * 2 + 4 * bm * tf * 4 + (8 << 20)
    return pl.pallas_call(
        functools.partial(_moe1_kernel, cfg),
        grid_spec=grid_spec,
        out_shape=jax.ShapeDtypeStruct((rows, f), BF16),
        compiler_params=_params(("arbitrary", "arbitrary"), vmem),
        name="moe1",
    )(blk_e, n_valid, xs, w1, w3)


def _moe2_kernel(cfg, be_ref, nv_ref, h_ref, w2_ref, o_ref):
    del be_ref
    i = pl.program_id(0)

    @pl.when(i < nv_ref[0])
    def _():
        y = jnp.dot(h_ref[...], w2_ref[0], preferred_element_type=F32)
        for c in range(cfg.d_model // PACK_CHUNK):
            o_ref[:, c * LANES:(c + 1) * LANES] = _pack_words(
                y[:, c * PACK_CHUNK:c * PACK_CHUNK + LANES], y[:, c * PACK_CHUNK + LANES:(c + 1) * PACK_CHUNK])

    @pl.when(i >= nv_ref[0])
    def _():
        o_ref[...] = jnp.zeros_like(o_ref)


def _moe2(cfg, blk_e, n_valid, hmid, w2):
    d, f = cfg.d_model, cfg.d_expert
    bm = cfg.moe_block
    rows = hmid.shape[0]

    def eff(i, nv):
        return jnp.minimum(i, nv[0] - 1)

    grid_spec = pltpu.PrefetchScalarGridSpec(
        num_scalar_prefetch=2,
        grid=(rows // bm,),
        in_specs=[pl.BlockSpec((bm, f), lambda i, be, nv: (eff(i, nv), 0)),
                  pl.BlockSpec((1, f, d), lambda i, be, nv: (be[eff(i, nv)], 0, 0))],
        out_specs=pl.BlockSpec((bm, d // 2), lambda i, be, nv: (i, 0)),
    )
    vmem = 2 * bm * f * 2 + 2 * f * d * 2 + 2 * bm * d * 2 + 2 * bm * d * 4 + (8 << 20)
    return pl.pallas_call(
        functools.partial(_moe2_kernel, cfg),
        grid_spec=grid_spec,
        out_shape=jax.ShapeDtypeStruct((rows, d // 2), U32),
        compiler_params=_params(("arbitrary",), vmem),
        name="moe2",
    )(blk_e, n_valid, hmid, w2)


def _combine_kernel(cfg, tm, pos_ref, x_ref, gt2_ref, gate_ref, y_ref, o_ref, buf_ref, sem):
    def issue(r, carry):
        for kk in range(2):
            pltpu.make_async_copy(y_ref.at[pl.ds(pos_ref[kk, r], 1)], buf_ref.at[kk, pl.ds(r, 1)], sem).start()
        return carry

    def drain(r, carry):
        for kk in range(2):
            pltpu.make_async_copy(y_ref.at[pl.ds(pos_ref[kk, r], 1)], buf_ref.at[kk, pl.ds(r, 1)], sem).wait()
        return carry

    lax.fori_loop(0, tm, issue, 0)
    lax.fori_loop(0, tm, drain, 0)

    g0 = gate_ref[:, 0:1]
    g1 = gate_ref[:, 1:2]
    for c in range(cfg.d_model // PACK_CHUNK):
        lo0, hi0 = _unpack_words(buf_ref[0, :, c * LANES:(c + 1) * LANES])
        lo1, hi1 = _unpack_words(buf_ref[1, :, c * LANES:(c + 1) * LANES])
        sl_lo = slice(c * PACK_CHUNK, c * PACK_CHUNK + LANES)
        sl_hi = slice(c * PACK_CHUNK + LANES, (c + 1) * PACK_CHUNK)
        o_ref[:, sl_lo] = x_ref[:, sl_lo] + gt2_ref[0, :, sl_lo] * (g0 * lo0 + g1 * lo1)
        o_ref[:, sl_hi] = x_ref[:, sl_hi] + gt2_ref[0, :, sl_hi] * (g0 * hi0 + g1 * hi1)


def _combine(cfg, k_idx, pos, x1, mod3, gates, y, tm):
    b, s = cfg.streams[k_idx]
    off = cfg.tok_offsets[k_idx]
    d = cfg.d_model
    n = b * s
    ob = off // tm
    vmem = 4 * tm * d * 4 + 2 * tm * d * 2 + 2 * tm * 128 * 4 + 6 * tm * 128 * 4 + (8 << 20)
    return pl.pallas_call(
        functools.partial(_combine_kernel, cfg, tm),
        grid=(n // tm,),
        in_specs=[pl.BlockSpec((2, tm), lambda i: (0, ob + i), memory_space=pltpu.SMEM),
                  pl.BlockSpec((tm, d), lambda i: (ob + i, 0)),
                  pl.BlockSpec((1, 1, d), lambda i: (cfg.row_offsets[k_idx] + (i * tm) // s, 0, 5)),
                  pl.BlockSpec((tm, LANES), lambda i: (ob + i, 0)),
                  pl.BlockSpec(memory_space=pl.ANY)],
        out_specs=pl.BlockSpec((tm, d), lambda i: (i, 0)),
        out_shape=jax.ShapeDtypeStruct((n, d), F32),
        scratch_shapes=[pltpu.VMEM((2, tm, d // 2), U32), pltpu.SemaphoreType.DMA(())],
        compiler_params=_params(("arbitrary",), vmem),
        name=f"combine{k_idx}",
    )(pos, x1, mod3, gates, y)


def _rot_half(a, axis=-1):
    h = a.shape[axis] // 2
    lo = lax.slice_in_dim(a, 0, h, axis=axis)
    hi = lax.slice_in_dim(a, h, 2 * h, axis=axis)
    return jnp.concatenate([hi, lo], axis=axis)


def _rope_table(cfg, n_pos):
    half = cfg.rope // 2
    inv_freq = cfg.theta ** (-2.0 * jnp.arange(half, dtype=F32) / cfg.rope)
    ang = jnp.arange(n_pos, dtype=F32)[:, None] * inv_freq[None, :]
    cos, sin = jnp.cos(ang), jnp.sin(ang)
    return jnp.concatenate([cos, cos, -sin, sin], axis=-1)


def _dft(n, scale):
    idx = jnp.arange(n, dtype=I32)
    ang = ((idx[:, None] * idx[None, :]) % n).astype(F32) * (2.0 * math.pi / n)
    return jnp.cos(ang) * scale, jnp.sin(ang) * scale


def _layer(cfg, xs, cs_, w_ada, b_ada, g_attn, w_in, g_qa, w_uq, g_kva, w_ukv, g_qn, g_kn,
           w_fmix, w_o, g_ffn, w_group, w_route, w1, w3, w2):
    d, h = cfg.d_model, cfg.n_heads
    t = cfg.n_tokens
    nope, rope = cfg.nope, cfg.rope
    assert nope == LANES and cfg.v_dim == LANES and rope == 64 and cfg.f_gdim == PACK_CHUNK
    assert cfg.f_width % cfg.q_rank == 0 and (cfg.f_width + cfg.q_rank) % cfg.kv_rank == 0
    qr, kr = cfg.q_rank, cfg.kv_rank

    wq_in, wkv_in, wpe_in, wf_in = jnp.split(w_in, [qr, qr + kr, qr + kr + rope], axis=1)
    w_cat = jnp.concatenate([wf_in, wq_in, wkv_in, wpe_in, _rot_half(wpe_in)], axis=1)
    n_cat = w_cat.shape[1]
    tn_in = 768 if n_cat > 768 else n_cat
    w_cat = jnp.pad(w_cat, ((0, 0), (0, -n_cat % tn_in))).astype(BF16)

    wq3 = w_uq.reshape(qr, h, nope + rope)
    w_q = jnp.concatenate([wq3, _rot_half(wq3[..., nope:])], axis=-1).transpose(1, 0, 2).astype(BF16)
    w_kv = w_ukv.reshape(kr, h, nope + cfg.v_dim).transpose(1, 0, 2).astype(BF16)
    ga_q, gb_q = g_qn[None, :nope], jnp.concatenate([g_qn[nope:], _rot_half(g_qn[nope:])])[None]
    ga_k, gb_k = g_kn[None, :nope], jnp.concatenate([g_kn[nope:], _rot_half(g_kn[nope:])])[None]
    wa, wb = w_o[:cfg.attn_width].astype(BF16), w_o[cfg.attn_width:].astype(BF16)
    wr = jnp.concatenate([w_group, w_route.reshape(d, cfg.n_experts)], axis=1)
    wr = jnp.pad(wr, ((0, 0), (0, LANES - wr.shape[1])))
    wr_hi = wr.astype(BF16)
    wr_cat = jnp.concatenate([wr_hi, (wr - wr_hi.astype(F32)).astype(BF16)], axis=1)
    w1b, w3b, w2b = w1.astype(BF16), w3.astype(BF16), w2.astype(BF16)

    s_max = max(s for _, s in cfg.streams)
    cs_tab = _rope_table(cfg, s_max)
    cc, sc = _dft(cfg.f_gdim, cfg.f_gdim ** -0.5)

    c_all = jnp.concatenate(cs_, axis=0)
    rows = -(-c_all.shape[0] // 8) * 8
    c_pad = jnp.pad(c_all, ((0, rows - c_all.shape[0]), (0, 0)))
    mod3 = _ada(c_pad, w_ada.astype(BF16), b_ada[None]).reshape(rows, 1, 6 * d)

    x2d = [x.reshape(-1, d) for x in xs]
    z = _in_proj(cfg, x2d, mod3, g_attn[None], w_cat)
    tm = _tile(min(s for _, s in cfg.streams), 512)
    q = _q_up(cfg, z, g_qa[None], w_q, ga_q, gb_q, cs_tab, tm)
    k, v = _kv_up(cfg, z, g_kva[None], w_kv, ga_k, gb_k, cs_tab, tm)
    o_attn = _attn(cfg, q, k, v)

    w_fold = _fold_fourier_weights(cfg, cc, sc, w_fmix)
    ab = _four1(cfg, z, w_fold, tm)
    dfts = []
    for _, s in cfg.streams:
        dc, ds = _dft(s, s ** -0.5)
        dfts.append((dc.astype(BF16), (-ds).astype(BF16)))
    o_four = _four2(cfg, dfts, ab)

    x1 = _out_proj(cfg, x2d, mod3, o_attn, o_four, wa, wb)

    tm_r = _tile(tm, 256)
    tri = (jnp.arange(tm_r)[:, None] < jnp.arange(tm_r)[None, :]).astype(BF16)
    h2p, meta, gates, counts = _router(cfg, x1, mod3, g_ffn[None], wr_cat, tri, tm_r)

    bm = cfg.moe_block
    cnt = counts[:, 0]
    padded = (cnt + bm - 1) // bm * bm
    pad_end = jnp.cumsum(padded)
    pad_start = pad_end - padded
    pos = pad_start[meta[0:2]] + meta[2:4]
    n_blocks = (2 * t) // bm + cfg.n_experts
    blk_start = jnp.arange(n_blocks, dtype=I32) * bm
    blk_e = jnp.minimum(jnp.searchsorted(pad_end, blk_start, side="right"), cfg.n_experts - 1).astype(I32)
    n_valid = (pad_end[-1] // bm).astype(I32)[None]

    xs_rows = _dispatch(cfg, pos, h2p, jnp.zeros((n_blocks * bm, d // 2), U32), _tile(tm, 256))
    hmid = _moe1(cfg, blk_e, n_valid, xs_rows, w1b, w3b)
    y = _moe2(cfg, blk_e, n_valid, hmid, w2b)

    outs = []
    for ki, (b, s) in enumerate(cfg.streams):
        o = _combine(cfg, ki, pos, x1, mod3, gates, y, _tile(tm, 256))
        outs.append(o.reshape(b, s, d))
    return tuple(outs)


def kernel(x_prompt, x_sample, c_prompt, c_sample, w_ada, b_ada, g_attn, w_in, g_qa, w_uq, g_kva, w_ukv,
           g_qn, g_kn, w_fmix, w_o, g_ffn, w_group, w_route, w1, w3, w2):
    cfg = Cfg()
    xs = (x_prompt, x_sample)
    cs_ = (c_prompt, c_sample)
    for l in range(w_ada.shape[0]):
        xs = _layer(cfg, xs, cs_, w_ada[l], b_ada[l], g_attn[l], w_in[l], g_qa[l], w_uq[l], g_kva[l],
                    w_ukv[l], g_qn[l], g_kn[l], w_fmix[l], w_o[l], g_ffn[l], w_group[l], w_route[l],
                    w1[l], w3[l], w2[l])
    return xs
```

```python
import dataclasses
import functools
import math

import jax
import jax.numpy as jnp
from jax import lax
from jax.experimental import pallas as pl
from jax.experimental.pallas import tpu as pltpu

BF16 = jnp.bfloat16
F32 = jnp.float32
U32 = jnp.uint32
I32 = jnp.int32

LANES = 128
MXU_DIM = 256
VMEM_CAP = 60 << 20
PACK_CHUNK = 2 * LANES


@dataclasses.dataclass(frozen=True)
class Cfg:
    d_model: int = 4096
    streams: tuple = ((8, 2048), (4, 4096))
    n_heads: int = 16
    nope: int = 128
    rope: int = 64
    v_dim: int = 128
    q_rank: int = 1024
    kv_rank: int = 512
    f_gdim: int = 256
    n_groups: int = 8
    e_per_group: int = 8
    d_expert: int = 1024
    theta: float = 10000.0
    eps: float = 1e-6
    moe_block: int = 256

    @property
    def attn_width(self):
        return self.n_heads * self.v_dim

    @property
    def f_width(self):
        return self.d_model - self.attn_width

    @property
    def f_groups(self):
        return self.f_width // self.f_gdim

    @property
    def qk_dim(self):
        return self.nope + self.rope

    @property
    def n_experts(self):
        return self.n_groups * self.e_per_group

    @property
    def tok_offsets(self):
        offs, t = [], 0
        for b, s in self.streams:
            offs.append(t)
            t += b * s
        return tuple(offs)

    @property
    def row_offsets(self):
        offs, r = [], 0
        for b, _ in self.streams:
            offs.append(r)
            r += b
        return tuple(offs)

    @property
    def n_tokens(self):
        return sum(b * s for b, s in self.streams)

    @property
    def n_rows(self):
        return sum(b for b, _ in self.streams)


def _tile(n, want):
    if n <= want:
        return n
    t = want
    while n % t:
        t -= 8
    return t


def _params(sem, vmem_bytes):
    return pltpu.CompilerParams(dimension_semantics=sem,
                                vmem_limit_bytes=int(min(VMEM_CAP, vmem_bytes)))


def _mod_row(cfg, t0):
    row = None
    for k, (b, s) in enumerate(cfg.streams):
        expr = cfg.row_offsets[k] + (t0 - cfg.tok_offsets[k]) // s
        row = expr if row is None else jnp.where(t0 >= cfg.tok_offsets[k], expr, row)
    return row


def _pos_block(cfg, i, tm):
    t0 = i * tm
    blk = None
    for k, (b, s) in enumerate(cfg.streams):
        expr = ((t0 - cfg.tok_offsets[k]) % s) // tm
        blk = expr if blk is None else jnp.where(t0 >= cfg.tok_offsets[k], expr, blk)
    return blk


def _stream_block(cfg, k, i, tm):
    nb = cfg.streams[k][0] * cfg.streams[k][1] // tm
    return jnp.clip(i - cfg.tok_offsets[k] // tm, 0, nb - 1)


def _in_stream(cfg, k, i, tm):
    lo = cfg.tok_offsets[k] // tm
    hi = lo + cfg.streams[k][0] * cfg.streams[k][1] // tm
    return jnp.logical_and(i >= lo, i < hi)


def _rms(x, eps):
    return x * lax.rsqrt(jnp.mean(x * x, axis=-1, keepdims=True) + eps)


def _pack_words(a, b):
    wa = lax.bitcast_convert_type(a.astype(BF16).astype(F32), U32) >> 16
    wb = lax.bitcast_convert_type(b.astype(BF16).astype(F32), U32) & jnp.uint32(0xFFFF0000)
    return wa | wb


def _unpack_words(w):
    lo = lax.bitcast_convert_type(w << 16, F32)
    hi = lax.bitcast_convert_type(w & jnp.uint32(0xFFFF0000), F32)
    return lo, hi


def _ada_kernel(c_ref, w_ref, b_ref, o_ref):
    c = c_ref[...]
    s = (c * jax.nn.sigmoid(c)).astype(BF16)
    o_ref[...] = jnp.dot(s, w_ref[...], preferred_element_type=F32) + b_ref[...]


def _ada(c_pad, w_ada, b_ada):
    r, d = c_pad.shape
    n = w_ada.shape[1]
    tn = _tile(n, 2048)
    return pl.pallas_call(
        _ada_kernel,
        grid=(n // tn,),
        in_specs=[pl.BlockSpec((r, d), lambda j: (0, 0)),
                  pl.BlockSpec((d, tn), lambda j: (0, j)),
                  pl.BlockSpec((1, tn), lambda j: (0, j))],
        out_specs=pl.BlockSpec((r, tn), lambda j: (0, j)),
        out_shape=jax.ShapeDtypeStruct((r, n), F32),
        compiler_params=_params(("arbitrary",), 2 * d * tn * 2 + (8 << 20)),
        name="ada",
    )(c_pad, w_ada, b_ada)


def _in_proj_kernel(cfg, tm, n_blk, *refs):
    ns = len(cfg.streams)
    x_refs = refs[:ns]
    sh_ref, sc_ref, g_ref, w_ref, o_ref, h_ref, xbuf_ref, sem = refs[ns:]
    i = pl.program_id(0)

    def fetch(blk, slot, start):
        for k in range(ns):
            @pl.when(_in_stream(cfg, k, blk, tm))
            def _(k=k):
                row = pl.multiple_of((blk - cfg.tok_offsets[k] // tm) * tm, tm)
                cp = pltpu.make_async_copy(x_refs[k].at[pl.ds(row, tm)], xbuf_ref.at[slot], sem.at[slot])
                if start:
                    cp.start()
                else:
                    cp.wait()

    @pl.when(pl.program_id(1) == 0)
    def _():
        slot = i % 2

        @pl.when(i == 0)
        def _():
            fetch(i, slot, True)

        fetch(i, slot, False)

        @pl.when(i + 1 < n_blk)
        def _():
            fetch(i + 1, 1 - slot, True)

        y = _rms(xbuf_ref[slot], cfg.eps) * g_ref[...]
        h_ref[...] = (y * (1.0 + sc_ref[0]) + sh_ref[0]).astype(BF16)

    o_ref[...] = jnp.dot(h_ref[...], w_ref[...], preferred_element_type=F32).astype(o_ref.dtype)


def _in_proj(cfg, xs, mod3, g_attn, w_cat):
    d = cfg.d_model
    t = cfg.n_tokens
    n = w_cat.shape[1]
    tm = _tile(min(s for _, s in cfg.streams), 512)
    tn = _tile(n, 768)
    in_specs = [pl.BlockSpec(memory_space=pl.ANY) for _ in xs] + [
        pl.BlockSpec((1, 1, d), lambda i, j: (_mod_row(cfg, i * tm), 0, 0)),
        pl.BlockSpec((1, 1, d), lambda i, j: (_mod_row(cfg, i * tm), 0, 1)),
        pl.BlockSpec((1, d), lambda i, j: (0, 0)),
        pl.BlockSpec((d, tn), lambda i, j: (0, j)),
    ]
    vmem = 2 * tm * d * 4 + tm * d * 2 + 2 * d * tn * 2 + 2 * tm * tn * 2 + tm * tn * 4 + 2 * tm * d * 4
    return pl.pallas_call(
        functools.partial(_in_proj_kernel, cfg, tm, t // tm),
        grid=(t // tm, n // tn),
        in_specs=in_specs,
        out_specs=pl.BlockSpec((tm, tn), lambda i, j: (i, j)),
        out_shape=jax.ShapeDtypeStruct((t, n), BF16),
        scratch_shapes=[pltpu.VMEM((tm, d), BF16), pltpu.VMEM((2, tm, d), F32), pltpu.SemaphoreType.DMA((2,))],
        compiler_params=_params(("arbitrary", "arbitrary"), vmem + (6 << 20)),
        name="in_proj",
    )(*xs, mod3, mod3, g_attn, w_cat)


def _rope_half(t):
    lane = lax.broadcasted_iota(I32, t.shape, 1)
    return jnp.where(lane < 64, t + pltpu.roll(t, 64, axis=1), 0.0)


def _q_up_kernel(cfg, c_ref, gl_ref, w_ref, ga_ref, gb_ref, cs_ref, o_ref):
    cn = (_rms(c_ref[...].astype(F32), cfg.eps) * gl_ref[...]).astype(BF16)
    cs = cs_ref[...]
    ga = ga_ref[...]
    gb = gb_ref[...]
    scale = cfg.qk_dim ** -0.5 * math.log2(math.e)
    for h in range(cfg.n_heads):
        y = jnp.dot(cn, w_ref[h], preferred_element_type=F32)
        a = y[:, :LANES]
        b = y[:, LANES:]
        lane = lax.broadcasted_iota(I32, b.shape, 1)
        ssq = jnp.sum(a * a, axis=-1, keepdims=True) + jnp.sum(
            jnp.where(lane < 64, b * b, 0.0), axis=-1, keepdims=True)
        s = lax.rsqrt(ssq * (1.0 / cfg.qk_dim) + cfg.eps) * scale
        o_ref[h, :, :LANES] = (a * ga * s).astype(o_ref.dtype)
        o_ref[h, :, LANES:] = (_rope_half(b * gb * cs) * s).astype(o_ref.dtype)


def _q_up(cfg, z, g_qa, w_q, ga, gb, cs, tm):
    t = cfg.n_tokens
    h = cfg.n_heads
    r = cfg.q_rank
    col = cfg.f_width // r
    vmem = 2 * tm * r * 2 + 2 * h * r * 256 * 2 + 4 * h * tm * 256 * 2 + tm * 128 * 8 + tm * r * 8 + (8 << 20)
    return pl.pallas_call(
        functools.partial(_q_up_kernel, cfg),
        grid=(t // tm,),
        in_specs=[pl.BlockSpec((tm, r), lambda i: (i, col)),
                  pl.BlockSpec((1, r), lambda i: (0, 0)),
                  pl.BlockSpec((h, r, 256), lambda i: (0, 0, 0)),
                  pl.BlockSpec((1, LANES), lambda i: (0, 0)),
                  pl.BlockSpec((1, LANES), lambda i: (0, 0)),
                  pl.BlockSpec((tm, LANES), lambda i: (_pos_block(cfg, i, tm), 0))],
        out_specs=pl.BlockSpec((h, tm, 256), lambda i: (0, i, 0)),
        out_shape=jax.ShapeDtypeStruct((h, t, 256), BF16),
        compiler_params=_params(("arbitrary",), vmem),
        name="q_up",
    )(z, g_qa, w_q, ga, gb, cs)


def _kv_up_kernel(cfg, c_ref, pe_ref, gl_ref, w_ref, ga_ref, gb_ref, cs_ref, k_ref, v_ref):
    cn = (_rms(c_ref[...].astype(F32), cfg.eps) * gl_ref[...]).astype(BF16)
    pe = pe_ref[...].astype(F32)
    lane = lax.broadcasted_iota(I32, pe.shape, 1)
    ssq_pe = jnp.sum(jnp.where(lane < 64, pe * pe, 0.0), axis=-1, keepdims=True)
    kr = _rope_half(pe * gb_ref[...] * cs_ref[...])
    ga = ga_ref[...]
    for h in range(cfg.n_heads):
        y = jnp.dot(cn, w_ref[h], preferred_element_type=F32)
        kn = y[:, :LANES]
        s = lax.rsqrt((jnp.sum(kn * kn, axis=-1, keepdims=True) + ssq_pe) * (1.0 / cfg.qk_dim) + cfg.eps)
        k_ref[h, :, :LANES] = (kn * ga * s).astype(k_ref.dtype)
        k_ref[h, :, LANES:] = (kr * s).astype(k_ref.dtype)
        v_ref[h, :, :LANES] = y[:, LANES:].astype(v_ref.dtype)
        v_ref[h, :, LANES:] = jnp.where(lane == 0, 1.0, 0.0).astype(v_ref.dtype)


def _kv_up(cfg, z, g_kva, w_kv, ga, gb, cs, tm):
    t = cfg.n_tokens
    h = cfg.n_heads
    r = cfg.kv_rank
    col_c = (cfg.f_width + cfg.q_rank) // r
    col_pe = (cfg.f_width + cfg.q_rank + cfg.kv_rank) // LANES
    vmem = 2 * tm * (r + 128) * 2 + 2 * h * r * 256 * 2 + 4 * h * tm * 512 * 2 + tm * r * 8 + (8 << 20)
    return pl.pallas_call(
        functools.partial(_kv_up_kernel, cfg),
        grid=(t // tm,),
        in_specs=[pl.BlockSpec((tm, r), lambda i: (i, col_c)),
                  pl.BlockSpec((tm, LANES), lambda i: (i, col_pe)),
                  pl.BlockSpec((1, r), lambda i: (0, 0)),
                  pl.BlockSpec((h, r, 256), lambda i: (0, 0, 0)),
                  pl.BlockSpec((1, LANES), lambda i: (0, 0)),
                  pl.BlockSpec((1, LANES), lambda i: (0, 0)),
                  pl.BlockSpec((tm, LANES), lambda i: (_pos_block(cfg, i, tm), 0))],
        out_specs=[pl.BlockSpec((h, tm, 256), lambda i: (0, i, 0)),
                   pl.BlockSpec((h, tm, 256), lambda i: (0, i, 0))],
        out_shape=[jax.ShapeDtypeStruct((h, t, 256), BF16),
                   jax.ShapeDtypeStruct((h, t, 256), BF16)],
        compiler_params=_params(("arbitrary",), vmem),
        name="kv_up",
    )(z, z, g_kva, w_kv, ga, gb, cs)


def _chunk_len(cfg):
    chunk = max(s for _, s in cfg.streams)
    for (b, s), off in zip(cfg.streams, cfg.tok_offsets):
        assert chunk % s == 0 and (b * s) % chunk == 0 and off % chunk == 0
    return chunk


def _in_stream_chunks(cfg, k, c, chunk):
    lo = cfg.tok_offsets[k] // chunk
    hi = lo + cfg.streams[k][0] * cfg.streams[k][1] // chunk
    return jnp.logical_and(c >= lo, c < hi)


def _attn_kernel(cfg, chunk, tq, tk, q_ref, k_ref, v_ref, o_ref):
    c = pl.program_id(0)
    qi = pl.program_id(2)
    for k, (_, s) in enumerate(cfg.streams):
        @pl.when(_in_stream_chunks(cfg, k, c, chunk))
        def _(s=s):
            start = ((qi * tq) // s) * s
            q = q_ref[0]
            m = jnp.full((tq, 1), -jnp.inf, F32)
            acc = jnp.zeros((tq, 2 * LANES), F32)
            for j in range(s // tk):
                rows = pl.ds(pl.multiple_of(start + j * tk, tk), tk)
                sc = lax.dot_general(q, k_ref[0, rows, :], (((1,), (1,)), ((), ())),
                                     preferred_element_type=F32)
                m_new = jnp.maximum(m, jnp.max(sc, axis=-1, keepdims=True))
                p = jnp.exp2((sc - m_new).astype(BF16))
                acc = jnp.exp2(m - m_new) * acc + jnp.dot(p, v_ref[0, rows, :], preferred_element_type=F32)
                m = m_new
            o_ref[...] = (acc[:, :LANES] / acc[:, LANES:LANES + 1]).astype(o_ref.dtype)


def _attn(cfg, q, k, v):
    chunk = _chunk_len(cfg)
    h = cfg.n_heads
    t = cfg.n_tokens
    s_min = min(s for _, s in cfg.streams)
    tq = _tile(s_min, 512)
    tk = _tile(s_min, 512)
    nq = chunk // tq
    vmem = 4 * tq * 256 * 2 + 2 * chunk * 512 * 2 + 6 * tq * tk * 4 + 6 * tq * 256 * 4 + (8 << 20)
    return pl.pallas_call(
        functools.partial(_attn_kernel, cfg, chunk, tq, tk),
        grid=(t // chunk, h, nq),
        in_specs=[pl.BlockSpec((1, tq, 256), lambda ci, hi, qi: (hi, ci * nq + qi, 0)),
                  pl.BlockSpec((1, chunk, 256), lambda ci, hi, qi: (hi, ci, 0)),
                  pl.BlockSpec((1, chunk, 256), lambda ci, hi, qi: (hi, ci, 0))],
        out_specs=pl.BlockSpec((tq, LANES), lambda ci, hi, qi: (ci * nq + qi, hi)),
        out_shape=jax.ShapeDtypeStruct((t, h * LANES), BF16),
        compiler_params=_params(("arbitrary", "arbitrary", "arbitrary"), vmem),
        name="attn",
    )(q, k, v)


def _fold_kernel(cc_ref, sc_ref, w_ref, o_ref):
    w = w_ref[0]
    o_ref[0, :, :256] = jnp.dot(cc_ref[...], w, preferred_element_type=F32,
                                precision=lax.Precision.HIGHEST).astype(o_ref.dtype)
    o_ref[0, :, 256:] = jnp.dot(sc_ref[...], w, preferred_element_type=F32,
                                precision=lax.Precision.HIGHEST).astype(o_ref.dtype)


def _fold_fourier_weights(cfg, cc, sc, w_fmix):
    g, c = cfg.f_groups, cfg.f_gdim
    return pl.pallas_call(
        _fold_kernel,
        grid=(g,),
        in_specs=[pl.BlockSpec((c, c), lambda i: (0, 0)),
                  pl.BlockSpec((c, c), lambda i: (0, 0)),
                  pl.BlockSpec((1, c, c), lambda i: (i, 0, 0))],
        out_specs=pl.BlockSpec((1, c, 2 * c), lambda i: (i, 0, 0)),
        out_shape=jax.ShapeDtypeStruct((g, c, 2 * c), BF16),
        compiler_params=_params(("arbitrary",), 16 << 20),
        name="fold_fourier",
    )(cc, sc, w_fmix)


def _four1_kernel(cfg, f_ref, w_ref, o_ref):
    c = cfg.f_gdim
    for g in range(cfg.f_groups):
        y = jnp.dot(f_ref[:, g * c:(g + 1) * c], w_ref[g], preferred_element_type=F32)
        o_ref[0, :, g * c:(g + 1) * c] = y[:, :c].astype(o_ref.dtype)
        o_ref[1, :, g * c:(g + 1) * c] = y[:, c:].astype(o_ref.dtype)


def _four1(cfg, z, w_fold, tm):
    t = cfg.n_tokens
    fw = cfg.f_width
    g, c = cfg.f_groups, cfg.f_gdim
    vmem = 2 * tm * fw * 2 + 2 * g * c * 2 * c * 2 + 4 * tm * fw * 2 + (8 << 20)
    return pl.pallas_call(
        functools.partial(_four1_kernel, cfg),
        grid=(t // tm,),
        in_specs=[pl.BlockSpec((tm, fw), lambda i: (i, 0)),
                  pl.BlockSpec((g, c, 2 * c), lambda i: (0, 0, 0))],
        out_specs=pl.BlockSpec((2, tm, fw), lambda i: (0, i, 0)),
        out_shape=jax.ShapeDtypeStruct((2, t, fw), BF16),
        compiler_params=_params(("arbitrary",), vmem),
        name="four1",
    )(z, w_fold)


def _four2_kernel(cfg, chunk, tm, *refs):
    ns = len(cfg.streams)
    ab_ref, o_ref = refs[2 * ns:]
    c = pl.program_id(0)
    mi = pl.program_id(2)
    for k, (_, s) in enumerate(cfg.streams):
        @pl.when(_in_stream_chunks(cfg, k, c, chunk))
        def _(k=k, s=s):
            start = pl.multiple_of(((mi * tm) // s) * s, s)
            y = jnp.dot(refs[2 * k][...], ab_ref[0, pl.ds(start, s), :], preferred_element_type=F32)
            y = y + jnp.dot(refs[2 * k + 1][...], ab_ref[1, pl.ds(start, s), :], preferred_element_type=F32)
            o_ref[...] = y.astype(o_ref.dtype)


def _four2(cfg, dfts, ab):
    chunk = _chunk_len(cfg)
    t = cfg.n_tokens
    fw = cfg.f_width
    tm = _tile(min(s for _, s in cfg.streams), 512)
    tn = _tile(fw, 512)
    nm = chunk // tm

    def d_spec(k):
        s = cfg.streams[k][1]

        def imap(ci, ni, mi):
            inside = _in_stream_chunks(cfg, k, ci, chunk)
            before = ci < cfg.tok_offsets[k] // chunk
            return (jnp.where(inside, ((mi * tm) % s) // tm, jnp.where(before, 0, s // tm - 1)), 0)
        return pl.BlockSpec((tm, s), imap)

    in_specs, args, vmem = [], [], 0
    for k, (dc, ds) in enumerate(dfts):
        in_specs += [d_spec(k), d_spec(k)]
        args += [dc, ds]
        vmem += 4 * tm * cfg.streams[k][1] * 2
    in_specs.append(pl.BlockSpec((2, chunk, tn), lambda ci, ni, mi: (0, ci, ni)))
    vmem += 4 * chunk * tn * 2 + 2 * tm * tn * 2 + 2 * tm * tn * 4 + (8 << 20)
    return pl.pallas_call(
        functools.partial(_four2_kernel, cfg, chunk, tm),
        grid=(t // chunk, fw // tn, nm),
        in_specs=in_specs,
        out_specs=pl.BlockSpec((tm, tn), lambda ci, ni, mi: (ci * nm + mi, ni)),
        out_shape=jax.ShapeDtypeStruct((t, fw), BF16),
        compiler_params=_params(("arbitrary", "arbitrary", "arbitrary"), vmem),
        name="four2",
    )(*args, ab)


def _out_proj_kernel(cfg, tm, *refs):
    ns = len(cfg.streams)
    x_refs = refs[:ns]
    gt_ref, a_ref, f_ref, wa_ref, wb_ref, o_ref = refs[ns:]
    i = pl.program_id(0)
    mix = jnp.dot(a_ref[...], wa_ref[...], preferred_element_type=F32)
    mix = mix + jnp.dot(f_ref[...], wb_ref[...], preferred_element_type=F32)
    for k in range(ns):
        @pl.when(_in_stream(cfg, k, i, tm))
        def _(k=k):
            o_ref[...] = x_refs[k][...] + gt_ref[0] * mix


def _out_proj(cfg, xs, mod3, o_attn, o_four, wa, wb):
    d = cfg.d_model
    t = cfg.n_tokens
    aw, fw = cfg.attn_width, cfg.f_width
    tm = _tile(min(s for _, s in cfg.streams), 1024)
    tn = _tile(d, 512)
    nj = d // tn

    def x_spec(k):
        def imap(i, j):
            inside = _in_stream(cfg, k, i, tm)
            before = i < cfg.tok_offsets[k] // tm
            return (_stream_block(cfg, k, i, tm), jnp.where(inside, j, jnp.where(before, 0, nj - 1)))
        return pl.BlockSpec((tm, tn), imap)

    in_specs = [x_spec(k) for k in range(len(xs))] + [
        pl.BlockSpec((1, 1, tn), lambda i, j: (_mod_row(cfg, i * tm), 0, 2 * nj + j)),
        pl.BlockSpec((tm, aw), lambda i, j: (i, 0)),
        pl.BlockSpec((tm, fw), lambda i, j: (i, 0)),
        pl.BlockSpec((aw, tn), lambda i, j: (0, j)),
        pl.BlockSpec((fw, tn), lambda i, j: (0, j)),
    ]
    vmem = (2 * tm * tn * 4 * len(xs) + 2 * tm * (aw + fw) * 2 + 2 * (aw + fw) * tn * 2
            + 2 * tm * tn * 4 + 2 * tm * tn * 4 + (8 << 20))
    return pl.pallas_call(
        functools.partial(_out_proj_kernel, cfg, tm),
        grid=(t // tm, nj),
        in_specs=in_specs,
        out_specs=pl.BlockSpec((tm, tn), lambda i, j: (i, j)),
        out_shape=jax.ShapeDtypeStruct((t, d), F32),
        compiler_params=_params(("arbitrary", "arbitrary"), vmem),
        name="out_proj",
    )(*xs, mod3, o_attn, o_four, wa, wb)


def _router_kernel(cfg, tm, x_ref, sh_ref, sc_ref, g_ref, wr_ref, tri_ref,
                   hp_ref, mi_ref, gt_ref, cnt_ref, carry_ref):
    i = pl.program_id(0)
    ng, ne = cfg.n_groups, cfg.e_per_group
    n_exp = cfg.n_experts

    @pl.when(i == 0)
    def _():
        carry_ref[...] = jnp.zeros_like(carry_ref)

    h2 = _rms(x_ref[...], cfg.eps) * g_ref[...] * (1.0 + sc_ref[0]) + sh_ref[0]
    for c in range(cfg.d_model // PACK_CHUNK):
        lo = h2[:, c * PACK_CHUNK:c * PACK_CHUNK + LANES]
        hi = h2[:, c * PACK_CHUNK + LANES:(c + 1) * PACK_CHUNK]
        hp_ref[:, c * LANES:(c + 1) * LANES] = _pack_words(lo, hi)

    h_hi = h2.astype(BF16)
    h_lo = (h2 - h_hi.astype(F32)).astype(BF16)
    p1 = jnp.dot(h_hi, wr_ref[...], preferred_element_type=F32)
    p2 = jnp.dot(h_lo, wr_ref[:, :LANES], preferred_element_type=F32)
    logits = (p1[:, :LANES] + p1[:, LANES:] + p2).T

    gl = logits[0:ng]
    io = lax.broadcasted_iota(I32, (ng, tm), 0)
    gm = jnp.max(gl, axis=0, keepdims=True)
    p_group = 1.0 / jnp.sum(jnp.exp(gl - gm), axis=0, keepdims=True)
    gidx = jnp.min(jnp.where(gl == gm, io, ng), axis=0, keepdims=True)
    sel = jnp.zeros((ne, tm), F32)
    for g in range(ng):
        sel = jnp.where(gidx == g, logits[ng + g * ne:ng + (g + 1) * ne], sel)
    ie = lax.broadcasted_iota(I32, (ne, tm), 0)
    m1 = jnp.max(sel, axis=0, keepdims=True)
    i1 = jnp.min(jnp.where(sel == m1, ie, ne), axis=0, keepdims=True)
    sel2 = jnp.where(ie == i1, -jnp.inf, sel)
    m2 = jnp.max(sel2, axis=0, keepdims=True)
    i2 = jnp.min(jnp.where(sel2 == m2, ie, ne), axis=0, keepdims=True)
    e21 = jnp.exp(m2 - m1)
    gate0 = p_group / (1.0 + e21)
    gate1 = p_group * e21 / (1.0 + e21)
    e0 = gidx * ne + i1
    e1 = gidx * ne + i2

    ix = lax.broadcasted_iota(I32, (n_exp, tm), 0)
    hit0 = ix == e0
    hit1 = ix == e1
    member = jnp.logical_or(hit0, hit1).astype(F32)
    before = jnp.dot(member.astype(BF16), tri_ref[...], preferred_element_type=F32)
    total = before + carry_ref[:, 0:1]
    r0 = jnp.sum(jnp.where(hit0, total, 0.0), axis=0, keepdims=True).astype(I32)
    r1 = jnp.sum(jnp.where(hit1, total, 0.0), axis=0, keepdims=True).astype(I32)
    carry_ref[...] = carry_ref[...] + jnp.sum(member, axis=1, keepdims=True)
    cnt_ref[...] = carry_ref[...].astype(I32)

    i8 = lax.broadcasted_iota(I32, (8, tm), 0)
    mi_ref[...] = jnp.where(i8 == 0, e0, jnp.where(i8 == 1, e1, jnp.where(i8 == 2, r0, jnp.where(i8 == 3, r1, 0))))
    il = lax.broadcasted_iota(I32, (LANES, tm), 0)
    gt_ref[...] = jnp.where(il == 0, gate0, jnp.where(il == 1, gate1, 0.0)).T


def _router(cfg, x1, mod3, g_ffn, wr_cat, tri, tm):
    d = cfg.d_model
    t = cfg.n_tokens
    ne = cfg.n_experts
    vmem = 2 * tm * d * 4 + 2 * tm * d * 2 + 2 * d * 256 * 2 + 6 * tm * d * 4 + 2 * tm * tm * 2 + (8 << 20)
    return pl.pallas_call(
        functools.partial(_router_kernel, cfg, tm),
        grid=(t // tm,),
        in_specs=[pl.BlockSpec((tm, d), lambda i: (i, 0)),
                  pl.BlockSpec((1, 1, d), lambda i: (_mod_row(cfg, i * tm), 0, 3)),
                  pl.BlockSpec((1, 1, d), lambda i: (_mod_row(cfg, i * tm), 0, 4)),
                  pl.BlockSpec((1, d), lambda i: (0, 0)),
                  pl.BlockSpec((d, 256), lambda i: (0, 0)),
                  pl.BlockSpec((tm, tm), lambda i: (0, 0))],
        out_specs=[pl.BlockSpec((tm, d // 2), lambda i: (i, 0)),
                   pl.BlockSpec((8, tm), lambda i: (0, i)),
                   pl.BlockSpec((tm, LANES), lambda i: (i, 0)),
                   pl.BlockSpec((ne, LANES), lambda i: (0, 0))],
        out_shape=[jax.ShapeDtypeStruct((t, d // 2), U32),
                   jax.ShapeDtypeStruct((8, t), I32),
                   jax.ShapeDtypeStruct((t, LANES), F32),
                   jax.ShapeDtypeStruct((ne, LANES), I32)],
        scratch_shapes=[pltpu.VMEM((ne, LANES), F32)],
        compiler_params=_params(("arbitrary",), vmem),
        name="router",
    )(x1, mod3, mod3, g_ffn, wr_cat, tri)


def _dest_row(start_ref, meta_ref, kk, r):
    return start_ref[meta_ref[kk, r]] + meta_ref[2 + kk, r]


def _dispatch_kernel(tm, start_ref, meta_ref, h_ref, xs_in_ref, xs_ref, sem):
    del xs_in_ref

    def copy(r, kk):
        return pltpu.make_async_copy(h_ref.at[pl.ds(r, 1)],
                                     xs_ref.at[pl.ds(_dest_row(start_ref, meta_ref, kk, r), 1)], sem)

    def issue(r, carry):
        for kk in range(2):
            copy(r, kk).start()
        return carry

    def drain(r, carry):
        for kk in range(2):
            copy(r, kk).wait()
        return carry

    lax.fori_loop(0, tm, issue, 0)
    lax.fori_loop(0, tm, drain, 0)


def _dispatch(cfg, pad_start, meta, h2p, xs_init, tm):
    t = cfg.n_tokens
    w = h2p.shape[1]
    grid_spec = pltpu.PrefetchScalarGridSpec(
        num_scalar_prefetch=1,
        grid=(t // tm,),
        in_specs=[pl.BlockSpec((8, tm), lambda i, st: (0, i), memory_space=pltpu.SMEM),
                  pl.BlockSpec((tm, w), lambda i, st: (i, 0)),
                  pl.BlockSpec(memory_space=pl.ANY)],
        out_specs=pl.BlockSpec(memory_space=pl.ANY),
        scratch_shapes=[pltpu.SemaphoreType.DMA(())],
    )
    return pl.pallas_call(
        functools.partial(_dispatch_kernel, tm),
        grid_spec=grid_spec,
        out_shape=jax.ShapeDtypeStruct(xs_init.shape, xs_init.dtype),
        input_output_aliases={3: 0},
        compiler_params=_params(("arbitrary",), 4 * tm * w * 4 + (4 << 20)),
        name="dispatch",
    )(pad_start, meta, h2p, xs_init)


def _moe1_kernel(cfg, be_ref, nv_ref, x_ref, w1_ref, w3_ref, o_ref, xb_ref):
    del be_ref
    i = pl.program_id(1)

    @pl.when(i < nv_ref[0])
    def _():
        for c in range(cfg.d_model // PACK_CHUNK):
            lo, hi = _unpack_words(x_ref[:, c * LANES:(c + 1) * LANES])
            xb_ref[:, c * PACK_CHUNK:c * PACK_CHUNK + LANES] = lo.astype(BF16)
            xb_ref[:, c * PACK_CHUNK + LANES:(c + 1) * PACK_CHUNK] = hi.astype(BF16)
        x = xb_ref[...]
        a = jnp.dot(x, w1_ref[0], preferred_element_type=F32)
        b = jnp.dot(x, w3_ref[0], preferred_element_type=F32)
        o_ref[...] = (a * jax.nn.sigmoid(a) * b).astype(o_ref.dtype)

    @pl.when(i >= nv_ref[0])
    def _():
        o_ref[...] = jnp.zeros_like(o_ref)


def _moe1(cfg, blk_e, n_valid, xs, w1, w3):
    d, f = cfg.d_model, cfg.d_expert
    bm = cfg.moe_block
    rows = xs.shape[0]
    nb = rows // bm
    tf = _tile(f, 512)

    def eff(i, nv):
        return jnp.minimum(i, nv[0] - 1)

    grid_spec = pltpu.PrefetchScalarGridSpec(
        num_scalar_prefetch=2,
        grid=(f // tf, nb),
        in_specs=[pl.BlockSpec((bm, d // 2), lambda j, i, be, nv: (eff(i, nv), 0)),
                  pl.BlockSpec((1, d, tf), lambda j, i, be, nv: (be[eff(i, nv)], 0, j)),
                  pl.BlockSpec((1, d, tf), lambda j, i, be, nv: (be[eff(i, nv)], 0, j))],
        out_specs=pl.BlockSpec((bm, tf), lambda j, i, be, nv: (i, j)),
        scratch_shapes=[pltpu.VMEM((bm, d), BF16)],
    )
    vmem = 2 * bm * d * 2 + 4 * d * tf * 2 + bm * d * 2 + 2 * bm * tf * 2 + 4 * bm * tf * 4 + (8 << 20)
    return pl.pallas_call(
        functools.partial(_moe1_kernel, cfg),
        grid_spec=grid_spec,
        out_shape=jax.ShapeDtypeStruct((rows, f), BF16),
        compiler_params=_params(("arbitrary", "arbitrary"), vmem),
        name="moe1",
    )(blk_e, n_valid, xs, w1, w3)


def _moe2_kernel(cfg, be_ref, nv_ref, h_ref, w2_ref, o_ref):
    del be_ref
    i = pl.program_id(0)

    @pl.when(i < nv_ref[0])
    def _():
        y = jnp.dot(h_ref[...], w2_ref[0], preferred_element_type=F32)
        for c in range(cfg.d_model // PACK_CHUNK):
            o_ref[:, c * LANES:(c + 1) * LANES] = _pack_words(
                y[:, c * PACK_CHUNK:c * PACK_CHUNK + LANES], y[:, c * PACK_CHUNK + LANES:(c + 1) * PACK_CHUNK])

    @pl.when(i >= nv_ref[0])
    def _():
        o_ref[...] = jnp.zeros_like(o_ref)


def _moe2(cfg, blk_e, n_valid, hmid, w2):
    d, f = cfg.d_model, cfg.d_expert
    bm = cfg.moe_block
    rows = hmid.shape[0]

    def eff(i, nv):
        return jnp.minimum(i, nv[0] - 1)

    grid_spec = pltpu.PrefetchScalarGridSpec(
        num_scalar_prefetch=2,
        grid=(rows // bm,),
        in_specs=[pl.BlockSpec((bm, f), lambda i, be, nv: (eff(i, nv), 0)),
                  pl.BlockSpec((1, f, d), lambda i, be, nv: (be[eff(i, nv)], 0, 0))],
        out_specs=pl.BlockSpec((bm, d // 2), lambda i, be, nv: (i, 0)),
    )
    vmem = 2 * bm * f * 2 + 2 * f * d * 2 + 2 * bm * d * 2 + 2 * bm * d * 4 + (8 << 20)
    return pl.pallas_call(
        functools.partial(_moe2_kernel, cfg),
        grid_spec=grid_spec,
        out_shape=jax.ShapeDtypeStruct((rows, d // 2), U32),
        compiler_params=_params(("arbitrary",), vmem),
        name="moe2",
    )(blk_e, n_valid, hmid, w2)


def _combine_kernel(cfg, tm, start_ref, meta_ref, x_ref, gt2_ref, gate_ref, y_ref, o_ref, buf_ref, sem):
    def copy(r, kk):
        return pltpu.make_async_copy(y_ref.at[pl.ds(_dest_row(start_ref, meta_ref, kk, r), 1)],
                                     buf_ref.at[kk, pl.ds(r, 1)], sem)

    def issue(r, carry):
        for kk in range(2):
            copy(r, kk).start()
        return carry

    def drain(r, carry):
        for kk in range(2):
            copy(r, kk).wait()
        return carry

    lax.fori_loop(0, tm, issue, 0)
    lax.fori_loop(0, tm, drain, 0)

    g0 = gate_ref[:, 0:1]
    g1 = gate_ref[:, 1:2]
    for c in range(cfg.d_model // PACK_CHUNK):
        lo0, hi0 = _unpack_words(buf_ref[0, :, c * LANES:(c + 1) * LANES])
        lo1, hi1 = _unpack_words(buf_ref[1, :, c * LANES:(c + 1) * LANES])
        sl_lo = slice(c * PACK_CHUNK, c * PACK_CHUNK + LANES)
        sl_hi = slice(c * PACK_CHUNK + LANES, (c + 1) * PACK_CHUNK)
        o_ref[:, sl_lo] = x_ref[:, sl_lo] + gt2_ref[0, :, sl_lo] * (g0 * lo0 + g1 * lo1)
        o_ref[:, sl_hi] = x_ref[:, sl_hi] + gt2_ref[0, :, sl_hi] * (g0 * hi0 + g1 * hi1)


def _combine(cfg, k_idx, pad_start, meta, x1, mod3, gates, y, tm):
    b, s = cfg.streams[k_idx]
    off = cfg.tok_offsets[k_idx]
    d = cfg.d_model
    n = b * s
    ob = off // tm
    row0 = cfg.row_offsets[k_idx]
    grid_spec = pltpu.PrefetchScalarGridSpec(
        num_scalar_prefetch=1,
        grid=(n // tm,),
        in_specs=[pl.BlockSpec((8, tm), lambda i, st: (0, ob + i), memory_space=pltpu.SMEM),
                  pl.BlockSpec((tm, d), lambda i, st: (ob + i, 0)),
                  pl.BlockSpec((1, 1, d), lambda i, st: (row0 + (i * tm) // s, 0, 5)),
                  pl.BlockSpec((tm, LANES), lambda i, st: (ob + i, 0)),
                  pl.BlockSpec(memory_space=pl.ANY)],
        out_specs=pl.BlockSpec((tm, d), lambda i, st: (i, 0)),
        scratch_shapes=[pltpu.VMEM((2, tm, d // 2), U32), pltpu.SemaphoreType.DMA(())],
    )
    vmem = 4 * tm * d * 4 + 2 * tm * d * 2 + 2 * tm * 128 * 4 + 6 * tm * 128 * 4 + (8 << 20)
    return pl.pallas_call(
        functools.partial(_combine_kernel, cfg, tm),
        grid_spec=grid_spec,
        out_shape=jax.ShapeDtypeStruct((n, d), F32),
        compiler_params=_params(("arbitrary",), vmem),
        name=f"combine{k_idx}",
    )(pad_start, meta, x1, mod3, gates, y)


def _rot_half(a, axis=-1):
    h = a.shape[axis] // 2
    lo = lax.slice_in_dim(a, 0, h, axis=axis)
    hi = lax.slice_in_dim(a, h, 2 * h, axis=axis)
    return jnp.concatenate([hi, lo], axis=axis)


def _rope_table(cfg, n_pos):
    half = cfg.rope // 2
    inv_freq = cfg.theta ** (-2.0 * jnp.arange(half, dtype=F32) / cfg.rope)
    ang = jnp.arange(n_pos, dtype=F32)[:, None] * inv_freq[None, :]
    cos, sin = jnp.cos(ang), jnp.sin(ang)
    return jnp.concatenate([cos, cos, -sin, sin], axis=-1)


def _dft(n, scale):
    idx = jnp.arange(n, dtype=I32)
    ang = ((idx[:, None] * idx[None, :]) % n).astype(F32) * (2.0 * math.pi / n)
    return jnp.cos(ang) * scale, jnp.sin(ang) * scale


def _layer(cfg, xs, cs_, w_ada, b_ada, g_attn, w_in, g_qa, w_uq, g_kva, w_ukv, g_qn, g_kn,
           w_fmix, w_o, g_ffn, w_group, w_route, w1, w3, w2):
    d, h = cfg.d_model, cfg.n_heads
    t = cfg.n_tokens
    nope, rope = cfg.nope, cfg.rope
    assert nope == LANES and cfg.v_dim == LANES and rope == 64 and cfg.f_gdim == PACK_CHUNK
    assert cfg.f_width % cfg.q_rank == 0 and (cfg.f_width + cfg.q_rank) % cfg.kv_rank == 0
    qr, kr = cfg.q_rank, cfg.kv_rank

    wq_in, wkv_in, wpe_in, wf_in = jnp.split(w_in, [qr, qr + kr, qr + kr + rope], axis=1)
    w_cat = jnp.concatenate([wf_in, wq_in, wkv_in, wpe_in, _rot_half(wpe_in)], axis=1)
    n_cat = w_cat.shape[1]
    tn_in = 768 if n_cat > 768 else n_cat
    w_cat = jnp.pad(w_cat, ((0, 0), (0, -n_cat % tn_in))).astype(BF16)

    wq3 = w_uq.reshape(qr, h, nope + rope)
    w_q = jnp.concatenate([wq3, _rot_half(wq3[..., nope:])], axis=-1).transpose(1, 0, 2).astype(BF16)
    w_kv = w_ukv.reshape(kr, h, nope + cfg.v_dim).transpose(1, 0, 2).astype(BF16)
    ga_q, gb_q = g_qn[None, :nope], jnp.concatenate([g_qn[nope:], _rot_half(g_qn[nope:])])[None]
    ga_k, gb_k = g_kn[None, :nope], jnp.concatenate([g_kn[nope:], _rot_half(g_kn[nope:])])[None]
    wa, wb = w_o[:cfg.attn_width].astype(BF16), w_o[cfg.attn_width:].astype(BF16)
    wr = jnp.concatenate([w_group, w_route.reshape(d, cfg.n_experts)], axis=1)
    wr = jnp.pad(wr, ((0, 0), (0, LANES - wr.shape[1])))
    wr_hi = wr.astype(BF16)
    wr_cat = jnp.concatenate([wr_hi, (wr - wr_hi.astype(F32)).astype(BF16)], axis=1)
    w1b, w3b, w2b = w1.astype(BF16), w3.astype(BF16), w2.astype(BF16)

    s_max = max(s for _, s in cfg.streams)
    cs_tab = _rope_table(cfg, s_max)
    cc, sc = _dft(cfg.f_gdim, cfg.f_gdim ** -0.5)

    c_all = jnp.concatenate(cs_, axis=0)
    rows = -(-c_all.shape[0] // 8) * 8
    c_pad = jnp.pad(c_all, ((0, rows - c_all.shape[0]), (0, 0)))
    mod3 = _ada(c_pad, w_ada.astype(BF16), b_ada[None]).reshape(rows, 1, 6 * d)

    x2d = [x.reshape(-1, d) for x in xs]
    z = _in_proj(cfg, x2d, mod3, g_attn[None], w_cat)
    tm = _tile(min(s for _, s in cfg.streams), 512)
    q = _q_up(cfg, z, g_qa[None], w_q, ga_q, gb_q, cs_tab, tm)
    k, v = _kv_up(cfg, z, g_kva[None], w_kv, ga_k, gb_k, cs_tab, tm)
    o_attn = _attn(cfg, q, k, v)

    w_fold = _fold_fourier_weights(cfg, cc, sc, w_fmix)
    ab = _four1(cfg, z, w_fold, tm)
    dfts = []
    for _, s in cfg.streams:
        dc, ds = _dft(s, s ** -0.5)
        dfts.append((dc.astype(BF16), (-ds).astype(BF16)))
    o_four = _four2(cfg, dfts, ab)

    x1 = _out_proj(cfg, x2d, mod3, o_attn, o_four, wa, wb)

    tm_r = _tile(tm, 256)
    tri = (jnp.arange(tm_r)[:, None] < jnp.arange(tm_r)[None, :]).astype(BF16)
    h2p, meta, gates, counts = _router(cfg, x1, mod3, g_ffn[None], wr_cat, tri, tm_r)

    bm = cfg.moe_block
    cnt = counts[:, 0]
    padded = (cnt + bm - 1) // bm * bm
    pad_end = jnp.cumsum(padded)
    pad_start = (pad_end - padded).astype(I32)
    n_blocks = (2 * t) // bm + cfg.n_experts
    blk_start = jnp.arange(n_blocks, dtype=I32) * bm
    blk_e = jnp.minimum(jnp.searchsorted(pad_end, blk_start, side="right"), cfg.n_experts - 1).astype(I32)
    n_valid = (pad_end[-1] // bm).astype(I32)[None]

    xs_rows = _dispatch(cfg, pad_start, meta, h2p, jnp.zeros((n_blocks * bm, d // 2), U32), _tile(tm, 256))
    hmid = _moe1(cfg, blk_e, n_valid, xs_rows, w1b, w3b)
    y = _moe2(cfg, blk_e, n_valid, hmid, w2b)

    outs = []
    for ki, (b, s) in enumerate(cfg.streams):
        o = _combine(cfg, ki, pad_start, meta, x1, mod3, gates, y, _tile(tm, 256))
        outs.append(o.reshape(b, s, d))
    return tuple(outs)


def kernel(x_prompt, x_sample, c_prompt, c_sample, w_ada, b_ada, g_attn, w_in, g_qa, w_uq, g_kva, w_ukv,
           g_qn, g_kn, w_fmix, w_o, g_ffn, w_group, w_route, w1, w3, w2):
    cfg = Cfg()
    xs = (x_prompt, x_sample)
    cs_ = (c_prompt, c_sample)
    for l in range(w_ada.shape[0]):
        xs = _layer(cfg, xs, cs_, w_ada[l], b_ada[l], g_attn[l], w_in[l], g_qa[l], w_uq[l], g_kva[l],
                    w_ukv[l], g_qn[l], g_kn[l], w_fmix[l], w_o[l], g_ffn[l], w_group[l], w_route[l],
                    w1[l], w3[l], w2[l])
    return xs
```

```python
import dataclasses
import functools
import math

import jax
import jax.numpy as jnp
from jax import lax
from jax.experimental import pallas as pl
from jax.experimental.pallas import tpu as pltpu

BF16 = jnp.bfloat16
F32 = jnp.float32
U32 = jnp.uint32
I32 = jnp.int32

LANES = 128
MXU_DIM = 256
VMEM_CAP = 60 << 20
PACK_CHUNK = 2 * LANES


@dataclasses.dataclass(frozen=True)
class Cfg:
    d_model: int = 4096
    streams: tuple = ((8, 2048), (4, 4096))
    n_heads: int = 16
    nope: int = 128
    rope: int = 64
    v_dim: int = 128
    q_rank: int = 1024
    kv_rank: int = 512
    f_gdim: int = 256
    n_groups: int = 8
    e_per_group: int = 8
    d_expert: int = 1024
    theta: float = 10000.0
    eps: float = 1e-6
    moe_block: int = 512
    moe_up_tile: int = 512
    moe_down_tile: int = 2048

    @property
    def attn_width(self):
        return self.n_heads * self.v_dim

    @property
    def f_width(self):
        return self.d_model - self.attn_width

    @property
    def f_groups(self):
        return self.f_width // self.f_gdim

    @property
    def qk_dim(self):
        return self.nope + self.rope

    @property
    def n_experts(self):
        return self.n_groups * self.e_per_group

    @property
    def tok_offsets(self):
        offs, t = [], 0
        for b, s in self.streams:
            offs.append(t)
            t += b * s
        return tuple(offs)

    @property
    def row_offsets(self):
        offs, r = [], 0
        for b, _ in self.streams:
            offs.append(r)
            r += b
        return tuple(offs)

    @property
    def n_tokens(self):
        return sum(b * s for b, s in self.streams)

    @property
    def n_rows(self):
        return sum(b for b, _ in self.streams)


def _tile(n, want):
    if n <= want:
        return n
    t = want
    while n % t:
        t -= 8
    return t


def _params(sem, vmem_bytes):
    return pltpu.CompilerParams(dimension_semantics=sem,
                                vmem_limit_bytes=int(min(VMEM_CAP, vmem_bytes)))


def _mod_row(cfg, t0):
    row = None
    for k, (b, s) in enumerate(cfg.streams):
        expr = cfg.row_offsets[k] + (t0 - cfg.tok_offsets[k]) // s
        row = expr if row is None else jnp.where(t0 >= cfg.tok_offsets[k], expr, row)
    return row


def _pos_block(cfg, i, tm):
    t0 = i * tm
    blk = None
    for k, (b, s) in enumerate(cfg.streams):
        expr = ((t0 - cfg.tok_offsets[k]) % s) // tm
        blk = expr if blk is None else jnp.where(t0 >= cfg.tok_offsets[k], expr, blk)
    return blk


def _stream_block(cfg, k, i, tm):
    nb = cfg.streams[k][0] * cfg.streams[k][1] // tm
    return jnp.clip(i - cfg.tok_offsets[k] // tm, 0, nb - 1)


def _in_stream(cfg, k, i, tm):
    lo = cfg.tok_offsets[k] // tm
    hi = lo + cfg.streams[k][0] * cfg.streams[k][1] // tm
    return jnp.logical_and(i >= lo, i < hi)


def _rms(x, eps):
    return x * lax.rsqrt(jnp.mean(x * x, axis=-1, keepdims=True) + eps)


def _pack_words(a, b):
    wa = lax.bitcast_convert_type(a.astype(BF16).astype(F32), U32) >> 16
    wb = lax.bitcast_convert_type(b.astype(BF16).astype(F32), U32) & jnp.uint32(0xFFFF0000)
    return wa | wb


def _unpack_words(w):
    lo = lax.bitcast_convert_type(w << 16, F32)
    hi = lax.bitcast_convert_type(w & jnp.uint32(0xFFFF0000), F32)
    return lo, hi


def _ada_kernel(c_ref, w_ref, b_ref, o_ref):
    c = c_ref[...]
    s = (c * jax.nn.sigmoid(c)).astype(BF16)
    o_ref[...] = jnp.dot(s, w_ref[...].astype(BF16), preferred_element_type=F32) + b_ref[...]


def _ada(c_pad, w_ada, b_ada):
    r, d = c_pad.shape
    n = w_ada.shape[1]
    tn = _tile(n, 512)
    return pl.pallas_call(
        _ada_kernel,
        grid=(n // tn,),
        in_specs=[pl.BlockSpec((r, d), lambda j: (0, 0)),
                  pl.BlockSpec((d, tn), lambda j: (0, j)),
                  pl.BlockSpec((1, tn), lambda j: (0, j))],
        out_specs=pl.BlockSpec((r, tn), lambda j: (0, j)),
        out_shape=jax.ShapeDtypeStruct((r, n), F32),
        compiler_params=_params(("arbitrary",), 2 * d * tn * 4 + d * tn * 2 + (8 << 20)),
        name="ada",
    )(c_pad, w_ada, b_ada)


def _in_proj_kernel(cfg, tm, n_blk, *refs):
    ns = len(cfg.streams)
    x_refs = refs[:ns]
    sh_ref, sc_ref, g_ref, w_ref, o_ref, h_ref, xbuf_ref, sem = refs[ns:]
    i = pl.program_id(0)

    def fetch(blk, slot, start):
        for k in range(ns):
            @pl.when(_in_stream(cfg, k, blk, tm))
            def _(k=k):
                row = pl.multiple_of((blk - cfg.tok_offsets[k] // tm) * tm, tm)
                cp = pltpu.make_async_copy(x_refs[k].at[pl.ds(row, tm)], xbuf_ref.at[slot], sem.at[slot])
                if start:
                    cp.start()
                else:
                    cp.wait()

    @pl.when(pl.program_id(1) == 0)
    def _():
        slot = i % 2

        @pl.when(i == 0)
        def _():
            fetch(i, slot, True)

        fetch(i, slot, False)

        @pl.when(i + 1 < n_blk)
        def _():
            fetch(i + 1, 1 - slot, True)

        y = _rms(xbuf_ref[slot], cfg.eps) * g_ref[...]
        h_ref[...] = (y * (1.0 + sc_ref[0]) + sh_ref[0]).astype(BF16)

    o_ref[...] = jnp.dot(h_ref[...], w_ref[...], preferred_element_type=F32).astype(o_ref.dtype)


def _in_proj(cfg, xs, mod3, g_attn, w_cat):
    d = cfg.d_model
    t = cfg.n_tokens
    n = w_cat.shape[1]
    tm = _tile(min(s for _, s in cfg.streams), 512)
    tn = _tile(n, 768)
    in_specs = [pl.BlockSpec(memory_space=pl.ANY) for _ in xs] + [
        pl.BlockSpec((1, 1, d), lambda i, j: (_mod_row(cfg, i * tm), 0, 0)),
        pl.BlockSpec((1, 1, d), lambda i, j: (_mod_row(cfg, i * tm), 0, 1)),
        pl.BlockSpec((1, d), lambda i, j: (0, 0)),
        pl.BlockSpec((d, tn), lambda i, j: (0, j)),
    ]
    vmem = 2 * tm * d * 4 + tm * d * 2 + 2 * d * tn * 2 + 2 * tm * tn * 2 + tm * tn * 4 + 2 * tm * d * 4
    return pl.pallas_call(
        functools.partial(_in_proj_kernel, cfg, tm, t // tm),
        grid=(t // tm, n // tn),
        in_specs=in_specs,
        out_specs=pl.BlockSpec((tm, tn), lambda i, j: (i, j)),
        out_shape=jax.ShapeDtypeStruct((t, n), BF16),
        scratch_shapes=[pltpu.VMEM((tm, d), BF16), pltpu.VMEM((2, tm, d), F32), pltpu.SemaphoreType.DMA((2,))],
        compiler_params=_params(("arbitrary", "arbitrary"), vmem + (6 << 20)),
        name="in_proj",
    )(*xs, mod3, mod3, g_attn, w_cat)


def _rope_half(t):
    lane = lax.broadcasted_iota(I32, t.shape, 1)
    return jnp.where(lane < 64, t + pltpu.roll(t, 64, axis=1), 0.0)


def _q_up_kernel(cfg, c_ref, gl_ref, w_ref, ga_ref, gb_ref, cs_ref, o_ref):
    cn = (_rms(c_ref[...].astype(F32), cfg.eps) * gl_ref[...]).astype(BF16)
    ga = ga_ref[...]
    gbcs = gb_ref[...] * cs_ref[...]
    for h in range(cfg.n_heads):
        y = jnp.dot(cn, w_ref[h], preferred_element_type=F32)
        a = y[:, :LANES]
        b = y[:, LANES:]
        aa = a * a
        ssq2 = jnp.sum((aa + b * b) + aa, axis=-1, keepdims=True)
        s = lax.rsqrt(ssq2 + 2.0 * cfg.qk_dim * cfg.eps)
        t = b * gbcs
        o_ref[h, :, :LANES] = (a * ga * s).astype(o_ref.dtype)
        o_ref[h, :, LANES:] = ((t + pltpu.roll(t, 64, axis=1)) * s).astype(o_ref.dtype)


def _q_up(cfg, z, g_qa, w_q, ga, gb, cs, tm):
    t = cfg.n_tokens
    h = cfg.n_heads
    r = cfg.q_rank
    col = cfg.f_width // r
    vmem = 2 * tm * r * 2 + 2 * h * r * 256 * 2 + 4 * h * tm * 256 * 2 + tm * 128 * 8 + tm * r * 8 + (8 << 20)
    return pl.pallas_call(
        functools.partial(_q_up_kernel, cfg),
        grid=(t // tm,),
        in_specs=[pl.BlockSpec((tm, r), lambda i: (i, col)),
                  pl.BlockSpec((1, r), lambda i: (0, 0)),
                  pl.BlockSpec((h, r, 256), lambda i: (0, 0, 0)),
                  pl.BlockSpec((1, LANES), lambda i: (0, 0)),
                  pl.BlockSpec((1, LANES), lambda i: (0, 0)),
                  pl.BlockSpec((tm, LANES), lambda i: (_pos_block(cfg, i, tm), 0))],
        out_specs=pl.BlockSpec((h, tm, 256), lambda i: (0, i, 0)),
        out_shape=jax.ShapeDtypeStruct((h, t, 256), BF16),
        compiler_params=_params(("arbitrary",), vmem),
        name="q_up",
    )(z, g_qa, w_q, ga, gb, cs)


def _kv_up_kernel(cfg, c_ref, pe_ref, gl_ref, w_ref, ga_ref, gb_ref, cs_ref, k_ref, v_ref):
    cn = (_rms(c_ref[...].astype(F32), cfg.eps) * gl_ref[...]).astype(BF16)
    pe = pe_ref[...].astype(F32)
    lane = lax.broadcasted_iota(I32, pe.shape, 1)
    ssq_pe = jnp.sum(jnp.where(lane < 64, pe * pe, 0.0), axis=-1, keepdims=True)
    kr = _rope_half(pe * gb_ref[...] * cs_ref[...])
    ga = ga_ref[...]
    for h in range(cfg.n_heads):
        y = jnp.dot(cn, w_ref[h], preferred_element_type=F32)
        kn = y[:, :LANES]
        s = lax.rsqrt((jnp.sum(kn * kn, axis=-1, keepdims=True) + ssq_pe) * (1.0 / cfg.qk_dim) + cfg.eps)
        k_ref[h, :, :LANES] = (kn * ga * s).astype(k_ref.dtype)
        k_ref[h, :, LANES:] = (kr * s).astype(k_ref.dtype)
        v_ref[h, :, :LANES] = y[:, LANES:].astype(v_ref.dtype)
        v_ref[h, :, LANES:] = jnp.where(lane == 0, 1.0, 0.0).astype(v_ref.dtype)


def _kv_up(cfg, z, g_kva, w_kv, ga, gb, cs, tm):
    t = cfg.n_tokens
    h = cfg.n_heads
    r = cfg.kv_rank
    col_c = (cfg.f_width + cfg.q_rank) // r
    col_pe = (cfg.f_width + cfg.q_rank + cfg.kv_rank) // LANES
    vmem = 2 * tm * (r + 128) * 2 + 2 * h * r * 256 * 2 + 4 * h * tm * 512 * 2 + tm * r * 8 + (8 << 20)
    return pl.pallas_call(
        functools.partial(_kv_up_kernel, cfg),
        grid=(t // tm,),
        in_specs=[pl.BlockSpec((tm, r), lambda i: (i, col_c)),
                  pl.BlockSpec((tm, LANES), lambda i: (i, col_pe)),
                  pl.BlockSpec((1, r), lambda i: (0, 0)),
                  pl.BlockSpec((h, r, 256), lambda i: (0, 0, 0)),
                  pl.BlockSpec((1, LANES), lambda i: (0, 0)),
                  pl.BlockSpec((1, LANES), lambda i: (0, 0)),
                  pl.BlockSpec((tm, LANES), lambda i: (_pos_block(cfg, i, tm), 0))],
        out_specs=[pl.BlockSpec((h, tm, 256), lambda i: (0, i, 0)),
                   pl.BlockSpec((h, tm, 256), lambda i: (0, i, 0))],
        out_shape=[jax.ShapeDtypeStruct((h, t, 256), BF16),
                   jax.ShapeDtypeStruct((h, t, 256), BF16)],
        compiler_params=_params(("arbitrary",), vmem),
        name="kv_up",
    )(z, z, g_kva, w_kv, ga, gb, cs)


def _chunk_len(cfg):
    chunk = max(s for _, s in cfg.streams)
    for (b, s), off in zip(cfg.streams, cfg.tok_offsets):
        assert chunk % s == 0 and (b * s) % chunk == 0 and off % chunk == 0
    return chunk


def _in_stream_chunks(cfg, k, c, chunk):
    lo = cfg.tok_offsets[k] // chunk
    hi = lo + cfg.streams[k][0] * cfg.streams[k][1] // chunk
    return jnp.logical_and(c >= lo, c < hi)


def _attn_kernel(cfg, chunk, tq, tk, q_ref, k_ref, v_ref, o_ref):
    c = pl.program_id(0)
    qi = pl.program_id(2)
    for k, (_, s) in enumerate(cfg.streams):
        @pl.when(_in_stream_chunks(cfg, k, c, chunk))
        def _(s=s):
            start = ((qi * tq) // s) * s
            q = q_ref[0]
            m = jnp.full((tq, 1), -jnp.inf, F32)
            acc = jnp.zeros((tq, 2 * LANES), F32)
            for j in range(s // tk):
                rows = pl.ds(pl.multiple_of(start + j * tk, tk), tk)
                sc = lax.dot_general(q, k_ref[0, rows, :], (((1,), (1,)), ((), ())),
                                     preferred_element_type=F32)
                m_new = jnp.maximum(m, jnp.max(sc, axis=-1, keepdims=True))
                p = jnp.exp2((sc - m_new).astype(BF16))
                acc = jnp.exp2(m - m_new) * acc + jnp.dot(p, v_ref[0, rows, :], preferred_element_type=F32)
                m = m_new
            o_ref[...] = (acc[:, :LANES] / acc[:, LANES:LANES + 1]).astype(o_ref.dtype)


def _attn(cfg, q, k, v):
    chunk = _chunk_len(cfg)
    h = cfg.n_heads
    t = cfg.n_tokens
    s_min = min(s for _, s in cfg.streams)
    tq = _tile(s_min, 512)
    tk = _tile(s_min, 512)
    nq = chunk // tq
    vmem = 4 * tq * 256 * 2 + 2 * chunk * 512 * 2 + 6 * tq * tk * 4 + 6 * tq * 256 * 4 + (8 << 20)
    return pl.pallas_call(
        functools.partial(_attn_kernel, cfg, chunk, tq, tk),
        grid=(t // chunk, h, nq),
        in_specs=[pl.BlockSpec((1, tq, 256), lambda ci, hi, qi: (hi, ci * nq + qi, 0)),
                  pl.BlockSpec((1, chunk, 256), lambda ci, hi, qi: (hi, ci, 0)),
                  pl.BlockSpec((1, chunk, 256), lambda ci, hi, qi: (hi, ci, 0))],
        out_specs=pl.BlockSpec((tq, LANES), lambda ci, hi, qi: (ci * nq + qi, hi)),
        out_shape=jax.ShapeDtypeStruct((t, h * LANES), BF16),
        compiler_params=_params(("arbitrary", "arbitrary", "arbitrary"), vmem),
        name="attn",
    )(q, k, v)


def _fold_kernel(cc_ref, sc_ref, w_ref, o_ref):
    w = w_ref[0]
    o_ref[0, :, :256] = jnp.dot(cc_ref[...], w, preferred_element_type=F32,
                                precision=lax.Precision.HIGHEST).astype(o_ref.dtype)
    o_ref[0, :, 256:] = jnp.dot(sc_ref[...], w, preferred_element_type=F32,
                                precision=lax.Precision.HIGHEST).astype(o_ref.dtype)


def _fold_fourier_weights(cfg, cc, sc, w_fmix):
    g, c = cfg.f_groups, cfg.f_gdim
    return pl.pallas_call(
        _fold_kernel,
        grid=(g,),
        in_specs=[pl.BlockSpec((c, c), lambda i: (0, 0)),
                  pl.BlockSpec((c, c), lambda i: (0, 0)),
                  pl.BlockSpec((1, c, c), lambda i: (i, 0, 0))],
        out_specs=pl.BlockSpec((1, c, 2 * c), lambda i: (i, 0, 0)),
        out_shape=jax.ShapeDtypeStruct((g, c, 2 * c), BF16),
        compiler_params=_params(("arbitrary",), 16 << 20),
        name="fold_fourier",
    )(cc, sc, w_fmix)


def _four1_kernel(cfg, f_ref, w_ref, o_ref):
    c = cfg.f_gdim
    for g in range(cfg.f_groups):
        y = jnp.dot(f_ref[:, g * c:(g + 1) * c], w_ref[g], preferred_element_type=F32)
        o_ref[0, :, g * c:(g + 1) * c] = y[:, :c].astype(o_ref.dtype)
        o_ref[1, :, g * c:(g + 1) * c] = y[:, c:].astype(o_ref.dtype)


def _four1(cfg, z, w_fold, tm):
    t = cfg.n_tokens
    fw = cfg.f_width
    g, c = cfg.f_groups, cfg.f_gdim
    vmem = 2 * tm * fw * 2 + 2 * g * c * 2 * c * 2 + 4 * tm * fw * 2 + (8 << 20)
    return pl.pallas_call(
        functools.partial(_four1_kernel, cfg),
        grid=(t // tm,),
        in_specs=[pl.BlockSpec((tm, fw), lambda i: (i, 0)),
                  pl.BlockSpec((g, c, 2 * c), lambda i: (0, 0, 0))],
        out_specs=pl.BlockSpec((2, tm, fw), lambda i: (0, i, 0)),
        out_shape=jax.ShapeDtypeStruct((2, t, fw), BF16),
        compiler_params=_params(("arbitrary",), vmem),
        name="four1",
    )(z, w_fold)


def _four2_kernel(cfg, chunk, tm, *refs):
    ns = len(cfg.streams)
    ab_ref, o_ref = refs[2 * ns:]
    c = pl.program_id(0)
    mi = pl.program_id(2)
    for k, (_, s) in enumerate(cfg.streams):
        @pl.when(_in_stream_chunks(cfg, k, c, chunk))
        def _(k=k, s=s):
            start = pl.multiple_of(((mi * tm) // s) * s, s)
            y = jnp.dot(refs[2 * k][...], ab_ref[0, pl.ds(start, s), :], preferred_element_type=F32)
            y = y + jnp.dot(refs[2 * k + 1][...], ab_ref[1, pl.ds(start, s), :], preferred_element_type=F32)
            o_ref[...] = y.astype(o_ref.dtype)


def _four2(cfg, dfts, ab):
    chunk = _chunk_len(cfg)
    t = cfg.n_tokens
    fw = cfg.f_width
    tm = _tile(min(s for _, s in cfg.streams), 512)
    tn = _tile(fw, 512)
    nm = chunk // tm

    def d_spec(k):
        s = cfg.streams[k][1]

        def imap(ci, ni, mi):
            inside = _in_stream_chunks(cfg, k, ci, chunk)
            before = ci < cfg.tok_offsets[k] // chunk
            return (jnp.where(inside, ((mi * tm) % s) // tm, jnp.where(before, 0, s // tm - 1)), 0)
        return pl.BlockSpec((tm, s), imap)

    in_specs, args, vmem = [], [], 0
    for k, (dc, ds) in enumerate(dfts):
        in_specs += [d_spec(k), d_spec(k)]
        args += [dc, ds]
        vmem += 4 * tm * cfg.streams[k][1] * 2
    in_specs.append(pl.BlockSpec((2, chunk, tn), lambda ci, ni, mi: (0, ci, ni)))
    vmem += 4 * chunk * tn * 2 + 2 * tm * tn * 2 + 2 * tm * tn * 4 + (8 << 20)
    return pl.pallas_call(
        functools.partial(_four2_kernel, cfg, chunk, tm),
        grid=(t // chunk, fw // tn, nm),
        in_specs=in_specs,
        out_specs=pl.BlockSpec((tm, tn), lambda ci, ni, mi: (ci * nm + mi, ni)),
        out_shape=jax.ShapeDtypeStruct((t, fw), BF16),
        compiler_params=_params(("arbitrary", "arbitrary", "arbitrary"), vmem),
        name="four2",
    )(*args, ab)


def _out_proj_kernel(cfg, tm, *refs):
    ns = len(cfg.streams)
    x_refs = refs[:ns]
    gt_ref, a_ref, f_ref, wa_ref, wb_ref, o_ref = refs[ns:]
    i = pl.program_id(0)
    mix = jnp.dot(a_ref[...], wa_ref[...], preferred_element_type=F32)
    mix = mix + jnp.dot(f_ref[...], wb_ref[...], preferred_element_type=F32)
    for k in range(ns):
        @pl.when(_in_stream(cfg, k, i, tm))
        def _(k=k):
            o_ref[...] = x_refs[k][...] + gt_ref[0] * mix


def _out_proj(cfg, xs, mod3, o_attn, o_four, wa, wb):
    d = cfg.d_model
    t = cfg.n_tokens
    aw, fw = cfg.attn_width, cfg.f_width
    tm = _tile(min(s for _, s in cfg.streams), 1024)
    tn = _tile(d, 512)
    nj = d // tn

    def x_spec(k):
        def imap(i, j):
            inside = _in_stream(cfg, k, i, tm)
            before = i < cfg.tok_offsets[k] // tm
            return (_stream_block(cfg, k, i, tm), jnp.where(inside, j, jnp.where(before, 0, nj - 1)))
        return pl.BlockSpec((tm, tn), imap)

    in_specs = [x_spec(k) for k in range(len(xs))] + [
        pl.BlockSpec((1, 1, tn), lambda i, j: (_mod_row(cfg, i * tm), 0, 2 * nj + j)),
        pl.BlockSpec((tm, aw), lambda i, j: (i, 0)),
        pl.BlockSpec((tm, fw), lambda i, j: (i, 0)),
        pl.BlockSpec((aw, tn), lambda i, j: (0, j)),
        pl.BlockSpec((fw, tn), lambda i, j: (0, j)),
    ]
    vmem = (2 * tm * tn * 4 * len(xs) + 2 * tm * (aw + fw) * 2 + 2 * (aw + fw) * tn * 2
            + 2 * tm * tn * 4 + 2 * tm * tn * 4 + (8 << 20))
    return pl.pallas_call(
        functools.partial(_out_proj_kernel, cfg, tm),
        grid=(t // tm, nj),
        in_specs=in_specs,
        out_specs=pl.BlockSpec((tm, tn), lambda i, j: (i, j)),
        out_shape=jax.ShapeDtypeStruct((t, d), F32),
        compiler_params=_params(("arbitrary", "arbitrary"), vmem),
        name="out_proj",
    )(*xs, mod3, o_attn, o_four, wa, wb)


def _router_kernel(cfg, tm, x_ref, sh_ref, sc_ref, g_ref, wr_ref, tri_ref,
                   hp_ref, mi_ref, gt_ref, cnt_ref, carry_ref):
    i = pl.program_id(0)
    ng, ne = cfg.n_groups, cfg.e_per_group
    n_exp = cfg.n_experts

    @pl.when(i == 0)
    def _():
        carry_ref[...] = jnp.zeros_like(carry_ref)

    h2 = _rms(x_ref[...], cfg.eps) * g_ref[...] * (1.0 + sc_ref[0]) + sh_ref[0]
    for c in range(cfg.d_model // PACK_CHUNK):
        lo = h2[:, c * PACK_CHUNK:c * PACK_CHUNK + LANES]
        hi = h2[:, c * PACK_CHUNK + LANES:(c + 1) * PACK_CHUNK]
        hp_ref[:, c * LANES:(c + 1) * LANES] = _pack_words(lo, hi)

    h_hi = h2.astype(BF16)
    h_lo = (h2 - h_hi.astype(F32)).astype(BF16)
    p1 = jnp.dot(h_hi, wr_ref[...], preferred_element_type=F32)
    p2 = jnp.dot(h_lo, wr_ref[:, :LANES], preferred_element_type=F32)
    logits = (p1[:, :LANES] + p1[:, LANES:] + p2).T

    gl = logits[0:ng]
    io = lax.broadcasted_iota(I32, (ng, tm), 0)
    gm = jnp.max(gl, axis=0, keepdims=True)
    p_group = 1.0 / jnp.sum(jnp.exp(gl - gm), axis=0, keepdims=True)
    gidx = jnp.min(jnp.where(gl == gm, io, ng), axis=0, keepdims=True)
    sel = jnp.zeros((ne, tm), F32)
    for g in range(ng):
        sel = jnp.where(gidx == g, logits[ng + g * ne:ng + (g + 1) * ne], sel)
    ie = lax.broadcasted_iota(I32, (ne, tm), 0)
    m1 = jnp.max(sel, axis=0, keepdims=True)
    i1 = jnp.min(jnp.where(sel == m1, ie, ne), axis=0, keepdims=True)
    sel2 = jnp.where(ie == i1, -jnp.inf, sel)
    m2 = jnp.max(sel2, axis=0, keepdims=True)
    i2 = jnp.min(jnp.where(sel2 == m2, ie, ne), axis=0, keepdims=True)
    e21 = jnp.exp(m2 - m1)
    gate0 = p_group / (1.0 + e21)
    gate1 = p_group * e21 / (1.0 + e21)
    e0 = gidx * ne + i1
    e1 = gidx * ne + i2

    ix = lax.broadcasted_iota(I32, (n_exp, tm), 0)
    hit0 = ix == e0
    hit1 = ix == e1
    member = jnp.logical_or(hit0, hit1).astype(F32)
    before = jnp.dot(member.astype(BF16), tri_ref[...], preferred_element_type=F32)
    total = before + carry_ref[:, 0:1]
    r0 = jnp.sum(jnp.where(hit0, total, 0.0), axis=0, keepdims=True).astype(I32)
    r1 = jnp.sum(jnp.where(hit1, total, 0.0), axis=0, keepdims=True).astype(I32)
    carry_ref[...] = carry_ref[...] + jnp.sum(member, axis=1, keepdims=True)
    cnt_ref[...] = carry_ref[...].astype(I32)

    i8 = lax.broadcasted_iota(I32, (8, tm), 0)
    mi_ref[...] = jnp.where(i8 == 0, e0, jnp.where(i8 == 1, e1, jnp.where(i8 == 2, r0, jnp.where(i8 == 3, r1, 0))))
    il = lax.broadcasted_iota(I32, (LANES, tm), 0)
    gt_ref[...] = jnp.where(il == 0, gate0, jnp.where(il == 1, gate1, 0.0)).T


def _router(cfg, x1, mod3, g_ffn, wr_cat, tri, tm):
    d = cfg.d_model
    t = cfg.n_tokens
    ne = cfg.n_experts
    vmem = 2 * tm * d * 4 + 2 * tm * d * 2 + 2 * d * 256 * 2 + 6 * tm * d * 4 + 2 * tm * tm * 2 + (8 << 20)
    return pl.pallas_call(
        functools.partial(_router_kernel, cfg, tm),
        grid=(t // tm,),
        in_specs=[pl.BlockSpec((tm, d), lambda i: (i, 0)),
                  pl.BlockSpec((1, 1, d), lambda i: (_mod_row(cfg, i * tm), 0, 3)),
                  pl.BlockSpec((1, 1, d), lambda i: (_mod_row(cfg, i * tm), 0, 4)),
                  pl.BlockSpec((1, d), lambda i: (0, 0)),
                  pl.BlockSpec((d, 256), lambda i: (0, 0)),
                  pl.BlockSpec((tm, tm), lambda i: (0, 0))],
        out_specs=[pl.BlockSpec((tm, d // 2), lambda i: (i, 0)),
                   pl.BlockSpec((8, tm), lambda i: (0, i)),
                   pl.BlockSpec((tm, LANES), lambda i: (i, 0)),
                   pl.BlockSpec((ne, LANES), lambda i: (0, 0))],
        out_shape=[jax.ShapeDtypeStruct((t, d // 2), U32),
                   jax.ShapeDtypeStruct((8, t), I32),
                   jax.ShapeDtypeStruct((t, LANES), F32),
                   jax.ShapeDtypeStruct((ne, LANES), I32)],
        scratch_shapes=[pltpu.VMEM((ne, LANES), F32)],
        compiler_params=_params(("arbitrary",), vmem),
        name="router",
    )(x1, mod3, mod3, g_ffn, wr_cat, tri)


def _dest_row(start_ref, meta_ref, kk, r):
    return start_ref[meta_ref[kk, r]] + meta_ref[2 + kk, r]


def _dispatch_kernel(tm, start_ref, meta_ref, h_ref, xs_in_ref, xs_ref, sem):
    del xs_in_ref

    def issue(r, carry):
        for kk in range(2):
            dst = xs_ref.at[pl.ds(_dest_row(start_ref, meta_ref, kk, r), 1)]
            pltpu.make_async_copy(h_ref.at[pl.ds(r, 1)], dst, sem).start(priority=kk)
        return carry

    lax.fori_loop(0, tm, issue, 0, unroll=8)
    for kk in range(2):
        pltpu.make_async_copy(h_ref, xs_ref.at[pl.ds(0, tm)], sem).wait()


def _dispatch(cfg, pad_start, meta, h2p, xs_init, tm):
    t = cfg.n_tokens
    w = h2p.shape[1]
    grid_spec = pltpu.PrefetchScalarGridSpec(
        num_scalar_prefetch=1,
        grid=(t // tm,),
        in_specs=[pl.BlockSpec((8, tm), lambda i, st: (0, i), memory_space=pltpu.SMEM),
                  pl.BlockSpec((tm, w), lambda i, st: (i, 0)),
                  pl.BlockSpec(memory_space=pl.ANY)],
        out_specs=pl.BlockSpec(memory_space=pl.ANY),
        scratch_shapes=[pltpu.SemaphoreType.DMA(())],
    )
    return pl.pallas_call(
        functools.partial(_dispatch_kernel, tm),
        grid_spec=grid_spec,
        out_shape=jax.ShapeDtypeStruct(xs_init.shape, xs_init.dtype),
        input_output_aliases={3: 0},
        compiler_params=_params(("arbitrary",), 4 * tm * w * 4 + (4 << 20)),
        name="dispatch",
    )(pad_start, meta, h2p, xs_init)


def _expert_schedule(cfg, cnt, n_blocks):
    bm = cfg.moe_block
    padded = (cnt + bm - 1) // bm * bm
    pad_end = jnp.cumsum(padded)
    pad_start = (pad_end - padded).astype(I32)
    blk = jnp.arange(n_blocks, dtype=I32)
    be = jnp.minimum(jnp.sum((pad_end[None, :] <= (blk * bm)[:, None]).astype(I32), axis=1), cfg.n_experts - 1)
    nv = (pad_end[-1] // bm).astype(I32)
    prev = jnp.concatenate([jnp.full((1,), -1, I32), be[:-1]])
    first = jnp.logical_and(blk < nv, be != prev)
    suffix = lax.cummin(jnp.where(first, blk, n_blocks), axis=0, reverse=True)
    next_first = jnp.concatenate([suffix[1:], jnp.full((1,), n_blocks, I32)])
    last = next_first >= n_blocks
    nxt = be[jnp.where(last, 0, next_first)]
    return pad_start, (be.astype(I32), first.astype(I32), nxt.astype(I32), last.astype(I32), nv[None])


def _stage_expert_weights(sched, i, tile, n_tiles, copies, convert):
    be_ref, first_ref, nxt_ref, last_ref, nv_ref = sched

    @pl.when(jnp.logical_and(i < nv_ref[0], first_ref[i] == 1))
    def _():
        @pl.when(jnp.logical_and(tile == 0, i == 0))
        def _():
            for cp in copies:
                cp(be_ref[i], tile).start()

        for cp in copies:
            cp(be_ref[i], tile).wait()
        convert()
        is_last = last_ref[i] == 1

        @pl.when(jnp.logical_not(is_last))
        def _():
            for cp in copies:
                cp(nxt_ref[i], tile).start()

        @pl.when(jnp.logical_and(is_last, tile + 1 < n_tiles))
        def _():
            for cp in copies:
                cp(nxt_ref[i], tile + 1).start()


def _moe1_kernel(cfg, tf, n_tiles, be_ref, first_ref, nxt_ref, last_ref, nv_ref, x_ref, w1_ref, w3_ref, o_ref,
                 xb_ref, s1_ref, s3_ref, b1_ref, b3_ref, sem):
    j = pl.program_id(0)
    i = pl.program_id(1)

    def copy(w_ref, stage_ref, k):
        def make(e, jj):
            return pltpu.make_async_copy(w_ref.at[e, :, pl.ds(pl.multiple_of(jj * tf, tf), tf)], stage_ref, sem.at[k])
        return make

    def convert():
        b1_ref[...] = s1_ref[...].astype(BF16)
        b3_ref[...] = s3_ref[...].astype(BF16)

    _stage_expert_weights((be_ref, first_ref, nxt_ref, last_ref, nv_ref), i, j, n_tiles,
                          [copy(w1_ref, s1_ref, 0), copy(w3_ref, s3_ref, 1)], convert)

    @pl.when(i < nv_ref[0])
    def _():
        for c in range(cfg.d_model // PACK_CHUNK):
            lo, hi = _unpack_words(x_ref[:, c * LANES:(c + 1) * LANES])
            xb_ref[:, c * PACK_CHUNK:c * PACK_CHUNK + LANES] = lo.astype(BF16)
            xb_ref[:, c * PACK_CHUNK + LANES:(c + 1) * PACK_CHUNK] = hi.astype(BF16)
        x = xb_ref[...]
        a = jnp.dot(x, b1_ref[...], preferred_element_type=F32)
        b = jnp.dot(x, b3_ref[...], preferred_element_type=F32)
        o_ref[...] = (a * jax.nn.sigmoid(a) * b).astype(o_ref.dtype)

    @pl.when(i >= nv_ref[0])
    def _():
        o_ref[...] = jnp.zeros_like(o_ref)


def _moe1(cfg, sched, xs, w1, w3):
    d, f = cfg.d_model, cfg.d_expert
    bm = cfg.moe_block
    rows = xs.shape[0]
    tf = _tile(f, cfg.moe_up_tile)
    n_tiles = f // tf

    def x_map(j, i, be, first, nxt, last, nv):
        return (jnp.minimum(i, nv[0] - 1), 0)

    grid_spec = pltpu.PrefetchScalarGridSpec(
        num_scalar_prefetch=5,
        grid=(n_tiles, rows // bm),
        in_specs=[pl.BlockSpec((bm, d // 2), x_map),
                  pl.BlockSpec(memory_space=pl.ANY),
                  pl.BlockSpec(memory_space=pl.ANY)],
        out_specs=pl.BlockSpec((bm, tf), lambda j, i, *_: (i, j)),
        scratch_shapes=[pltpu.VMEM((bm, d), BF16),
                        pltpu.VMEM((d, tf), F32), pltpu.VMEM((d, tf), F32),
                        pltpu.VMEM((d, tf), BF16), pltpu.VMEM((d, tf), BF16),
                        pltpu.SemaphoreType.DMA((2,))],
    )
    vmem = 2 * bm * d * 2 + bm * d * 2 + 2 * d * tf * 4 + 3 * d * tf * 2 + 2 * bm * tf * 2 + 4 * bm * tf * 4 + (8 << 20)
    return pl.pallas_call(
        functools.partial(_moe1_kernel, cfg, tf, n_tiles),
        grid_spec=grid_spec,
        out_shape=jax.ShapeDtypeStruct((rows, f), BF16),
        compiler_params=_params(("arbitrary", "arbitrary"), vmem),
        name="moe1",
    )(*sched, xs, w1, w3)


def _moe2_kernel(cfg, tn, n_tiles, be_ref, first_ref, nxt_ref, last_ref, nv_ref, h_ref, w2_ref, o_ref,
                 s2_ref, b2_ref, sem):
    n = pl.program_id(0)
    i = pl.program_id(1)

    def copy(e, nn):
        return pltpu.make_async_copy(w2_ref.at[e, :, pl.ds(pl.multiple_of(nn * tn, tn), tn)], s2_ref, sem.at[0])

    def convert():
        b2_ref[...] = s2_ref[...].astype(BF16)

    _stage_expert_weights((be_ref, first_ref, nxt_ref, last_ref, nv_ref), i, n, n_tiles, [copy], convert)

    @pl.when(i < nv_ref[0])
    def _():
        y = jnp.dot(h_ref[...], b2_ref[...], preferred_element_type=F32)
        for c in range(tn // PACK_CHUNK):
            o_ref[:, c * LANES:(c + 1) * LANES] = _pack_words(
                y[:, c * PACK_CHUNK:c * PACK_CHUNK + LANES], y[:, c * PACK_CHUNK + LANES:(c + 1) * PACK_CHUNK])

    @pl.when(i >= nv_ref[0])
    def _():
        o_ref[...] = jnp.zeros_like(o_ref)


def _moe2(cfg, sched, hmid, w2):
    d, f = cfg.d_model, cfg.d_expert
    bm = cfg.moe_block
    rows = hmid.shape[0]
    tn = _tile(d, cfg.moe_down_tile)
    assert tn % PACK_CHUNK == 0
    n_tiles = d // tn

    def h_map(n, i, be, first, nxt, last, nv):
        return (jnp.minimum(i, nv[0] - 1), 0)

    grid_spec = pltpu.PrefetchScalarGridSpec(
        num_scalar_prefetch=5,
        grid=(n_tiles, rows // bm),
        in_specs=[pl.BlockSpec((bm, f), h_map),
                  pl.BlockSpec(memory_space=pl.ANY)],
        out_specs=pl.BlockSpec((bm, tn // 2), lambda n, i, *_: (i, n)),
        scratch_shapes=[pltpu.VMEM((f, tn), F32), pltpu.VMEM((f, tn), BF16), pltpu.SemaphoreType.DMA((1,))],
    )
    vmem = 2 * bm * f * 2 + f * tn * 4 + 2 * f * tn * 2 + 2 * bm * tn * 2 + 2 * bm * tn * 4 + (8 << 20)
    return pl.pallas_call(
        functools.partial(_moe2_kernel, cfg, tn, n_tiles),
        grid_spec=grid_spec,
        out_shape=jax.ShapeDtypeStruct((rows, d // 2), U32),
        compiler_params=_params(("arbitrary", "arbitrary"), vmem),
        name="moe2",
    )(*sched, hmid, w2)


def _combine_kernel(cfg, tm, n_blk, start_ref, meta_ref, meta_nxt_ref, x_ref, gt2_ref, gate_ref, y_ref, o_ref,
                    buf_ref, sem):
    i = pl.program_id(0)
    slot = i % 2

    def gather(m_ref, sl):
        def body(r, carry):
            for kk in range(2):
                src = y_ref.at[pl.ds(_dest_row(start_ref, m_ref, kk, r), 1)]
                pltpu.make_async_copy(src, buf_ref.at[sl, kk, pl.ds(r, 1)], sem.at[sl]).start(priority=kk)
            return carry
        lax.fori_loop(0, tm, body, 0, unroll=8)

    @pl.when(i == 0)
    def _():
        gather(meta_ref, slot)

    @pl.when(i + 1 < n_blk)
    def _():
        gather(meta_nxt_ref, 1 - slot)

    for kk in range(2):
        pltpu.make_async_copy(y_ref.at[pl.ds(0, tm)], buf_ref.at[slot, kk], sem.at[slot]).wait()

    g0 = gate_ref[:, 0:1]
    g1 = gate_ref[:, 1:2]
    for c in range(cfg.d_model // PACK_CHUNK):
        lo0, hi0 = _unpack_words(buf_ref[slot, 0, :, c * LANES:(c + 1) * LANES])
        lo1, hi1 = _unpack_words(buf_ref[slot, 1, :, c * LANES:(c + 1) * LANES])
        sl_lo = slice(c * PACK_CHUNK, c * PACK_CHUNK + LANES)
        sl_hi = slice(c * PACK_CHUNK + LANES, (c + 1) * PACK_CHUNK)
        o_ref[:, sl_lo] = x_ref[:, sl_lo] + gt2_ref[0, :, sl_lo] * (g0 * lo0 + g1 * lo1)
        o_ref[:, sl_hi] = x_ref[:, sl_hi] + gt2_ref[0, :, sl_hi] * (g0 * hi0 + g1 * hi1)


def _combine(cfg, k_idx, pad_start, meta, x1, mod3, gates, y, tm):
    b, s = cfg.streams[k_idx]
    off = cfg.tok_offsets[k_idx]
    d = cfg.d_model
    n = b * s
    ob = off // tm
    row0 = cfg.row_offsets[k_idx]
    n_blk = n // tm
    grid_spec = pltpu.PrefetchScalarGridSpec(
        num_scalar_prefetch=1,
        grid=(n_blk,),
        in_specs=[pl.BlockSpec((8, tm), lambda i, st: (0, ob + i), memory_space=pltpu.SMEM),
                  pl.BlockSpec((8, tm), lambda i, st: (0, ob + jnp.minimum(i + 1, n_blk - 1)),
                               memory_space=pltpu.SMEM),
                  pl.BlockSpec((tm, d), lambda i, st: (ob + i, 0)),
                  pl.BlockSpec((1, 1, d), lambda i, st: (row0 + (i * tm) // s, 0, 5)),
                  pl.BlockSpec((tm, LANES), lambda i, st: (ob + i, 0)),
                  pl.BlockSpec(memory_space=pl.ANY)],
        out_specs=pl.BlockSpec((tm, d), lambda i, st: (i, 0)),
        scratch_shapes=[pltpu.VMEM((2, 2, tm, d // 2), U32), pltpu.SemaphoreType.DMA((2,))],
    )
    vmem = 4 * tm * d * 4 + 4 * tm * d * 2 + 2 * tm * 128 * 4 + 6 * tm * 128 * 4 + (8 << 20)
    return pl.pallas_call(
        functools.partial(_combine_kernel, cfg, tm, n_blk),
        grid_spec=grid_spec,
        out_shape=jax.ShapeDtypeStruct((n, d), F32),
        compiler_params=_params(("arbitrary",), vmem),
        name=f"combine{k_idx}",
    )(pad_start, meta, meta, x1, mod3, gates, y)


def _rot_half(a, axis=-1):
    h = a.shape[axis] // 2
    lo = lax.slice_in_dim(a, 0, h, axis=axis)
    hi = lax.slice_in_dim(a, h, 2 * h, axis=axis)
    return jnp.concatenate([hi, lo], axis=axis)


def _rope_table(cfg, n_pos):
    half = cfg.rope // 2
    inv_freq = cfg.theta ** (-2.0 * jnp.arange(half, dtype=F32) / cfg.rope)
    ang = jnp.arange(n_pos, dtype=F32)[:, None] * inv_freq[None, :]
    cos, sin = jnp.cos(ang), jnp.sin(ang)
    return jnp.concatenate([cos, cos, -sin, sin], axis=-1)


def _dft(n, scale):
    idx = jnp.arange(n, dtype=I32)
    ang = ((idx[:, None] * idx[None, :]) % n).astype(F32) * (2.0 * math.pi / n)
    return jnp.cos(ang) * scale, jnp.sin(ang) * scale


def _layer(cfg, xs, cs_, w_ada, b_ada, g_attn, w_in, g_qa, w_uq, g_kva, w_ukv, g_qn, g_kn,
           w_fmix, w_o, g_ffn, w_group, w_route, w1, w3, w2):
    d, h = cfg.d_model, cfg.n_heads
    t = cfg.n_tokens
    nope, rope = cfg.nope, cfg.rope
    assert nope == LANES and cfg.v_dim == LANES and rope == 64 and cfg.f_gdim == PACK_CHUNK
    assert cfg.f_width % cfg.q_rank == 0 and (cfg.f_width + cfg.q_rank) % cfg.kv_rank == 0
    qr, kr = cfg.q_rank, cfg.kv_rank

    wq_in, wkv_in, wpe_in, wf_in = jnp.split(w_in, [qr, qr + kr, qr + kr + rope], axis=1)
    w_cat = jnp.concatenate([wf_in, wq_in, wkv_in, wpe_in, _rot_half(wpe_in)], axis=1)
    n_cat = w_cat.shape[1]
    tn_in = 768 if n_cat > 768 else n_cat
    w_cat = jnp.pad(w_cat, ((0, 0), (0, -n_cat % tn_in))).astype(BF16)

    wq3 = w_uq.reshape(qr, h, nope + rope)
    w_q = jnp.concatenate([wq3, _rot_half(wq3[..., nope:])], axis=-1).transpose(1, 0, 2).astype(BF16)
    w_kv = w_ukv.reshape(kr, h, nope + cfg.v_dim).transpose(1, 0, 2).astype(BF16)
    q_const = math.sqrt(2.0 * cfg.qk_dim) * cfg.qk_dim ** -0.5 * math.log2(math.e)
    ga_q = g_qn[None, :nope] * q_const
    gb_q = jnp.concatenate([g_qn[nope:], _rot_half(g_qn[nope:])])[None] * q_const
    ga_k, gb_k = g_kn[None, :nope], jnp.concatenate([g_kn[nope:], _rot_half(g_kn[nope:])])[None]
    wa, wb = w_o[:cfg.attn_width].astype(BF16), w_o[cfg.attn_width:].astype(BF16)
    wr = jnp.concatenate([w_group, w_route.reshape(d, cfg.n_experts)], axis=1)
    wr = jnp.pad(wr, ((0, 0), (0, LANES - wr.shape[1])))
    wr_hi = wr.astype(BF16)
    wr_cat = jnp.concatenate([wr_hi, (wr - wr_hi.astype(F32)).astype(BF16)], axis=1)

    s_max = max(s for _, s in cfg.streams)
    cs_tab = _rope_table(cfg, s_max)
    cc, sc = _dft(cfg.f_gdim, cfg.f_gdim ** -0.5)

    c_all = jnp.concatenate(cs_, axis=0)
    rows = -(-c_all.shape[0] // 8) * 8
    c_pad = jnp.pad(c_all, ((0, rows - c_all.shape[0]), (0, 0)))
    mod3 = _ada(c_pad, w_ada, b_ada[None]).reshape(rows, 1, 6 * d)

    x2d = [x.reshape(-1, d) for x in xs]
    z = _in_proj(cfg, x2d, mod3, g_attn[None], w_cat)
    tm = _tile(min(s for _, s in cfg.streams), 512)
    q = _q_up(cfg, z, g_qa[None], w_q, ga_q, gb_q, cs_tab, tm)
    k, v = _kv_up(cfg, z, g_kva[None], w_kv, ga_k, gb_k, cs_tab, tm)
    o_attn = _attn(cfg, q, k, v)

    w_fold = _fold_fourier_weights(cfg, cc, sc, w_fmix)
    ab = _four1(cfg, z, w_fold, tm)
    dfts = []
    for _, s in cfg.streams:
        dc, ds = _dft(s, s ** -0.5)
        dfts.append((dc.astype(BF16), (-ds).astype(BF16)))
    o_four = _four2(cfg, dfts, ab)

    x1 = _out_proj(cfg, x2d, mod3, o_attn, o_four, wa, wb)

    tm_r = _tile(tm, 256)
    tri = (jnp.arange(tm_r)[:, None] < jnp.arange(tm_r)[None, :]).astype(BF16)
    h2p, meta, gates, counts = _router(cfg, x1, mod3, g_ffn[None], wr_cat, tri, tm_r)

    bm = cfg.moe_block
    n_blocks = (2 * t) // bm + cfg.n_experts
    pad_start, sched = _expert_schedule(cfg, counts[:, 0], n_blocks)

    xs_rows = _dispatch(cfg, pad_start, meta, h2p, jnp.zeros((n_blocks * bm, d // 2), U32), _tile(tm, 256))
    hmid = _moe1(cfg, sched, xs_rows, w1, w3)
    y = _moe2(cfg, sched, hmid, w2)

    outs = []
    for ki, (b, s) in enumerate(cfg.streams):
        o = _combine(cfg, ki, pad_start, meta, x1, mod3, gates, y, _tile(tm, 256))
        outs.append(o.reshape(b, s, d))
    return tuple(outs)


def kernel(x_prompt, x_sample, c_prompt, c_sample, w_ada, b_ada, g_attn, w_in, g_qa, w_uq, g_kva, w_ukv,
           g_qn, g_kn, w_fmix, w_o, g_ffn, w_group, w_route, w1, w3, w2):
    cfg = Cfg()
    xs = (x_prompt, x_sample)
    cs_ = (c_prompt, c_sample)
    for l in range(w_ada.shape[0]):
        xs = _layer(cfg, xs, cs_, w_ada[l], b_ada[l], g_attn[l], w_in[l], g_qa[l], w_uq[l], g_kva[l],
                    w_ukv[l], g_qn[l], g_kn[l], w_fmix[l], w_o[l], g_ffn[l], w_group[l], w_route[l],
                    w1[l], w3[l], w2[l])
    return xs
```

```python
import dataclasses
import functools
import math

import jax
import jax.numpy as jnp
from jax import lax
from jax.experimental import pallas as pl
from jax.experimental.pallas import tpu as pltpu

BF16 = jnp.bfloat16
F32 = jnp.float32
U32 = jnp.uint32
I32 = jnp.int32

LANES = 128
MXU_DIM = 256
VMEM_CAP = 60 << 20
PACK_CHUNK = 2 * LANES


@dataclasses.dataclass(frozen=True)
class Cfg:
    d_model: int = 4096
    streams: tuple = ((8, 2048), (4, 4096))
    n_heads: int = 16
    nope: int = 128
    rope: int = 64
    v_dim: int = 128
    q_rank: int = 1024
    kv_rank: int = 512
    f_gdim: int = 256
    n_groups: int = 8
    e_per_group: int = 8
    d_expert: int = 1024
    theta: float = 10000.0
    eps: float = 1e-6
    moe_block: int = 512
    moe_up_tile: int = 512
    moe_down_tile: int = 2048

    @property
    def attn_width(self):
        return self.n_heads * self.v_dim

    @property
    def f_width(self):
        return self.d_model - self.attn_width

    @property
    def f_groups(self):
        return self.f_width // self.f_gdim

    @property
    def qk_dim(self):
        return self.nope + self.rope

    @property
    def n_experts(self):
        return self.n_groups * self.e_per_group

    @property
    def tok_offsets(self):
        offs, t = [], 0
        for b, s in self.streams:
            offs.append(t)
            t += b * s
        return tuple(offs)

    @property
    def row_offsets(self):
        offs, r = [], 0
        for b, _ in self.streams:
            offs.append(r)
            r += b
        return tuple(offs)

    @property
    def n_tokens(self):
        return sum(b * s for b, s in self.streams)

    @property
    def n_rows(self):
        return sum(b for b, _ in self.streams)


def _tile(n, want):
    if n <= want:
        return n
    t = want
    while n % t:
        t -= 8
    return t


def _params(sem, vmem_bytes, flags=None):
    return pltpu.CompilerParams(dimension_semantics=sem,
                                vmem_limit_bytes=int(min(VMEM_CAP, vmem_bytes)), flags=flags)


def _mod_row(cfg, t0):
    row = None
    for k, (b, s) in enumerate(cfg.streams):
        expr = cfg.row_offsets[k] + (t0 - cfg.tok_offsets[k]) // s
        row = expr if row is None else jnp.where(t0 >= cfg.tok_offsets[k], expr, row)
    return row


def _pos_block(cfg, i, tm):
    t0 = i * tm
    blk = None
    for k, (b, s) in enumerate(cfg.streams):
        expr = ((t0 - cfg.tok_offsets[k]) % s) // tm
        blk = expr if blk is None else jnp.where(t0 >= cfg.tok_offsets[k], expr, blk)
    return blk


def _stream_block(cfg, k, i, tm):
    nb = cfg.streams[k][0] * cfg.streams[k][1] // tm
    return jnp.clip(i - cfg.tok_offsets[k] // tm, 0, nb - 1)


def _in_stream(cfg, k, i, tm):
    lo = cfg.tok_offsets[k] // tm
    hi = lo + cfg.streams[k][0] * cfg.streams[k][1] // tm
    return jnp.logical_and(i >= lo, i < hi)


def _rms(x, eps):
    return x * lax.rsqrt(jnp.mean(x * x, axis=-1, keepdims=True) + eps)


def _pack_words(a, b):
    wa = lax.bitcast_convert_type(a.astype(BF16).astype(F32), U32) >> 16
    wb = lax.bitcast_convert_type(b.astype(BF16).astype(F32), U32) & jnp.uint32(0xFFFF0000)
    return wa | wb


def _unpack_words(w):
    lo = lax.bitcast_convert_type(w << 16, F32)
    hi = lax.bitcast_convert_type(w & jnp.uint32(0xFFFF0000), F32)
    return lo, hi


def _ada_kernel(c_ref, w_ref, b_ref, o_ref):
    c = c_ref[...]
    s = (c * jax.nn.sigmoid(c)).astype(BF16)
    o_ref[...] = jnp.dot(s, w_ref[...].astype(BF16), preferred_element_type=F32) + b_ref[...]


def _ada(c_pad, w_ada, b_ada):
    r, d = c_pad.shape
    n = w_ada.shape[1]
    tn = _tile(n, 512)
    return pl.pallas_call(
        _ada_kernel,
        grid=(n // tn,),
        in_specs=[pl.BlockSpec((r, d), lambda j: (0, 0)),
                  pl.BlockSpec((d, tn), lambda j: (0, j)),
                  pl.BlockSpec((1, tn), lambda j: (0, j))],
        out_specs=pl.BlockSpec((r, tn), lambda j: (0, j)),
        out_shape=jax.ShapeDtypeStruct((r, n), F32),
        compiler_params=_params(("arbitrary",), 2 * d * tn * 4 + d * tn * 2 + (8 << 20)),
        name="ada",
    )(c_pad, w_ada, b_ada)


def _in_tile(n):
    return 768 if n > 768 else n


def _in_proj_kernel(cfg, tm, n_blk, *refs):
    ns = len(cfg.streams)
    x_refs = refs[:ns]
    sh_ref, sc_ref, g_ref, w_ref, o_ref, h_ref, xbuf_ref, sem = refs[ns:]
    i = pl.program_id(0)

    def fetch(blk, slot, start):
        for k in range(ns):
            @pl.when(_in_stream(cfg, k, blk, tm))
            def _(k=k):
                row = pl.multiple_of((blk - cfg.tok_offsets[k] // tm) * tm, tm)
                cp = pltpu.make_async_copy(x_refs[k].at[pl.ds(row, tm)], xbuf_ref.at[slot], sem.at[slot])
                if start:
                    cp.start()
                else:
                    cp.wait()

    @pl.when(pl.program_id(1) == 0)
    def _():
        slot = i % 2

        @pl.when(i == 0)
        def _():
            fetch(i, slot, True)

        fetch(i, slot, False)

        @pl.when(i + 1 < n_blk)
        def _():
            fetch(i + 1, 1 - slot, True)

        gain = g_ref[...] * (1.0 + sc_ref[0])
        h_ref[...] = (_rms(xbuf_ref[slot], cfg.eps) * gain + sh_ref[0]).astype(BF16)

    o_ref[...] = jnp.dot(h_ref[...], w_ref[...], preferred_element_type=F32).astype(o_ref.dtype)


def _in_proj(cfg, xs, mod3, g_attn, w_cat):
    d = cfg.d_model
    t = cfg.n_tokens
    n = w_cat.shape[1]
    tm = _tile(min(s for _, s in cfg.streams), 512)
    tn = _in_tile(n)
    in_specs = [pl.BlockSpec(memory_space=pl.ANY) for _ in xs] + [
        pl.BlockSpec((1, 1, d), lambda i, j: (_mod_row(cfg, i * tm), 0, 0)),
        pl.BlockSpec((1, 1, d), lambda i, j: (_mod_row(cfg, i * tm), 0, 1)),
        pl.BlockSpec((1, d), lambda i, j: (0, 0)),
        pl.BlockSpec((d, tn), lambda i, j: (0, j)),
    ]
    vmem = 2 * tm * d * 4 + tm * d * 2 + 2 * d * tn * 2 + 2 * tm * tn * 2 + tm * tn * 4 + 2 * tm * d * 4
    return pl.pallas_call(
        functools.partial(_in_proj_kernel, cfg, tm, t // tm),
        grid=(t // tm, n // tn),
        in_specs=in_specs,
        out_specs=pl.BlockSpec((tm, tn), lambda i, j: (i, j)),
        out_shape=jax.ShapeDtypeStruct((t, n), BF16),
        scratch_shapes=[pltpu.VMEM((tm, d), BF16), pltpu.VMEM((2, tm, d), F32), pltpu.SemaphoreType.DMA((2,))],
        compiler_params=_params(("arbitrary", "arbitrary"), vmem + (6 << 20)),
        name="in_proj",
    )(*xs, mod3, mod3, g_attn, w_cat)


def _rope_half(t):
    lane = lax.broadcasted_iota(I32, t.shape, 1)
    return jnp.where(lane < 64, t + pltpu.roll(t, 64, axis=1), 0.0)


def _q_up_kernel(cfg, c_ref, gl_ref, w_ref, ga_ref, gb_ref, cs_ref, o_ref):
    cn = (_rms(c_ref[...].astype(F32), cfg.eps) * gl_ref[...]).astype(BF16)
    ga = ga_ref[...]
    gbcs = gb_ref[...] * cs_ref[...]
    for h in range(cfg.n_heads):
        y = jnp.dot(cn, w_ref[h], preferred_element_type=F32)
        a = y[:, :LANES]
        b = y[:, LANES:]
        aa = a * a
        ssq2 = jnp.sum((aa + b * b) + aa, axis=-1, keepdims=True)
        s = lax.rsqrt(ssq2 + 2.0 * cfg.qk_dim * cfg.eps)
        t = b * gbcs
        o_ref[h, :, :LANES] = (a * ga * s).astype(o_ref.dtype)
        o_ref[h, :, LANES:] = ((t + pltpu.roll(t, 64, axis=1)) * s).astype(o_ref.dtype)


def _q_up(cfg, z, g_qa, w_q, ga, gb, cs, tm):
    t = cfg.n_tokens
    h = cfg.n_heads
    r = cfg.q_rank
    col = cfg.f_width // r
    vmem = 2 * tm * r * 2 + 2 * h * r * 256 * 2 + 4 * h * tm * 256 * 2 + tm * 128 * 8 + tm * r * 8 + (8 << 20)
    return pl.pallas_call(
        functools.partial(_q_up_kernel, cfg),
        grid=(t // tm,),
        in_specs=[pl.BlockSpec((tm, r), lambda i: (i, col)),
                  pl.BlockSpec((1, r), lambda i: (0, 0)),
                  pl.BlockSpec((h, r, 256), lambda i: (0, 0, 0)),
                  pl.BlockSpec((1, LANES), lambda i: (0, 0)),
                  pl.BlockSpec((1, LANES), lambda i: (0, 0)),
                  pl.BlockSpec((tm, LANES), lambda i: (_pos_block(cfg, i, tm), 0))],
        out_specs=pl.BlockSpec((h, tm, 256), lambda i: (0, i, 0)),
        out_shape=jax.ShapeDtypeStruct((h, t, 256), BF16),
        compiler_params=_params(("arbitrary",), vmem),
        name="q_up",
    )(z, g_qa, w_q, ga, gb, cs)


def _kv_up_kernel(cfg, c_ref, pe_ref, gl_ref, w_ref, ga_ref, gb_ref, cs_ref, k_ref, v_ref):
    cn = (_rms(c_ref[...].astype(F32), cfg.eps) * gl_ref[...]).astype(BF16)
    pe = pe_ref[...].astype(F32)
    lane = lax.broadcasted_iota(I32, pe.shape, 1)
    ssq_pe = jnp.sum(jnp.where(lane < 64, pe * pe, 0.0), axis=-1, keepdims=True)
    kr = _rope_half(pe * gb_ref[...] * cs_ref[...])
    ga = ga_ref[...]
    for h in range(cfg.n_heads):
        y = jnp.dot(cn, w_ref[h], preferred_element_type=F32)
        kn = y[:, :LANES]
        s = lax.rsqrt((jnp.sum(kn * kn, axis=-1, keepdims=True) + ssq_pe) * (1.0 / cfg.qk_dim) + cfg.eps)
        k_ref[h, :, :LANES] = (kn * ga * s).astype(k_ref.dtype)
        k_ref[h, :, LANES:] = (kr * s).astype(k_ref.dtype)
        v_ref[h, :, :LANES] = y[:, LANES:].astype(v_ref.dtype)
        v_ref[h, :, LANES:] = jnp.where(lane == 0, 1.0, 0.0).astype(v_ref.dtype)


def _kv_up(cfg, z, g_kva, w_kv, ga, gb, cs, tm):
    t = cfg.n_tokens
    h = cfg.n_heads
    r = cfg.kv_rank
    col_c = (cfg.f_width + cfg.q_rank) // r
    col_pe = (cfg.f_width + cfg.q_rank + cfg.kv_rank) // LANES
    vmem = 2 * tm * (r + 128) * 2 + 2 * h * r * 256 * 2 + 4 * h * tm * 512 * 2 + tm * r * 8 + (8 << 20)
    return pl.pallas_call(
        functools.partial(_kv_up_kernel, cfg),
        grid=(t // tm,),
        in_specs=[pl.BlockSpec((tm, r), lambda i: (i, col_c)),
                  pl.BlockSpec((tm, LANES), lambda i: (i, col_pe)),
                  pl.BlockSpec((1, r), lambda i: (0, 0)),
                  pl.BlockSpec((h, r, 256), lambda i: (0, 0, 0)),
                  pl.BlockSpec((1, LANES), lambda i: (0, 0)),
                  pl.BlockSpec((1, LANES), lambda i: (0, 0)),
                  pl.BlockSpec((tm, LANES), lambda i: (_pos_block(cfg, i, tm), 0))],
        out_specs=[pl.BlockSpec((h, tm, 256), lambda i: (0, i, 0)),
                   pl.BlockSpec((h, tm, 256), lambda i: (0, i, 0))],
        out_shape=[jax.ShapeDtypeStruct((h, t, 256), BF16),
                   jax.ShapeDtypeStruct((h, t, 256), BF16)],
        compiler_params=_params(("arbitrary",), vmem),
        name="kv_up",
    )(z, z, g_kva, w_kv, ga, gb, cs)


def _chunk_len(cfg):
    chunk = max(s for _, s in cfg.streams)
    for (b, s), off in zip(cfg.streams, cfg.tok_offsets):
        assert chunk % s == 0 and (b * s) % chunk == 0 and off % chunk == 0
    return chunk


def _in_stream_chunks(cfg, k, c, chunk):
    lo = cfg.tok_offsets[k] // chunk
    hi = lo + cfg.streams[k][0] * cfg.streams[k][1] // chunk
    return jnp.logical_and(c >= lo, c < hi)


def _attn_kernel(cfg, chunk, tq, tk, q_ref, k_ref, v_ref, o_ref):
    c = pl.program_id(0)
    qi = pl.program_id(2)
    for k, (_, s) in enumerate(cfg.streams):
        @pl.when(_in_stream_chunks(cfg, k, c, chunk))
        def _(s=s):
            start = ((qi * tq) // s) * s
            q = q_ref[0]
            m = jnp.full((tq, 1), -jnp.inf, F32)
            acc = jnp.zeros((tq, 2 * LANES), F32)
            for j in range(s // tk):
                rows = pl.ds(pl.multiple_of(start + j * tk, tk), tk)
                sc = lax.dot_general(q, k_ref[0, rows, :], (((1,), (1,)), ((), ())),
                                     preferred_element_type=F32)
                m_new = jnp.maximum(m, jnp.max(sc, axis=-1, keepdims=True))
                p = jnp.exp2((sc - m_new).astype(BF16))
                acc = jnp.exp2(m - m_new) * acc + jnp.dot(p, v_ref[0, rows, :], preferred_element_type=F32)
                m = m_new
            o_ref[...] = (acc[:, :LANES] / acc[:, LANES:LANES + 1]).astype(o_ref.dtype)


def _attn(cfg, q, k, v):
    chunk = _chunk_len(cfg)
    h = cfg.n_heads
    t = cfg.n_tokens
    s_min = min(s for _, s in cfg.streams)
    tq = _tile(s_min, 512)
    tk = _tile(s_min, 256)
    nq = chunk // tq
    vmem = 4 * tq * 256 * 2 + 2 * chunk * 512 * 2 + 6 * tq * tk * 4 + 6 * tq * 256 * 4 + (8 << 20)
    return pl.pallas_call(
        functools.partial(_attn_kernel, cfg, chunk, tq, tk),
        grid=(t // chunk, h, nq),
        in_specs=[pl.BlockSpec((1, tq, 256), lambda ci, hi, qi: (hi, ci * nq + qi, 0)),
                  pl.BlockSpec((1, chunk, 256), lambda ci, hi, qi: (hi, ci, 0)),
                  pl.BlockSpec((1, chunk, 256), lambda ci, hi, qi: (hi, ci, 0))],
        out_specs=pl.BlockSpec((tq, LANES), lambda ci, hi, qi: (ci * nq + qi, hi)),
        out_shape=jax.ShapeDtypeStruct((t, h * LANES), BF16),
        compiler_params=_params(("arbitrary", "arbitrary", "arbitrary"), vmem),
        name="attn",
    )(q, k, v)


def _fold_kernel(cc_ref, sc_ref, w_ref, o_ref):
    w = w_ref[0]
    o_ref[0, :, :256] = jnp.dot(cc_ref[...], w, preferred_element_type=F32,
                                precision=lax.Precision.HIGHEST).astype(o_ref.dtype)
    o_ref[0, :, 256:] = jnp.dot(sc_ref[...], w, preferred_element_type=F32,
                                precision=lax.Precision.HIGHEST).astype(o_ref.dtype)


def _fold_fourier_weights(cfg, cc, sc, w_fmix):
    g, c = cfg.f_groups, cfg.f_gdim
    return pl.pallas_call(
        _fold_kernel,
        grid=(g,),
        in_specs=[pl.BlockSpec((c, c), lambda i: (0, 0)),
                  pl.BlockSpec((c, c), lambda i: (0, 0)),
                  pl.BlockSpec((1, c, c), lambda i: (i, 0, 0))],
        out_specs=pl.BlockSpec((1, c, 2 * c), lambda i: (i, 0, 0)),
        out_shape=jax.ShapeDtypeStruct((g, c, 2 * c), BF16),
        compiler_params=_params(("arbitrary",), 16 << 20),
        name="fold_fourier",
    )(cc, sc, w_fmix)


MID_ROWS = 16


def _half_block(cfg, i, tm):
    hr0 = i * tm
    out = None
    for k, (b, s) in enumerate(cfg.streams):
        local = hr0 - cfg.tok_offsets[k] // 2
        seq = local // (s // 2)
        s0 = local % (s // 2)
        vals = (cfg.tok_offsets[k] + seq * s + s0, s0, cfg.row_offsets[k] + seq, s)
        if out is None:
            out = vals
        else:
            here = hr0 >= cfg.tok_offsets[k] // 2
            out = tuple(jnp.where(here, v, o) for v, o in zip(vals, out))
    return out


def _four1_kernel(cfg, tm, f_ref, r_ref, n_ref, m_ref, j_ref, w_ref, ab_ref, mid_ref):
    _, s0, _, _ = _half_block(cfg, pl.program_id(0), tm)
    c = cfg.f_gdim
    row = lax.broadcasted_iota(I32, (tm, 1), 0)
    for g in range(cfg.f_groups):
        sl = slice(g * c, (g + 1) * c)
        rev = jnp.dot(j_ref[...], r_ref[:, sl], preferred_element_type=F32)
        rev0 = jnp.where(s0 == 0, 0.0, n_ref[0:1, sl].astype(F32))
        rev = jnp.where(row == 0, rev0, rev)
        f = f_ref[:, sl].astype(F32)
        a = jnp.dot((f + rev).astype(BF16), w_ref[g, :, :c], preferred_element_type=F32)
        b = jnp.dot((f - rev).astype(BF16), w_ref[g, :, c:], preferred_element_type=F32)
        ab_ref[0, :, sl] = a.astype(ab_ref.dtype)
        ab_ref[1, :, sl] = b.astype(ab_ref.dtype)

    @pl.when(s0 == 0)
    def _():
        for g in range(cfg.f_groups):
            sl = slice(g * c, (g + 1) * c)
            mid_ref[:, sl] = jnp.dot(m_ref[:, sl], w_ref[g, :, :c], preferred_element_type=F32)


def _four1(cfg, z, w_fold, flip, tm):
    t = cfg.n_tokens
    fw = cfg.f_width
    g, c = cfg.f_groups, cfg.f_gdim

    def cur(i):
        return (_half_block(cfg, i, tm)[0] // tm, 0)

    def partner(i):
        row, s0, _, s = _half_block(cfg, i, tm)
        return ((row - s0 + s - s0) // tm - 1, 0)

    def after_partner(i):
        row, s0, _, s = _half_block(cfg, i, tm)
        return (jnp.minimum((row - s0 + s - s0) // MID_ROWS, t // MID_ROWS - 1), 0)

    def middle(i):
        row, s0, _, s = _half_block(cfg, i, tm)
        return ((row - s0 + s // 2) // MID_ROWS, 0)

    vmem = 4 * tm * fw * 2 + 2 * g * c * 2 * c * 2 + 4 * tm * fw * 2 + 6 * tm * c * 4 + (8 << 20)
    return pl.pallas_call(
        functools.partial(_four1_kernel, cfg, tm),
        grid=(t // 2 // tm,),
        in_specs=[pl.BlockSpec((tm, fw), cur),
                  pl.BlockSpec((tm, fw), partner),
                  pl.BlockSpec((MID_ROWS, fw), after_partner),
                  pl.BlockSpec((MID_ROWS, fw), middle),
                  pl.BlockSpec((tm, tm), lambda i: (0, 0)),
                  pl.BlockSpec((g, c, 2 * c), lambda i: (0, 0, 0))],
        out_specs=[pl.BlockSpec((2, tm, fw), lambda i: (0, i, 0)),
                   pl.BlockSpec((MID_ROWS, fw), lambda i: (_half_block(cfg, i, tm)[2], 0))],
        out_shape=[jax.ShapeDtypeStruct((2, t // 2, fw), BF16),
                   jax.ShapeDtypeStruct((cfg.n_rows * MID_ROWS, fw), F32)],
        compiler_params=_params(("arbitrary",), vmem),
        name="four1",
    )(z, z, z, z, flip, w_fold)


def _four2_kernel(cfg, chunk, tm, *refs):
    ns = len(cfg.streams)
    ab_ref, mid_ref, o_ref = refs[2 * ns:]
    c = pl.program_id(0)
    mi = pl.program_id(2)
    row = lax.broadcasted_iota(I32, (tm, 1), 0)
    for k, (_, s) in enumerate(cfg.streams):
        @pl.when(_in_stream_chunks(cfg, k, c, chunk))
        def _(k=k, s=s):
            half = s // 2
            start = pl.multiple_of(((mi * tm) // s) * half, half)
            y = jnp.dot(refs[2 * k][...], ab_ref[0, pl.ds(start, half), :], preferred_element_type=F32)
            y = y + jnp.dot(refs[2 * k + 1][...], ab_ref[1, pl.ds(start, half), :], preferred_element_type=F32)
            sign = jnp.where(row % 2 == 0, s ** -0.5, -(s ** -0.5))
            o_ref[...] = (y + sign * mid_ref[0:1, :]).astype(o_ref.dtype)


def _four2(cfg, dfts, ab, mid):
    chunk = _chunk_len(cfg)
    t = cfg.n_tokens
    fw = cfg.f_width
    tm = _tile(min(s for _, s in cfg.streams), 512)
    tn = _tile(fw, 1024)
    nm = chunk // tm
    assert tm % 2 == 0

    def d_spec(k):
        s = cfg.streams[k][1]

        def imap(ci, ni, mi):
            inside = _in_stream_chunks(cfg, k, ci, chunk)
            before = ci < cfg.tok_offsets[k] // chunk
            return (jnp.where(inside, ((mi * tm) % s) // tm, jnp.where(before, 0, s // tm - 1)), 0)
        return pl.BlockSpec((tm, s // 2), imap)

    in_specs, args, vmem = [], [], 0
    for k, (dc, ds) in enumerate(dfts):
        in_specs += [d_spec(k), d_spec(k)]
        args += [dc, ds]
        vmem += 4 * tm * (cfg.streams[k][1] // 2) * 2
    in_specs.append(pl.BlockSpec((2, chunk // 2, tn), lambda ci, ni, mi: (0, ci, ni)))
    in_specs.append(pl.BlockSpec((MID_ROWS, tn), lambda ci, ni, mi: (_mod_row(cfg, ci * chunk + mi * tm), ni)))
    vmem += 4 * (chunk // 2) * tn * 2 + 2 * tm * tn * 2 + 3 * tm * tn * 4 + (8 << 20)
    return pl.pallas_call(
        functools.partial(_four2_kernel, cfg, chunk, tm),
        grid=(t // chunk, fw // tn, nm),
        in_specs=in_specs,
        out_specs=pl.BlockSpec((tm, tn), lambda ci, ni, mi: (ci * nm + mi, ni)),
        out_shape=jax.ShapeDtypeStruct((t, fw), BF16),
        compiler_params=_params(("arbitrary", "arbitrary", "arbitrary"), vmem),
        name="four2",
    )(*args, ab, mid)


def _out_proj_kernel(cfg, tm, *refs):
    ns = len(cfg.streams)
    x_refs = refs[:ns]
    gt_ref, a_ref, f_ref, wa_ref, wb_ref, o_ref = refs[ns:]
    i = pl.program_id(0)
    mix = jnp.dot(a_ref[...], wa_ref[...], preferred_element_type=F32)
    mix = mix + jnp.dot(f_ref[...], wb_ref[...], preferred_element_type=F32)
    for k in range(ns):
        @pl.when(_in_stream(cfg, k, i, tm))
        def _(k=k):
            o_ref[...] = x_refs[k][...] + gt_ref[0] * mix


def _out_proj(cfg, xs, mod3, o_attn, o_four, wa, wb):
    d = cfg.d_model
    t = cfg.n_tokens
    aw, fw = cfg.attn_width, cfg.f_width
    tm = _tile(min(s for _, s in cfg.streams), 1024)
    tn = _tile(d, 512)
    nj = d // tn

    def x_spec(k):
        def imap(i, j):
            inside = _in_stream(cfg, k, i, tm)
            before = i < cfg.tok_offsets[k] // tm
            return (_stream_block(cfg, k, i, tm), jnp.where(inside, j, jnp.where(before, 0, nj - 1)))
        return pl.BlockSpec((tm, tn), imap)

    in_specs = [x_spec(k) for k in range(len(xs))] + [
        pl.BlockSpec((1, 1, tn), lambda i, j: (_mod_row(cfg, i * tm), 0, 2 * nj + j)),
        pl.BlockSpec((tm, aw), lambda i, j: (i, 0)),
        pl.BlockSpec((tm, fw), lambda i, j: (i, 0)),
        pl.BlockSpec((aw, tn), lambda i, j: (0, j)),
        pl.BlockSpec((fw, tn), lambda i, j: (0, j)),
    ]
    vmem = (2 * tm * tn * 4 * len(xs) + 2 * tm * (aw + fw) * 2 + 2 * (aw + fw) * tn * 2
            + 2 * tm * tn * 4 + 2 * tm * tn * 4 + (8 << 20))
    return pl.pallas_call(
        functools.partial(_out_proj_kernel, cfg, tm),
        grid=(t // tm, nj),
        in_specs=in_specs,
        out_specs=pl.BlockSpec((tm, tn), lambda i, j: (i, j)),
        out_shape=jax.ShapeDtypeStruct((t, d), F32),
        compiler_params=_params(("arbitrary", "arbitrary"), vmem),
        name="out_proj",
    )(*xs, mod3, o_attn, o_four, wa, wb)


def _router_kernel(cfg, tm, x_ref, sh_ref, sc_ref, g_ref, wr_ref, tri_ref,
                   hp_ref, mi_ref, gt_ref, cnt_ref, carry_ref):
    i = pl.program_id(0)
    ng, ne = cfg.n_groups, cfg.e_per_group
    n_exp = cfg.n_experts

    @pl.when(i == 0)
    def _():
        carry_ref[...] = jnp.zeros_like(carry_ref)

    h2 = _rms(x_ref[...], cfg.eps) * g_ref[...] * (1.0 + sc_ref[0]) + sh_ref[0]
    for c in range(cfg.d_model // PACK_CHUNK):
        lo = h2[:, c * PACK_CHUNK:c * PACK_CHUNK + LANES]
        hi = h2[:, c * PACK_CHUNK + LANES:(c + 1) * PACK_CHUNK]
        hp_ref[:, c * LANES:(c + 1) * LANES] = _pack_words(lo, hi)

    h_hi = h2.astype(BF16)
    h_lo = (h2 - h_hi.astype(F32)).astype(BF16)
    p1 = jnp.dot(h_hi, wr_ref[...], preferred_element_type=F32)
    p2 = jnp.dot(h_lo, wr_ref[:, :LANES], preferred_element_type=F32)
    logits = (p1[:, :LANES] + p1[:, LANES:] + p2).T

    gl = logits[0:ng]
    io = lax.broadcasted_iota(I32, (ng, tm), 0)
    gm = jnp.max(gl, axis=0, keepdims=True)
    p_group = 1.0 / jnp.sum(jnp.exp(gl - gm), axis=0, keepdims=True)
    gidx = jnp.min(jnp.where(gl == gm, io, ng), axis=0, keepdims=True)
    sel = jnp.zeros((ne, tm), F32)
    for g in range(ng):
        sel = jnp.where(gidx == g, logits[ng + g * ne:ng + (g + 1) * ne], sel)
    ie = lax.broadcasted_iota(I32, (ne, tm), 0)
    m1 = jnp.max(sel, axis=0, keepdims=True)
    i1 = jnp.min(jnp.where(sel == m1, ie, ne), axis=0, keepdims=True)
    sel2 = jnp.where(ie == i1, -jnp.inf, sel)
    m2 = jnp.max(sel2, axis=0, keepdims=True)
    i2 = jnp.min(jnp.where(sel2 == m2, ie, ne), axis=0, keepdims=True)
    e21 = jnp.exp(m2 - m1)
    gate0 = p_group / (1.0 + e21)
    gate1 = p_group * e21 / (1.0 + e21)
    e0 = gidx * ne + i1
    e1 = gidx * ne + i2

    ix = lax.broadcasted_iota(I32, (n_exp, tm), 0)
    hit0 = ix == e0
    hit1 = ix == e1
    member = jnp.logical_or(hit0, hit1).astype(F32)
    before = jnp.dot(member.astype(BF16), tri_ref[...], preferred_element_type=F32)
    total = before + carry_ref[:, 0:1]
    r0 = jnp.sum(jnp.where(hit0, total, 0.0), axis=0, keepdims=True).astype(I32)
    r1 = jnp.sum(jnp.where(hit1, total, 0.0), axis=0, keepdims=True).astype(I32)
    carry_ref[...] = carry_ref[...] + jnp.sum(member, axis=1, keepdims=True)
    cnt_ref[...] = carry_ref[...].astype(I32)

    i8 = lax.broadcasted_iota(I32, (8, tm), 0)
    mi_ref[...] = jnp.where(i8 == 0, e0, jnp.where(i8 == 1, e1, jnp.where(i8 == 2, r0, jnp.where(i8 == 3, r1, 0))))
    il = lax.broadcasted_iota(I32, (LANES, tm), 0)
    gt_ref[...] = jnp.where(il == 0, gate0, jnp.where(il == 1, gate1, 0.0)).T


def _router(cfg, x1, mod3, g_ffn, wr_cat, tri, tm):
    d = cfg.d_model
    t = cfg.n_tokens
    ne = cfg.n_experts
    vmem = 2 * tm * d * 4 + 2 * tm * d * 2 + 2 * d * 256 * 2 + 6 * tm * d * 4 + 2 * tm * tm * 2 + (8 << 20)
    return pl.pallas_call(
        functools.partial(_router_kernel, cfg, tm),
        grid=(t // tm,),
        in_specs=[pl.BlockSpec((tm, d), lambda i: (i, 0)),
                  pl.BlockSpec((1, 1, d), lambda i: (_mod_row(cfg, i * tm), 0, 3)),
                  pl.BlockSpec((1, 1, d), lambda i: (_mod_row(cfg, i * tm), 0, 4)),
                  pl.BlockSpec((1, d), lambda i: (0, 0)),
                  pl.BlockSpec((d, 256), lambda i: (0, 0)),
                  pl.BlockSpec((tm, tm), lambda i: (0, 0))],
        out_specs=[pl.BlockSpec((tm, d // 2), lambda i: (i, 0)),
                   pl.BlockSpec((8, tm), lambda i: (0, i)),
                   pl.BlockSpec((tm, LANES), lambda i: (i, 0)),
                   pl.BlockSpec((ne, LANES), lambda i: (0, 0))],
        out_shape=[jax.ShapeDtypeStruct((t, d // 2), U32),
                   jax.ShapeDtypeStruct((8, t), I32),
                   jax.ShapeDtypeStruct((t, LANES), F32),
                   jax.ShapeDtypeStruct((ne, LANES), I32)],
        scratch_shapes=[pltpu.VMEM((ne, LANES), F32)],
        compiler_params=_params(("arbitrary",), vmem),
        name="router",
    )(x1, mod3, mod3, g_ffn, wr_cat, tri)


def _dest_row(start_ref, meta_ref, kk, r):
    return start_ref[meta_ref[kk, r]] + meta_ref[2 + kk, r]


def _dispatch_kernel(tm, start_ref, meta_ref, h_ref, xs_in_ref, xs_ref, sem):
    del xs_in_ref

    def issue(r, carry):
        for kk in range(2):
            dst = xs_ref.at[pl.ds(_dest_row(start_ref, meta_ref, kk, r), 1)]
            pltpu.make_async_copy(h_ref.at[pl.ds(r, 1)], dst, sem).start(priority=kk)
        return carry

    lax.fori_loop(0, tm, issue, 0, unroll=8)
    for kk in range(2):
        pltpu.make_async_copy(h_ref, xs_ref.at[pl.ds(0, tm)], sem).wait()


def _dispatch(cfg, pad_start, meta, h2p, xs_init, tm):
    t = cfg.n_tokens
    w = h2p.shape[1]
    grid_spec = pltpu.PrefetchScalarGridSpec(
        num_scalar_prefetch=1,
        grid=(t // tm,),
        in_specs=[pl.BlockSpec((8, tm), lambda i, st: (0, i), memory_space=pltpu.SMEM),
                  pl.BlockSpec((tm, w), lambda i, st: (i, 0)),
                  pl.BlockSpec(memory_space=pl.ANY)],
        out_specs=pl.BlockSpec(memory_space=pl.ANY),
        scratch_shapes=[pltpu.SemaphoreType.DMA(())],
    )
    return pl.pallas_call(
        functools.partial(_dispatch_kernel, tm),
        grid_spec=grid_spec,
        out_shape=jax.ShapeDtypeStruct(xs_init.shape, xs_init.dtype),
        input_output_aliases={3: 0},
        compiler_params=_params(("arbitrary",), 4 * tm * w * 4 + (4 << 20)),
        name="dispatch",
    )(pad_start, meta, h2p, xs_init)


def _expert_schedule(cfg, cnt, n_blocks):
    bm = cfg.moe_block
    padded = (cnt + bm - 1) // bm * bm
    pad_end = jnp.cumsum(padded)
    pad_start = (pad_end - padded).astype(I32)
    blk = jnp.arange(n_blocks, dtype=I32)
    be = jnp.minimum(jnp.sum((pad_end[None, :] <= (blk * bm)[:, None]).astype(I32), axis=1), cfg.n_experts - 1)
    nv = (pad_end[-1] // bm).astype(I32)
    prev = jnp.concatenate([jnp.full((1,), -1, I32), be[:-1]])
    first = jnp.logical_and(blk < nv, be != prev)
    suffix = lax.cummin(jnp.where(first, blk, n_blocks), axis=0, reverse=True)
    next_first = jnp.concatenate([suffix[1:], jnp.full((1,), n_blocks, I32)])
    last = next_first >= n_blocks
    nxt = be[jnp.where(last, 0, next_first)]
    run = jnp.cumsum(first.astype(I32)) - 1
    n_runs = jnp.sum(first.astype(I32))
    tables = (be.astype(I32), first.astype(I32), nxt.astype(I32), last.astype(I32), run.astype(I32))
    return pad_start, tables + (n_runs[None], nv[None])


N_SCHED = 7


def _stage_expert_weights(sched, i, tile, n_tiles, copies):
    be_ref, first_ref, nxt_ref, last_ref, run_ref, n_runs_ref, nv_ref = sched
    slot = (tile * n_runs_ref[0] + run_ref[i]) % 2

    @pl.when(jnp.logical_and(i < nv_ref[0], first_ref[i] == 1))
    def _():
        @pl.when(jnp.logical_and(tile == 0, i == 0))
        def _():
            for cp in copies:
                cp(be_ref[i], tile, slot).start()

        for cp in copies:
            cp(be_ref[i], tile, slot).wait()
        is_last = last_ref[i] == 1

        @pl.when(jnp.logical_not(is_last))
        def _():
            for cp in copies:
                cp(nxt_ref[i], tile, 1 - slot).start()

        @pl.when(jnp.logical_and(is_last, tile + 1 < n_tiles))
        def _():
            for cp in copies:
                cp(nxt_ref[i], tile + 1, 1 - slot).start()

    return slot


def _moe1_kernel(cfg, tf, n_tiles, *refs):
    sched = refs[:N_SCHED]
    x_ref, w1_ref, w3_ref, o_ref, xb_ref, s1_ref, s3_ref, sem = refs[N_SCHED:]
    nv_ref = sched[-1]
    j = pl.program_id(0)
    i = pl.program_id(1)

    def copy(w_ref, stage_ref, k):
        def make(e, jj, slot):
            src = w_ref.at[e, :, pl.ds(pl.multiple_of(jj * tf, tf), tf)]
            return pltpu.make_async_copy(src, stage_ref.at[slot], sem.at[k, slot])
        return make

    slot = _stage_expert_weights(sched, i, j, n_tiles, [copy(w1_ref, s1_ref, 0), copy(w3_ref, s3_ref, 1)])

    @pl.when(i < nv_ref[0])
    def _():
        for c in range(cfg.d_model // PACK_CHUNK):
            lo, hi = _unpack_words(x_ref[:, c * LANES:(c + 1) * LANES])
            xb_ref[:, c * PACK_CHUNK:c * PACK_CHUNK + LANES] = lo.astype(BF16)
            xb_ref[:, c * PACK_CHUNK + LANES:(c + 1) * PACK_CHUNK] = hi.astype(BF16)
        x = xb_ref[...]
        a = jnp.dot(x, s1_ref[slot].astype(BF16), preferred_element_type=F32)
        b = jnp.dot(x, s3_ref[slot].astype(BF16), preferred_element_type=F32)
        o_ref[...] = (a * jax.nn.sigmoid(a) * b).astype(o_ref.dtype)

    @pl.when(i >= nv_ref[0])
    def _():
        o_ref[...] = jnp.zeros_like(o_ref)


def _moe1(cfg, sched, xs, w1, w3):
    d, f = cfg.d_model, cfg.d_expert
    bm = cfg.moe_block
    rows = xs.shape[0]
    tf = _tile(f, cfg.moe_up_tile)
    n_tiles = f // tf

    def x_map(j, i, *sched):
        return (jnp.minimum(i, sched[-1][0] - 1), 0)

    grid_spec = pltpu.PrefetchScalarGridSpec(
        num_scalar_prefetch=N_SCHED,
        grid=(n_tiles, rows // bm),
        in_specs=[pl.BlockSpec((bm, d // 2), x_map),
                  pl.BlockSpec(memory_space=pl.ANY),
                  pl.BlockSpec(memory_space=pl.ANY)],
        out_specs=pl.BlockSpec((bm, tf), lambda j, i, *_: (i, j)),
        scratch_shapes=[pltpu.VMEM((bm, d), BF16),
                        pltpu.VMEM((2, d, tf), F32), pltpu.VMEM((2, d, tf), F32),
                        pltpu.SemaphoreType.DMA((2, 2))],
    )
    vmem = 2 * bm * d * 2 + bm * d * 2 + 4 * d * tf * 4 + 2 * d * tf * 2 + 2 * bm * tf * 2 + 4 * bm * tf * 4 + (4 << 20)
    return pl.pallas_call(
        functools.partial(_moe1_kernel, cfg, tf, n_tiles),
        grid_spec=grid_spec,
        out_shape=jax.ShapeDtypeStruct((rows, f), BF16),
        compiler_params=_params(("arbitrary", "arbitrary"), vmem),
        name="moe1",
    )(*sched, xs, w1, w3)


def _moe2_kernel(cfg, tn, n_tiles, *refs):
    sched = refs[:N_SCHED]
    h_ref, w2_ref, o_ref, s2_ref, sem = refs[N_SCHED:]
    nv_ref = sched[-1]
    n = pl.program_id(0)
    i = pl.program_id(1)

    def copy(e, nn, slot):
        src = w2_ref.at[e, :, pl.ds(pl.multiple_of(nn * tn, tn), tn)]
        return pltpu.make_async_copy(src, s2_ref.at[slot], sem.at[slot])

    slot = _stage_expert_weights(sched, i, n, n_tiles, [copy])

    @pl.when(i < nv_ref[0])
    def _():
        y = jnp.dot(h_ref[...], s2_ref[slot].astype(BF16), preferred_element_type=F32)
        for c in range(tn // PACK_CHUNK):
            o_ref[:, c * LANES:(c + 1) * LANES] = _pack_words(
                y[:, c * PACK_CHUNK:c * PACK_CHUNK + LANES], y[:, c * PACK_CHUNK + LANES:(c + 1) * PACK_CHUNK])

    @pl.when(i >= nv_ref[0])
    def _():
        o_ref[...] = jnp.zeros_like(o_ref)


def _moe2(cfg, sched, hmid, w2):
    d, f = cfg.d_model, cfg.d_expert
    bm = cfg.moe_block
    rows = hmid.shape[0]
    tn = _tile(d, cfg.moe_down_tile)
    assert tn % PACK_CHUNK == 0
    n_tiles = d // tn

    def h_map(n, i, *sched):
        return (jnp.minimum(i, sched[-1][0] - 1), 0)

    grid_spec = pltpu.PrefetchScalarGridSpec(
        num_scalar_prefetch=N_SCHED,
        grid=(n_tiles, rows // bm),
        in_specs=[pl.BlockSpec((bm, f), h_map),
                  pl.BlockSpec(memory_space=pl.ANY)],
        out_specs=pl.BlockSpec((bm, tn // 2), lambda n, i, *_: (i, n)),
        scratch_shapes=[pltpu.VMEM((2, f, tn), F32), pltpu.SemaphoreType.DMA((2,))],
    )
    vmem = 2 * bm * f * 2 + 2 * f * tn * 4 + f * tn * 2 + 2 * bm * tn * 2 + 2 * bm * tn * 4 + (8 << 20)
    return pl.pallas_call(
        functools.partial(_moe2_kernel, cfg, tn, n_tiles),
        grid_spec=grid_spec,
        out_shape=jax.ShapeDtypeStruct((rows, d // 2), U32),
        compiler_params=_params(("arbitrary", "arbitrary"), vmem),
        name="moe2",
    )(*sched, hmid, w2)


def _combine_kernel(cfg, tm, n_blk, start_ref, meta_ref, meta_nxt_ref, x_ref, gt2_ref, gate_ref, y_ref, o_ref,
                    buf_ref, sem):
    i = pl.program_id(0)
    slot = i % 2

    def gather(m_ref, sl):
        def body(r, carry):
            for kk in range(2):
                src = y_ref.at[pl.ds(_dest_row(start_ref, m_ref, kk, r), 1)]
                pltpu.make_async_copy(src, buf_ref.at[sl, kk, pl.ds(r, 1)], sem.at[sl]).start(priority=kk)
            return carry
        lax.fori_loop(0, tm, body, 0, unroll=8)

    @pl.when(i == 0)
    def _():
        gather(meta_ref, slot)

    @pl.when(i + 1 < n_blk)
    def _():
        gather(meta_nxt_ref, 1 - slot)

    for kk in range(2):
        pltpu.make_async_copy(y_ref.at[pl.ds(0, tm)], buf_ref.at[slot, kk], sem.at[slot]).wait()

    g0 = gate_ref[:, 0:1]
    g1 = gate_ref[:, 1:2]
    for c in range(cfg.d_model // PACK_CHUNK):
        lo0, hi0 = _unpack_words(buf_ref[slot, 0, :, c * LANES:(c + 1) * LANES])
        lo1, hi1 = _unpack_words(buf_ref[slot, 1, :, c * LANES:(c + 1) * LANES])
        sl_lo = slice(c * PACK_CHUNK, c * PACK_CHUNK + LANES)
        sl_hi = slice(c * PACK_CHUNK + LANES, (c + 1) * PACK_CHUNK)
        o_ref[:, sl_lo] = x_ref[:, sl_lo] + gt2_ref[0, :, sl_lo] * (g0 * lo0 + g1 * lo1)
        o_ref[:, sl_hi] = x_ref[:, sl_hi] + gt2_ref[0, :, sl_hi] * (g0 * hi0 + g1 * hi1)


def _combine(cfg, k_idx, pad_start, meta, x1, mod3, gates, y, tm):
    b, s = cfg.streams[k_idx]
    off = cfg.tok_offsets[k_idx]
    d = cfg.d_model
    n = b * s
    ob = off // tm
    row0 = cfg.row_offsets[k_idx]
    n_blk = n // tm
    grid_spec = pltpu.PrefetchScalarGridSpec(
        num_scalar_prefetch=1,
        grid=(n_blk,),
        in_specs=[pl.BlockSpec((8, tm), lambda i, st: (0, ob + i), memory_space=pltpu.SMEM),
                  pl.BlockSpec((8, tm), lambda i, st: (0, ob + jnp.minimum(i + 1, n_blk - 1)),
                               memory_space=pltpu.SMEM),
                  pl.BlockSpec((tm, d), lambda i, st: (ob + i, 0)),
                  pl.BlockSpec((1, 1, d), lambda i, st: (row0 + (i * tm) // s, 0, 5)),
                  pl.BlockSpec((tm, LANES), lambda i, st: (ob + i, 0)),
                  pl.BlockSpec(memory_space=pl.ANY)],
        out_specs=pl.BlockSpec((tm, d), lambda i, st: (i, 0)),
        scratch_shapes=[pltpu.VMEM((2, 2, tm, d // 2), U32), pltpu.SemaphoreType.DMA((2,))],
    )
    vmem = 4 * tm * d * 4 + 4 * tm * d * 2 + 2 * tm * 128 * 4 + 6 * tm * 128 * 4 + (8 << 20)
    return pl.pallas_call(
        functools.partial(_combine_kernel, cfg, tm, n_blk),
        grid_spec=grid_spec,
        out_shape=jax.ShapeDtypeStruct((n, d), F32),
        compiler_params=_params(("arbitrary",), vmem),
        name=f"combine{k_idx}",
    )(pad_start, meta, meta, x1, mod3, gates, y)


def _rot_half(a, axis=-1):
    h = a.shape[axis] // 2
    lo = lax.slice_in_dim(a, 0, h, axis=axis)
    hi = lax.slice_in_dim(a, h, 2 * h, axis=axis)
    return jnp.concatenate([hi, lo], axis=axis)


def _rope_table(cfg, n_pos):
    half = cfg.rope // 2
    inv_freq = cfg.theta ** (-2.0 * jnp.arange(half, dtype=F32) / cfg.rope)
    ang = jnp.arange(n_pos, dtype=F32)[:, None] * inv_freq[None, :]
    cos, sin = jnp.cos(ang), jnp.sin(ang)
    return jnp.concatenate([cos, cos, -sin, sin], axis=-1)


def _dft(n, scale):
    idx = jnp.arange(n, dtype=I32)
    ang = ((idx[:, None] * idx[None, :]) % n).astype(F32) * (2.0 * math.pi / n)
    return jnp.cos(ang) * scale, jnp.sin(ang) * scale


def _dft_half(n, split=64):
    s = jnp.arange(n // 2, dtype=I32)

    def table(rows):
        ang = ((rows[:, None] * s[None, :]) % n).astype(F32) * (2.0 * math.pi / n)
        return jnp.cos(ang), jnp.sin(ang)

    ch, sh = table(jnp.arange(n // split, dtype=I32) * split)
    cl, sl = table(jnp.arange(split, dtype=I32))
    scale = n ** -0.5
    cos = (ch[:, None, :] * cl[None] - sh[:, None, :] * sl[None]) * scale
    msin = (sh[:, None, :] * cl[None] + ch[:, None, :] * sl[None]) * -scale
    return cos.reshape(n, n // 2).astype(BF16), msin.reshape(n, n // 2).astype(BF16)


def _layer(cfg, xs, cs_, w_ada, b_ada, g_attn, w_in, g_qa, w_uq, g_kva, w_ukv, g_qn, g_kn,
           w_fmix, w_o, g_ffn, w_group, w_route, w1, w3, w2):
    d, h = cfg.d_model, cfg.n_heads
    t = cfg.n_tokens
    nope, rope = cfg.nope, cfg.rope
    assert nope == LANES and cfg.v_dim == LANES and rope == 64 and cfg.f_gdim == PACK_CHUNK
    assert cfg.f_width % cfg.q_rank == 0 and (cfg.f_width + cfg.q_rank) % cfg.kv_rank == 0
    qr, kr = cfg.q_rank, cfg.kv_rank

    wq_in, wkv_in, wpe_in, wf_in = jnp.split(w_in, [qr, qr + kr, qr + kr + rope], axis=1)
    w_cat = jnp.concatenate([wf_in, wq_in, wkv_in, wpe_in, _rot_half(wpe_in)], axis=1)
    n_cat = w_cat.shape[1]
    w_cat = jnp.pad(w_cat, ((0, 0), (0, -n_cat % _in_tile(n_cat)))).astype(BF16)

    wq3 = w_uq.reshape(qr, h, nope + rope)
    w_q = jnp.concatenate([wq3, _rot_half(wq3[..., nope:])], axis=-1).transpose(1, 0, 2).astype(BF16)
    w_kv = w_ukv.reshape(kr, h, nope + cfg.v_dim).transpose(1, 0, 2).astype(BF16)
    q_const = math.sqrt(2.0 * cfg.qk_dim) * cfg.qk_dim ** -0.5 * math.log2(math.e)
    ga_q = g_qn[None, :nope] * q_const
    gb_q = jnp.concatenate([g_qn[nope:], _rot_half(g_qn[nope:])])[None] * q_const
    ga_k, gb_k = g_kn[None, :nope], jnp.concatenate([g_kn[nope:], _rot_half(g_kn[nope:])])[None]
    wa, wb = w_o[:cfg.attn_width].astype(BF16), w_o[cfg.attn_width:].astype(BF16)
    wr = jnp.concatenate([w_group, w_route.reshape(d, cfg.n_experts)], axis=1)
    wr = jnp.pad(wr, ((0, 0), (0, LANES - wr.shape[1])))
    wr_hi = wr.astype(BF16)
    wr_cat = jnp.concatenate([wr_hi, (wr - wr_hi.astype(F32)).astype(BF16)], axis=1)

    s_max = max(s for _, s in cfg.streams)
    cs_tab = _rope_table(cfg, s_max)
    cc, sc = _dft(cfg.f_gdim, cfg.f_gdim ** -0.5)

    c_all = jnp.concatenate(cs_, axis=0)
    rows = -(-c_all.shape[0] // 8) * 8
    c_pad = jnp.pad(c_all, ((0, rows - c_all.shape[0]), (0, 0)))
    mod3 = _ada(c_pad, w_ada, b_ada[None]).reshape(rows, 1, 6 * d)

    x2d = [x.reshape(-1, d) for x in xs]
    z = _in_proj(cfg, x2d, mod3, g_attn[None], w_cat)
    tm = _tile(min(s for _, s in cfg.streams), 512)
    q = _q_up(cfg, z, g_qa[None], w_q, ga_q, gb_q, cs_tab, tm)
    k, v = _kv_up(cfg, z, g_kva[None], w_kv, ga_k, gb_k, cs_tab, tm)
    o_attn = _attn(cfg, q, k, v)

    w_fold = _fold_fourier_weights(cfg, cc, sc, w_fmix)
    tm_f = _tile(min(s for _, s in cfg.streams) // 2, 512)
    idx = jnp.arange(tm_f)
    flip = (idx[:, None] + idx[None, :] == tm_f).astype(BF16)
    ab, mid = _four1(cfg, z, w_fold, flip, tm_f)
    o_four = _four2(cfg, [_dft_half(s) for _, s in cfg.streams], ab, mid)

    x1 = _out_proj(cfg, x2d, mod3, o_attn, o_four, wa, wb)

    tm_r = _tile(tm, 256)
    tri = (jnp.arange(tm_r)[:, None] < jnp.arange(tm_r)[None, :]).astype(BF16)
    h2p, meta, gates, counts = _router(cfg, x1, mod3, g_ffn[None], wr_cat, tri, tm_r)

    bm = cfg.moe_block
    n_blocks = (2 * t) // bm + cfg.n_experts
    pad_start, sched = _expert_schedule(cfg, counts[:, 0], n_blocks)

    xs_rows = _dispatch(cfg, pad_start, meta, h2p, jnp.zeros((n_blocks * bm, d // 2), U32), _tile(tm, 256))
    hmid = _moe1(cfg, sched, xs_rows, w1, w3)
    y = _moe2(cfg, sched, hmid, w2)

    outs = []
    for ki, (b, s) in enumerate(cfg.streams):
        o = _combine(cfg, ki, pad_start, meta, x1, mod3, gates, y, _tile(tm, 256))
        outs.append(o.reshape(b, s, d))
    return tuple(outs)


def kernel(x_prompt, x_sample, c_prompt, c_sample, w_ada, b_ada, g_attn, w_in, g_qa, w_uq, g_kva, w_ukv,
           g_qn, g_kn, w_fmix, w_o, g_ffn, w_group, w_route, w1, w3, w2):
    cfg = Cfg()
    xs = (x_prompt, x_sample)
    cs_ = (c_prompt, c_sample)
    for l in range(w_ada.shape[0]):
        xs = _layer(cfg, xs, cs_, w_ada[l], b_ada[l], g_attn[l], w_in[l], g_qa[l], w_uq[l], g_kva[l],
                    w_ukv[l], g_qn[l], g_kn[l], w_fmix[l], w_o[l], g_ffn[l], w_group[l], w_route[l],
                    w1[l], w3[l], w2[l])
    return xs
```

```python
import dataclasses
import functools
import math

import jax
import jax.numpy as jnp
from jax import lax
from jax.experimental import pallas as pl
from jax.experimental.pallas import tpu as pltpu

BF16 = jnp.bfloat16
F32 = jnp.float32
U32 = jnp.uint32
I32 = jnp.int32

LANES = 128
MXU_DIM = 256
VMEM_CAP = 60 << 20
PACK_CHUNK = 2 * LANES


@dataclasses.dataclass(frozen=True)
class Cfg:
    d_model: int = 4096
    streams: tuple = ((8, 2048), (4, 4096))
    n_heads: int = 16
    nope: int = 128
    rope: int = 64
    v_dim: int = 128
    q_rank: int = 1024
    kv_rank: int = 512
    f_gdim: int = 256
    n_groups: int = 8
    e_per_group: int = 8
    d_expert: int = 1024
    theta: float = 10000.0
    eps: float = 1e-6
    moe_block: int = 512
    moe_up_tile: int = 512
    moe_down_tile: int = 2048

    @property
    def attn_width(self):
        return self.n_heads * self.v_dim

    @property
    def f_width(self):
        return self.d_model - self.attn_width

    @property
    def f_groups(self):
        return self.f_width // self.f_gdim

    @property
    def qk_dim(self):
        return self.nope + self.rope

    @property
    def n_experts(self):
        return self.n_groups * self.e_per_group

    @property
    def tok_offsets(self):
        offs, t = [], 0
        for b, s in self.streams:
            offs.append(t)
            t += b * s
        return tuple(offs)

    @property
    def row_offsets(self):
        offs, r = [], 0
        for b, _ in self.streams:
            offs.append(r)
            r += b
        return tuple(offs)

    @property
    def n_tokens(self):
        return sum(b * s for b, s in self.streams)

    @property
    def n_rows(self):
        return sum(b for b, _ in self.streams)


def _tile(n, want):
    if n <= want:
        return n
    t = want
    while n % t:
        t -= 8
    return t


def _params(sem, vmem_bytes, flags=None):
    return pltpu.CompilerParams(dimension_semantics=sem,
                                vmem_limit_bytes=int(min(VMEM_CAP, vmem_bytes)), flags=flags)


def _mod_row(cfg, t0):
    row = None
    for k, (b, s) in enumerate(cfg.streams):
        expr = cfg.row_offsets[k] + (t0 - cfg.tok_offsets[k]) // s
        row = expr if row is None else jnp.where(t0 >= cfg.tok_offsets[k], expr, row)
    return row


def _pos_block(cfg, i, tm):
    t0 = i * tm
    blk = None
    for k, (b, s) in enumerate(cfg.streams):
        expr = ((t0 - cfg.tok_offsets[k]) % s) // tm
        blk = expr if blk is None else jnp.where(t0 >= cfg.tok_offsets[k], expr, blk)
    return blk


def _stream_block(cfg, k, i, tm):
    nb = cfg.streams[k][0] * cfg.streams[k][1] // tm
    return jnp.clip(i - cfg.tok_offsets[k] // tm, 0, nb - 1)


def _in_stream(cfg, k, i, tm):
    lo = cfg.tok_offsets[k] // tm
    hi = lo + cfg.streams[k][0] * cfg.streams[k][1] // tm
    return jnp.logical_and(i >= lo, i < hi)


def _rms(x, eps):
    return x * lax.rsqrt(jnp.mean(x * x, axis=-1, keepdims=True) + eps)


def _pack_words(a, b):
    wa = lax.bitcast_convert_type(a.astype(BF16).astype(F32), U32) >> 16
    wb = lax.bitcast_convert_type(b.astype(BF16).astype(F32), U32) & jnp.uint32(0xFFFF0000)
    return wa | wb


def _unpack_words(w):
    lo = lax.bitcast_convert_type(w << 16, F32)
    hi = lax.bitcast_convert_type(w & jnp.uint32(0xFFFF0000), F32)
    return lo, hi


def _ada_kernel(c_ref, w_ref, b_ref, o_ref):
    c = c_ref[...]
    s = (c * jax.nn.sigmoid(c)).astype(BF16)
    o_ref[...] = jnp.dot(s, w_ref[...].astype(BF16), preferred_element_type=F32) + b_ref[...]


def _ada(c_pad, w_ada, b_ada):
    r, d = c_pad.shape
    n = w_ada.shape[1]
    tn = _tile(n, 512)
    return pl.pallas_call(
        _ada_kernel,
        grid=(n // tn,),
        in_specs=[pl.BlockSpec((r, d), lambda j: (0, 0)),
                  pl.BlockSpec((d, tn), lambda j: (0, j)),
                  pl.BlockSpec((1, tn), lambda j: (0, j))],
        out_specs=pl.BlockSpec((r, tn), lambda j: (0, j)),
        out_shape=jax.ShapeDtypeStruct((r, n), F32),
        compiler_params=_params(("arbitrary",), 2 * d * tn * 4 + d * tn * 2 + (8 << 20)),
        name="ada",
    )(c_pad, w_ada, b_ada)


def _in_tile(n):
    return 768 if n > 768 else n


def _in_proj_kernel(cfg, tm, n_blk, *refs):
    ns = len(cfg.streams)
    x_refs = refs[:ns]
    sh_ref, sc_ref, g_ref, w_ref, o_ref, h_ref, xbuf_ref, sem = refs[ns:]
    i = pl.program_id(0)

    def fetch(blk, slot, start):
        for k in range(ns):
            @pl.when(_in_stream(cfg, k, blk, tm))
            def _(k=k):
                row = pl.multiple_of((blk - cfg.tok_offsets[k] // tm) * tm, tm)
                cp = pltpu.make_async_copy(x_refs[k].at[pl.ds(row, tm)], xbuf_ref.at[slot], sem.at[slot])
                if start:
                    cp.start()
                else:
                    cp.wait()

    @pl.when(pl.program_id(1) == 0)
    def _():
        slot = i % 2

        @pl.when(i == 0)
        def _():
            fetch(i, slot, True)

        fetch(i, slot, False)

        @pl.when(i + 1 < n_blk)
        def _():
            fetch(i + 1, 1 - slot, True)

        gain = g_ref[...] * (1.0 + sc_ref[0])
        h_ref[...] = (_rms(xbuf_ref[slot], cfg.eps) * gain + sh_ref[0]).astype(BF16)

    o_ref[...] = jnp.dot(h_ref[...], w_ref[...], preferred_element_type=F32).astype(o_ref.dtype)


def _in_proj(cfg, xs, mod3, g_attn, w_cat):
    d = cfg.d_model
    t = cfg.n_tokens
    n = w_cat.shape[1]
    tm = _tile(min(s for _, s in cfg.streams), 512)
    tn = _in_tile(n)
    in_specs = [pl.BlockSpec(memory_space=pl.ANY) for _ in xs] + [
        pl.BlockSpec((1, 1, d), lambda i, j: (_mod_row(cfg, i * tm), 0, 0)),
        pl.BlockSpec((1, 1, d), lambda i, j: (_mod_row(cfg, i * tm), 0, 1)),
        pl.BlockSpec((1, d), lambda i, j: (0, 0)),
        pl.BlockSpec((d, tn), lambda i, j: (0, j)),
    ]
    vmem = 2 * tm * d * 4 + tm * d * 2 + 2 * d * tn * 2 + 2 * tm * tn * 2 + tm * tn * 4 + 2 * tm * d * 4
    return pl.pallas_call(
        functools.partial(_in_proj_kernel, cfg, tm, t // tm),
        grid=(t // tm, n // tn),
        in_specs=in_specs,
        out_specs=pl.BlockSpec((tm, tn), lambda i, j: (i, j)),
        out_shape=jax.ShapeDtypeStruct((t, n), BF16),
        scratch_shapes=[pltpu.VMEM((tm, d), BF16), pltpu.VMEM((2, tm, d), F32), pltpu.SemaphoreType.DMA((2,))],
        compiler_params=_params(("arbitrary", "arbitrary"), vmem + (6 << 20)),
        name="in_proj",
    )(*xs, mod3, mod3, g_attn, w_cat)


def _rope_half(t):
    lane = lax.broadcasted_iota(I32, t.shape, 1)
    return jnp.where(lane < 64, t + pltpu.roll(t, 64, axis=1), 0.0)


def _q_up_kernel(cfg, c_ref, gl_ref, w_ref, ga_ref, gb_ref, cs_ref, o_ref):
    cn = (_rms(c_ref[...].astype(F32), cfg.eps) * gl_ref[...]).astype(BF16)
    ga = ga_ref[...]
    gbcs = gb_ref[...] * cs_ref[...]
    for h in range(cfg.n_heads):
        y = jnp.dot(cn, w_ref[h], preferred_element_type=F32)
        a = y[:, :LANES]
        b = y[:, LANES:]
        aa = a * a
        ssq2 = jnp.sum((aa + b * b) + aa, axis=-1, keepdims=True)
        s = lax.rsqrt(ssq2 + 2.0 * cfg.qk_dim * cfg.eps)
        t = b * gbcs
        o_ref[h, :, :LANES] = (a * ga * s).astype(o_ref.dtype)
        o_ref[h, :, LANES:] = ((t + pltpu.roll(t, 64, axis=1)) * s).astype(o_ref.dtype)


def _q_up(cfg, z, g_qa, w_q, ga, gb, cs, tm):
    t = cfg.n_tokens
    h = cfg.n_heads
    r = cfg.q_rank
    col = cfg.f_width // r
    vmem = 2 * tm * r * 2 + 2 * h * r * 256 * 2 + 4 * h * tm * 256 * 2 + tm * 128 * 8 + tm * r * 8 + (8 << 20)
    return pl.pallas_call(
        functools.partial(_q_up_kernel, cfg),
        grid=(t // tm,),
        in_specs=[pl.BlockSpec((tm, r), lambda i: (i, col)),
                  pl.BlockSpec((1, r), lambda i: (0, 0)),
                  pl.BlockSpec((h, r, 256), lambda i: (0, 0, 0)),
                  pl.BlockSpec((1, LANES), lambda i: (0, 0)),
                  pl.BlockSpec((1, LANES), lambda i: (0, 0)),
                  pl.BlockSpec((tm, LANES), lambda i: (_pos_block(cfg, i, tm), 0))],
        out_specs=pl.BlockSpec((h, tm, 256), lambda i: (0, i, 0)),
        out_shape=jax.ShapeDtypeStruct((h, t, 256), BF16),
        compiler_params=_params(("arbitrary",), vmem),
        name="q_up",
    )(z, g_qa, w_q, ga, gb, cs)


def _kv_up_kernel(cfg, c_ref, pe_ref, gl_ref, w_ref, ga_ref, gb_ref, cs_ref, k_ref, v_ref):
    cn = (_rms(c_ref[...].astype(F32), cfg.eps) * gl_ref[...]).astype(BF16)
    pe = pe_ref[...].astype(F32)
    lane = lax.broadcasted_iota(I32, pe.shape, 1)
    ssq_pe = jnp.sum(jnp.where(lane < 64, pe * pe, 0.0), axis=-1, keepdims=True)
    kr = _rope_half(pe * gb_ref[...] * cs_ref[...])
    ga = ga_ref[...]
    for h in range(cfg.n_heads):
        y = jnp.dot(cn, w_ref[h], preferred_element_type=F32)
        kn = y[:, :LANES]
        s = lax.rsqrt((jnp.sum(kn * kn, axis=-1, keepdims=True) + ssq_pe) * (1.0 / cfg.qk_dim) + cfg.eps)
        k_ref[h, :, :LANES] = (kn * ga * s).astype(k_ref.dtype)
        k_ref[h, :, LANES:] = (kr * s).astype(k_ref.dtype)
        v_ref[h, :, :LANES] = y[:, LANES:].astype(v_ref.dtype)
        v_ref[h, :, LANES:] = jnp.where(lane == 0, 1.0, 0.0).astype(v_ref.dtype)


def _kv_up(cfg, z, g_kva, w_kv, ga, gb, cs, tm):
    t = cfg.n_tokens
    h = cfg.n_heads
    r = cfg.kv_rank
    col_c = (cfg.f_width + cfg.q_rank) // r
    col_pe = (cfg.f_width + cfg.q_rank + cfg.kv_rank) // LANES
    vmem = 2 * tm * (r + 128) * 2 + 2 * h * r * 256 * 2 + 4 * h * tm * 512 * 2 + tm * r * 8 + (8 << 20)
    return pl.pallas_call(
        functools.partial(_kv_up_kernel, cfg),
        grid=(t // tm,),
        in_specs=[pl.BlockSpec((tm, r), lambda i: (i, col_c)),
                  pl.BlockSpec((tm, LANES), lambda i: (i, col_pe)),
                  pl.BlockSpec((1, r), lambda i: (0, 0)),
                  pl.BlockSpec((h, r, 256), lambda i: (0, 0, 0)),
                  pl.BlockSpec((1, LANES), lambda i: (0, 0)),
                  pl.BlockSpec((1, LANES), lambda i: (0, 0)),
                  pl.BlockSpec((tm, LANES), lambda i: (_pos_block(cfg, i, tm), 0))],
        out_specs=[pl.BlockSpec((h, tm, 256), lambda i: (0, i, 0)),
                   pl.BlockSpec((h, tm, 256), lambda i: (0, i, 0))],
        out_shape=[jax.ShapeDtypeStruct((h, t, 256), BF16),
                   jax.ShapeDtypeStruct((h, t, 256), BF16)],
        compiler_params=_params(("arbitrary",), vmem),
        name="kv_up",
    )(z, z, g_kva, w_kv, ga, gb, cs)


def _chunk_len(cfg):
    chunk = max(s for _, s in cfg.streams)
    for (b, s), off in zip(cfg.streams, cfg.tok_offsets):
        assert chunk % s == 0 and (b * s) % chunk == 0 and off % chunk == 0
    return chunk


def _in_stream_chunks(cfg, k, c, chunk):
    lo = cfg.tok_offsets[k] // chunk
    hi = lo + cfg.streams[k][0] * cfg.streams[k][1] // chunk
    return jnp.logical_and(c >= lo, c < hi)


def _attn_kernel(cfg, chunk, tq, tk, q_ref, k_ref, v_ref, o_ref):
    c = pl.program_id(0)
    qi = pl.program_id(2)
    for k, (_, s) in enumerate(cfg.streams):
        @pl.when(_in_stream_chunks(cfg, k, c, chunk))
        def _(s=s):
            start = ((qi * tq) // s) * s
            q = q_ref[0]
            m = jnp.full((tq, 1), -jnp.inf, F32)
            acc = jnp.zeros((tq, 2 * LANES), F32)
            for j in range(s // tk):
                rows = pl.ds(pl.multiple_of(start + j * tk, tk), tk)
                sc = lax.dot_general(q, k_ref[0, rows, :], (((1,), (1,)), ((), ())),
                                     preferred_element_type=F32)
                m_new = jnp.maximum(m, jnp.max(sc, axis=-1, keepdims=True))
                p = jnp.exp2((sc - m_new).astype(BF16))
                acc = jnp.exp2(m - m_new) * acc + jnp.dot(p, v_ref[0, rows, :], preferred_element_type=F32)
                m = m_new
            o_ref[...] = (acc[:, :LANES] / acc[:, LANES:LANES + 1]).astype(o_ref.dtype)


def _attn(cfg, q, k, v):
    chunk = _chunk_len(cfg)
    h = cfg.n_heads
    t = cfg.n_tokens
    s_min = min(s for _, s in cfg.streams)
    tq = _tile(s_min, 1024)
    tk = _tile(s_min, 256)
    nq = chunk // tq
    vmem = 4 * tq * 256 * 2 + 2 * chunk * 512 * 2 + 6 * tq * tk * 4 + 6 * tq * 256 * 4 + (8 << 20)
    return pl.pallas_call(
        functools.partial(_attn_kernel, cfg, chunk, tq, tk),
        grid=(t // chunk, h, nq),
        in_specs=[pl.BlockSpec((1, tq, 256), lambda ci, hi, qi: (hi, ci * nq + qi, 0)),
                  pl.BlockSpec((1, chunk, 256), lambda ci, hi, qi: (hi, ci, 0)),
                  pl.BlockSpec((1, chunk, 256), lambda ci, hi, qi: (hi, ci, 0))],
        out_specs=pl.BlockSpec((tq, LANES), lambda ci, hi, qi: (ci * nq + qi, hi)),
        out_shape=jax.ShapeDtypeStruct((t, h * LANES), BF16),
        compiler_params=_params(("arbitrary", "arbitrary", "arbitrary"), vmem),
        name="attn",
    )(q, k, v)


def _fold_kernel(cc_ref, sc_ref, w_ref, o_ref):
    w = w_ref[0]
    o_ref[0, :, :256] = jnp.dot(cc_ref[...], w, preferred_element_type=F32,
                                precision=lax.Precision.HIGHEST).astype(o_ref.dtype)
    o_ref[0, :, 256:] = jnp.dot(sc_ref[...], w, preferred_element_type=F32,
                                precision=lax.Precision.HIGHEST).astype(o_ref.dtype)


def _fold_fourier_weights(cfg, cc, sc, w_fmix):
    g, c = cfg.f_groups, cfg.f_gdim
    return pl.pallas_call(
        _fold_kernel,
        grid=(g,),
        in_specs=[pl.BlockSpec((c, c), lambda i: (0, 0)),
                  pl.BlockSpec((c, c), lambda i: (0, 0)),
                  pl.BlockSpec((1, c, c), lambda i: (i, 0, 0))],
        out_specs=pl.BlockSpec((1, c, 2 * c), lambda i: (i, 0, 0)),
        out_shape=jax.ShapeDtypeStruct((g, c, 2 * c), BF16),
        compiler_params=_params(("arbitrary",), 16 << 20),
        name="fold_fourier",
    )(cc, sc, w_fmix)


MID_ROWS = 16


def _half_block(cfg, i, tm):
    hr0 = i * tm
    out = None
    for k, (b, s) in enumerate(cfg.streams):
        local = hr0 - cfg.tok_offsets[k] // 2
        seq = local // (s // 2)
        s0 = local % (s // 2)
        vals = (cfg.tok_offsets[k] + seq * s + s0, s0, cfg.row_offsets[k] + seq, s)
        if out is None:
            out = vals
        else:
            here = hr0 >= cfg.tok_offsets[k] // 2
            out = tuple(jnp.where(here, v, o) for v, o in zip(vals, out))
    return out


def _four1_kernel(cfg, tm, f_ref, r_ref, n_ref, m_ref, j_ref, w_ref, ab_ref, mid_ref):
    _, s0, _, _ = _half_block(cfg, pl.program_id(0), tm)
    c = cfg.f_gdim
    row = lax.broadcasted_iota(I32, (tm, 1), 0)
    for g in range(cfg.f_groups):
        sl = slice(g * c, (g + 1) * c)
        rev = jnp.dot(j_ref[...], r_ref[:, sl], preferred_element_type=F32)
        rev0 = jnp.where(s0 == 0, 0.0, n_ref[0:1, sl].astype(F32))
        rev = jnp.where(row == 0, rev0, rev)
        f = f_ref[:, sl].astype(F32)
        a = jnp.dot((f + rev).astype(BF16), w_ref[g, :, :c], preferred_element_type=F32)
        b = jnp.dot((f - rev).astype(BF16), w_ref[g, :, c:], preferred_element_type=F32)
        ab_ref[0, :, sl] = a.astype(ab_ref.dtype)
        ab_ref[1, :, sl] = b.astype(ab_ref.dtype)

    @pl.when(s0 == 0)
    def _():
        for g in range(cfg.f_groups):
            sl = slice(g * c, (g + 1) * c)
            mid_ref[:, sl] = jnp.dot(m_ref[:, sl], w_ref[g, :, :c], preferred_element_type=F32)


def _four1(cfg, z, w_fold, flip, tm):
    t = cfg.n_tokens
    fw = cfg.f_width
    g, c = cfg.f_groups, cfg.f_gdim

    def cur(i):
        return (_half_block(cfg, i, tm)[0] // tm, 0)

    def partner(i):
        row, s0, _, s = _half_block(cfg, i, tm)
        return ((row - s0 + s - s0) // tm - 1, 0)

    def after_partner(i):
        row, s0, _, s = _half_block(cfg, i, tm)
        return (jnp.minimum((row - s0 + s - s0) // MID_ROWS, t // MID_ROWS - 1), 0)

    def middle(i):
        row, s0, _, s = _half_block(cfg, i, tm)
        return ((row - s0 + s // 2) // MID_ROWS, 0)

    vmem = 4 * tm * fw * 2 + 2 * g * c * 2 * c * 2 + 4 * tm * fw * 2 + 6 * tm * c * 4 + (8 << 20)
    return pl.pallas_call(
        functools.partial(_four1_kernel, cfg, tm),
        grid=(t // 2 // tm,),
        in_specs=[pl.BlockSpec((tm, fw), cur),
                  pl.BlockSpec((tm, fw), partner),
                  pl.BlockSpec((MID_ROWS, fw), after_partner),
                  pl.BlockSpec((MID_ROWS, fw), middle),
                  pl.BlockSpec((tm, tm), lambda i: (0, 0)),
                  pl.BlockSpec((g, c, 2 * c), lambda i: (0, 0, 0))],
        out_specs=[pl.BlockSpec((2, tm, fw), lambda i: (0, i, 0)),
                   pl.BlockSpec((MID_ROWS, fw), lambda i: (_half_block(cfg, i, tm)[2], 0))],
        out_shape=[jax.ShapeDtypeStruct((2, t // 2, fw), BF16),
                   jax.ShapeDtypeStruct((cfg.n_rows * MID_ROWS, fw), F32)],
        compiler_params=_params(("arbitrary",), vmem),
        name="four1",
    )(z, z, z, z, flip, w_fold)


def _four2_kernel(cfg, chunk, tm, *refs):
    ns = len(cfg.streams)
    ab_ref, mid_ref, o_ref = refs[2 * ns:]
    c = pl.program_id(0)
    mi = pl.program_id(2)
    row = lax.broadcasted_iota(I32, (tm, 1), 0)
    for k, (_, s) in enumerate(cfg.streams):
        @pl.when(_in_stream_chunks(cfg, k, c, chunk))
        def _(k=k, s=s):
            half = s // 2
            start = pl.multiple_of(((mi * tm) // s) * half, half)
            y = jnp.dot(refs[2 * k][...], ab_ref[0, pl.ds(start, half), :], preferred_element_type=F32)
            y = y + jnp.dot(refs[2 * k + 1][...], ab_ref[1, pl.ds(start, half), :], preferred_element_type=F32)
            sign = jnp.where(row % 2 == 0, s ** -0.5, -(s ** -0.5))
            o_ref[...] = (y + sign * mid_ref[0:1, :]).astype(o_ref.dtype)


def _four2(cfg, dfts, ab, mid):
    chunk = _chunk_len(cfg)
    t = cfg.n_tokens
    fw = cfg.f_width
    tm = _tile(min(s for _, s in cfg.streams), 512)
    tn = _tile(fw, 1024)
    nm = chunk // tm
    assert tm % 2 == 0

    def d_spec(k):
        s = cfg.streams[k][1]

        def imap(ci, ni, mi):
            inside = _in_stream_chunks(cfg, k, ci, chunk)
            before = ci < cfg.tok_offsets[k] // chunk
            return (jnp.where(inside, ((mi * tm) % s) // tm, jnp.where(before, 0, s // tm - 1)), 0)
        return pl.BlockSpec((tm, s // 2), imap)

    in_specs, args, vmem = [], [], 0
    for k, (dc, ds) in enumerate(dfts):
        in_specs += [d_spec(k), d_spec(k)]
        args += [dc, ds]
        vmem += 4 * tm * (cfg.streams[k][1] // 2) * 2
    in_specs.append(pl.BlockSpec((2, chunk // 2, tn), lambda ci, ni, mi: (0, ci, ni)))
    in_specs.append(pl.BlockSpec((MID_ROWS, tn), lambda ci, ni, mi: (_mod_row(cfg, ci * chunk + mi * tm), ni)))
    vmem += 4 * (chunk // 2) * tn * 2 + 2 * tm * tn * 2 + 3 * tm * tn * 4 + (8 << 20)
    return pl.pallas_call(
        functools.partial(_four2_kernel, cfg, chunk, tm),
        grid=(t // chunk, fw // tn, nm),
        in_specs=in_specs,
        out_specs=pl.BlockSpec((tm, tn), lambda ci, ni, mi: (ci * nm + mi, ni)),
        out_shape=jax.ShapeDtypeStruct((t, fw), BF16),
        compiler_params=_params(("arbitrary", "arbitrary", "arbitrary"), vmem),
        name="four2",
    )(*args, ab, mid)


def _out_proj_kernel(cfg, tm, *refs):
    ns = len(cfg.streams)
    x_refs = refs[:ns]
    gt_ref, a_ref, f_ref, wa_ref, wb_ref, o_ref = refs[ns:]
    i = pl.program_id(0)
    mix = jnp.dot(a_ref[...], wa_ref[...], preferred_element_type=F32)
    mix = mix + jnp.dot(f_ref[...], wb_ref[...], preferred_element_type=F32)
    for k in range(ns):
        @pl.when(_in_stream(cfg, k, i, tm))
        def _(k=k):
            o_ref[...] = x_refs[k][...] + gt_ref[0] * mix


def _out_proj(cfg, xs, mod3, o_attn, o_four, wa, wb):
    d = cfg.d_model
    t = cfg.n_tokens
    aw, fw = cfg.attn_width, cfg.f_width
    tm = _tile(min(s for _, s in cfg.streams), 1024)
    tn = _tile(d, 512)
    nj = d // tn

    def x_spec(k):
        def imap(i, j):
            inside = _in_stream(cfg, k, i, tm)
            before = i < cfg.tok_offsets[k] // tm
            return (_stream_block(cfg, k, i, tm), jnp.where(inside, j, jnp.where(before, 0, nj - 1)))
        return pl.BlockSpec((tm, tn), imap)

    in_specs = [x_spec(k) for k in range(len(xs))] + [
        pl.BlockSpec((1, 1, tn), lambda i, j: (_mod_row(cfg, i * tm), 0, 2 * nj + j)),
        pl.BlockSpec((tm, aw), lambda i, j: (i, 0)),
        pl.BlockSpec((tm, fw), lambda i, j: (i, 0)),
        pl.BlockSpec((aw, tn), lambda i, j: (0, j)),
        pl.BlockSpec((fw, tn), lambda i, j: (0, j)),
    ]
    vmem = (2 * tm * tn * 4 * len(xs) + 2 * tm * (aw + fw) * 2 + 2 * (aw + fw) * tn * 2
            + 2 * tm * tn * 4 + 2 * tm * tn * 4 + (8 << 20))
    return pl.pallas_call(
        functools.partial(_out_proj_kernel, cfg, tm),
        grid=(t // tm, nj),
        in_specs=in_specs,
        out_specs=pl.BlockSpec((tm, tn), lambda i, j: (i, j)),
        out_shape=jax.ShapeDtypeStruct((t, d), F32),
        compiler_params=_params(("arbitrary", "arbitrary"), vmem),
        name="out_proj",
    )(*xs, mod3, o_attn, o_four, wa, wb)


def _router_kernel(cfg, tm, x_ref, sh_ref, sc_ref, g_ref, wr_ref, tri_ref,
                   hp_ref, mi_ref, gt_ref, cnt_ref, carry_ref):
    i = pl.program_id(0)
    ng, ne = cfg.n_groups, cfg.e_per_group
    n_exp = cfg.n_experts

    @pl.when(i == 0)
    def _():
        carry_ref[...] = jnp.zeros_like(carry_ref)

    h2 = _rms(x_ref[...], cfg.eps) * g_ref[...] * (1.0 + sc_ref[0]) + sh_ref[0]
    for c in range(cfg.d_model // PACK_CHUNK):
        lo = h2[:, c * PACK_CHUNK:c * PACK_CHUNK + LANES]
        hi = h2[:, c * PACK_CHUNK + LANES:(c + 1) * PACK_CHUNK]
        hp_ref[:, c * LANES:(c + 1) * LANES] = _pack_words(lo, hi)

    h_hi = h2.astype(BF16)
    h_lo = (h2 - h_hi.astype(F32)).astype(BF16)
    p1 = jnp.dot(h_hi, wr_ref[...], preferred_element_type=F32)
    p2 = jnp.dot(h_lo, wr_ref[:, :LANES], preferred_element_type=F32)
    logits = (p1[:, :LANES] + p1[:, LANES:] + p2).T

    gl = logits[0:ng]
    io = lax.broadcasted_iota(I32, (ng, tm), 0)
    gm = jnp.max(gl, axis=0, keepdims=True)
    p_group = 1.0 / jnp.sum(jnp.exp(gl - gm), axis=0, keepdims=True)
    gidx = jnp.min(jnp.where(gl == gm, io, ng), axis=0, keepdims=True)
    sel = jnp.zeros((ne, tm), F32)
    for g in range(ng):
        sel = jnp.where(gidx == g, logits[ng + g * ne:ng + (g + 1) * ne], sel)
    ie = lax.broadcasted_iota(I32, (ne, tm), 0)
    m1 = jnp.max(sel, axis=0, keepdims=True)
    i1 = jnp.min(jnp.where(sel == m1, ie, ne), axis=0, keepdims=True)
    sel2 = jnp.where(ie == i1, -jnp.inf, sel)
    m2 = jnp.max(sel2, axis=0, keepdims=True)
    i2 = jnp.min(jnp.where(sel2 == m2, ie, ne), axis=0, keepdims=True)
    e21 = jnp.exp(m2 - m1)
    gate0 = p_group / (1.0 + e21)
    gate1 = p_group * e21 / (1.0 + e21)
    e0 = gidx * ne + i1
    e1 = gidx * ne + i2

    ix = lax.broadcasted_iota(I32, (n_exp, tm), 0)
    hit0 = ix == e0
    hit1 = ix == e1
    member = jnp.logical_or(hit0, hit1).astype(F32)
    before = jnp.dot(member.astype(BF16), tri_ref[...], preferred_element_type=F32)
    total = before + carry_ref[:, 0:1]
    r0 = jnp.sum(jnp.where(hit0, total, 0.0), axis=0, keepdims=True).astype(I32)
    r1 = jnp.sum(jnp.where(hit1, total, 0.0), axis=0, keepdims=True).astype(I32)
    carry_ref[...] = carry_ref[...] + jnp.sum(member, axis=1, keepdims=True)
    cnt_ref[...] = carry_ref[...].astype(I32)

    i8 = lax.broadcasted_iota(I32, (8, tm), 0)
    mi_ref[...] = jnp.where(i8 == 0, e0, jnp.where(i8 == 1, e1, jnp.where(i8 == 2, r0, jnp.where(i8 == 3, r1, 0))))
    il = lax.broadcasted_iota(I32, (LANES, tm), 0)
    gt_ref[...] = jnp.where(il == 0, gate0, jnp.where(il == 1, gate1, 0.0)).T


def _router(cfg, x1, mod3, g_ffn, wr_cat, tri, tm):
    d = cfg.d_model
    t = cfg.n_tokens
    ne = cfg.n_experts
    vmem = 2 * tm * d * 4 + 2 * tm * d * 2 + 2 * d * 256 * 2 + 6 * tm * d * 4 + 2 * tm * tm * 2 + (8 << 20)
    return pl.pallas_call(
        functools.partial(_router_kernel, cfg, tm),
        grid=(t // tm,),
        in_specs=[pl.BlockSpec((tm, d), lambda i: (i, 0)),
                  pl.BlockSpec((1, 1, d), lambda i: (_mod_row(cfg, i * tm), 0, 3)),
                  pl.BlockSpec((1, 1, d), lambda i: (_mod_row(cfg, i * tm), 0, 4)),
                  pl.BlockSpec((1, d), lambda i: (0, 0)),
                  pl.BlockSpec((d, 256), lambda i: (0, 0)),
                  pl.BlockSpec((tm, tm), lambda i: (0, 0))],
        out_specs=[pl.BlockSpec((tm, d // 2), lambda i: (i, 0)),
                   pl.BlockSpec((8, tm), lambda i: (0, i)),
                   pl.BlockSpec((tm, LANES), lambda i: (i, 0)),
                   pl.BlockSpec((ne, LANES), lambda i: (0, 0))],
        out_shape=[jax.ShapeDtypeStruct((t, d // 2), U32),
                   jax.ShapeDtypeStruct((8, t), I32),
                   jax.ShapeDtypeStruct((t, LANES), F32),
                   jax.ShapeDtypeStruct((ne, LANES), I32)],
        scratch_shapes=[pltpu.VMEM((ne, LANES), F32)],
        compiler_params=_params(("arbitrary",), vmem),
        name="router",
    )(x1, mod3, mod3, g_ffn, wr_cat, tri)


def _dest_row(start_ref, meta_ref, kk, r):
    return start_ref[meta_ref[kk, r]] + meta_ref[2 + kk, r]


def _dispatch_kernel(tm, start_ref, meta_ref, h_ref, xs_in_ref, xs_ref, sem):
    del xs_in_ref

    def issue(r, carry):
        for kk in range(2):
            dst = xs_ref.at[pl.ds(_dest_row(start_ref, meta_ref, kk, r), 1)]
            pltpu.make_async_copy(h_ref.at[pl.ds(r, 1)], dst, sem).start(priority=kk)
        return carry

    lax.fori_loop(0, tm, issue, 0, unroll=8)
    for kk in range(2):
        pltpu.make_async_copy(h_ref, xs_ref.at[pl.ds(0, tm)], sem).wait()


def _dispatch(cfg, pad_start, meta, h2p, xs_init, tm):
    t = cfg.n_tokens
    w = h2p.shape[1]
    grid_spec = pltpu.PrefetchScalarGridSpec(
        num_scalar_prefetch=1,
        grid=(t // tm,),
        in_specs=[pl.BlockSpec((8, tm), lambda i, st: (0, i), memory_space=pltpu.SMEM),
                  pl.BlockSpec((tm, w), lambda i, st: (i, 0)),
                  pl.BlockSpec(memory_space=pl.ANY)],
        out_specs=pl.BlockSpec(memory_space=pl.ANY),
        scratch_shapes=[pltpu.SemaphoreType.DMA(())],
    )
    return pl.pallas_call(
        functools.partial(_dispatch_kernel, tm),
        grid_spec=grid_spec,
        out_shape=jax.ShapeDtypeStruct(xs_init.shape, xs_init.dtype),
        input_output_aliases={3: 0},
        compiler_params=_params(("arbitrary",), 4 * tm * w * 4 + (4 << 20)),
        name="dispatch",
    )(pad_start, meta, h2p, xs_init)


def _expert_schedule(cfg, cnt, n_blocks):
    bm = cfg.moe_block
    padded = (cnt + bm - 1) // bm * bm
    pad_end = jnp.cumsum(padded)
    pad_start = (pad_end - padded).astype(I32)
    blk = jnp.arange(n_blocks, dtype=I32)
    be = jnp.minimum(jnp.sum((pad_end[None, :] <= (blk * bm)[:, None]).astype(I32), axis=1), cfg.n_experts - 1)
    nv = (pad_end[-1] // bm).astype(I32)
    prev = jnp.concatenate([jnp.full((1,), -1, I32), be[:-1]])
    first = jnp.logical_and(blk < nv, be != prev)
    suffix = lax.cummin(jnp.where(first, blk, n_blocks), axis=0, reverse=True)
    next_first = jnp.concatenate([suffix[1:], jnp.full((1,), n_blocks, I32)])
    last = next_first >= n_blocks
    nxt = be[jnp.where(last, 0, next_first)]
    run = jnp.cumsum(first.astype(I32)) - 1
    n_runs = jnp.sum(first.astype(I32))
    half = (cnt[be] - (blk * bm - pad_start[be])) <= bm // 2
    tables = (be.astype(I32), first.astype(I32), nxt.astype(I32), last.astype(I32), run.astype(I32),
              half.astype(I32))
    return pad_start, tables + (n_runs[None], nv[None])


N_SCHED = 8


def _stage_expert_weights(sched, i, tile, n_tiles, copies):
    be_ref, first_ref, nxt_ref, last_ref, run_ref, _, n_runs_ref, nv_ref = sched
    slot = (tile * n_runs_ref[0] + run_ref[i]) % 2

    @pl.when(jnp.logical_and(i < nv_ref[0], first_ref[i] == 1))
    def _():
        @pl.when(jnp.logical_and(tile == 0, i == 0))
        def _():
            for cp in copies:
                cp(be_ref[i], tile, slot).start()

        for cp in copies:
            cp(be_ref[i], tile, slot).wait()
        is_last = last_ref[i] == 1

        @pl.when(jnp.logical_not(is_last))
        def _():
            for cp in copies:
                cp(nxt_ref[i], tile, 1 - slot).start()

        @pl.when(jnp.logical_and(is_last, tile + 1 < n_tiles))
        def _():
            for cp in copies:
                cp(nxt_ref[i], tile + 1, 1 - slot).start()

    return slot


def _moe1_kernel(cfg, tf, n_tiles, *refs):
    sched = refs[:N_SCHED]
    x_ref, w1_ref, w3_ref, o_ref, xb_ref, s1_ref, s3_ref, sem = refs[N_SCHED:]
    nv_ref = sched[-1]
    j = pl.program_id(0)
    i = pl.program_id(1)

    def copy(w_ref, stage_ref, k):
        def make(e, jj, slot):
            src = w_ref.at[e, :, pl.ds(pl.multiple_of(jj * tf, tf), tf)]
            return pltpu.make_async_copy(src, stage_ref.at[slot], sem.at[k, slot])
        return make

    slot = _stage_expert_weights(sched, i, j, n_tiles, [copy(w1_ref, s1_ref, 0), copy(w3_ref, s3_ref, 1)])
    bm = x_ref.shape[0]
    valid = i < nv_ref[0]
    half = sched[5][i] == 1

    def compute(m):
        for c in range(cfg.d_model // PACK_CHUNK):
            lo, hi = _unpack_words(x_ref[:m, c * LANES:(c + 1) * LANES])
            xb_ref[:m, c * PACK_CHUNK:c * PACK_CHUNK + LANES] = lo.astype(BF16)
            xb_ref[:m, c * PACK_CHUNK + LANES:(c + 1) * PACK_CHUNK] = hi.astype(BF16)
        x = xb_ref[:m, :]
        a = jnp.dot(x, s1_ref[slot].astype(BF16), preferred_element_type=F32)
        b = jnp.dot(x, s3_ref[slot].astype(BF16), preferred_element_type=F32)
        o_ref[:m, :] = (a * jax.nn.sigmoid(a) * b).astype(o_ref.dtype)
        if m < bm:
            o_ref[m:, :] = jnp.zeros((bm - m, o_ref.shape[1]), o_ref.dtype)

    @pl.when(jnp.logical_and(valid, jnp.logical_not(half)))
    def _():
        compute(bm)

    @pl.when(jnp.logical_and(valid, half))
    def _():
        compute(bm // 2)

    @pl.when(i >= nv_ref[0])
    def _():
        o_ref[...] = jnp.zeros_like(o_ref)


def _moe1(cfg, sched, xs, w1, w3):
    d, f = cfg.d_model, cfg.d_expert
    bm = cfg.moe_block
    rows = xs.shape[0]
    tf = _tile(f, cfg.moe_up_tile)
    n_tiles = f // tf

    def x_map(j, i, *sched):
        return (jnp.minimum(i, sched[-1][0] - 1), 0)

    grid_spec = pltpu.PrefetchScalarGridSpec(
        num_scalar_prefetch=N_SCHED,
        grid=(n_tiles, rows // bm),
        in_specs=[pl.BlockSpec((bm, d // 2), x_map),
                  pl.BlockSpec(memory_space=pl.ANY),
                  pl.BlockSpec(memory_space=pl.ANY)],
        out_specs=pl.BlockSpec((bm, tf), lambda j, i, *_: (i, j)),
        scratch_shapes=[pltpu.VMEM((bm, d), BF16),
                        pltpu.VMEM((2, d, tf), F32), pltpu.VMEM((2, d, tf), F32),
                        pltpu.SemaphoreType.DMA((2, 2))],
    )
    vmem = 2 * bm * d * 2 + bm * d * 2 + 4 * d * tf * 4 + 2 * d * tf * 2 + 2 * bm * tf * 2 + 4 * bm * tf * 4 + (4 << 20)
    return pl.pallas_call(
        functools.partial(_moe1_kernel, cfg, tf, n_tiles),
        grid_spec=grid_spec,
        out_shape=jax.ShapeDtypeStruct((rows, f), BF16),
        compiler_params=_params(("arbitrary", "arbitrary"), vmem),
        name="moe1",
    )(*sched, xs, w1, w3)


def _moe2_kernel(cfg, tn, n_tiles, *refs):
    sched = refs[:N_SCHED]
    h_ref, w2_ref, o_ref, s2_ref, sem = refs[N_SCHED:]
    nv_ref = sched[-1]
    n = pl.program_id(0)
    i = pl.program_id(1)

    def copy(e, nn, slot):
        src = w2_ref.at[e, :, pl.ds(pl.multiple_of(nn * tn, tn), tn)]
        return pltpu.make_async_copy(src, s2_ref.at[slot], sem.at[slot])

    slot = _stage_expert_weights(sched, i, n, n_tiles, [copy])
    bm = h_ref.shape[0]
    valid = i < nv_ref[0]
    half = sched[5][i] == 1

    def compute(m):
        y = jnp.dot(h_ref[:m, :], s2_ref[slot].astype(BF16), preferred_element_type=F32)
        for c in range(tn // PACK_CHUNK):
            o_ref[:m, c * LANES:(c + 1) * LANES] = _pack_words(
                y[:, c * PACK_CHUNK:c * PACK_CHUNK + LANES], y[:, c * PACK_CHUNK + LANES:(c + 1) * PACK_CHUNK])
        if m < bm:
            o_ref[m:, :] = jnp.zeros((bm - m, o_ref.shape[1]), o_ref.dtype)

    @pl.when(jnp.logical_and(valid, jnp.logical_not(half)))
    def _():
        compute(bm)

    @pl.when(jnp.logical_and(valid, half))
    def _():
        compute(bm // 2)

    @pl.when(i >= nv_ref[0])
    def _():
        o_ref[...] = jnp.zeros_like(o_ref)


def _moe2(cfg, sched, hmid, w2):
    d, f = cfg.d_model, cfg.d_expert
    bm = cfg.moe_block
    rows = hmid.shape[0]
    tn = _tile(d, cfg.moe_down_tile)
    assert tn % PACK_CHUNK == 0
    n_tiles = d // tn

    def h_map(n, i, *sched):
        return (jnp.minimum(i, sched[-1][0] - 1), 0)

    grid_spec = pltpu.PrefetchScalarGridSpec(
        num_scalar_prefetch=N_SCHED,
        grid=(n_tiles, rows // bm),
        in_specs=[pl.BlockSpec((bm, f), h_map),
                  pl.BlockSpec(memory_space=pl.ANY)],
        out_specs=pl.BlockSpec((bm, tn // 2), lambda n, i, *_: (i, n)),
        scratch_shapes=[pltpu.VMEM((2, f, tn), F32), pltpu.SemaphoreType.DMA((2,))],
    )
    vmem = 2 * bm * f * 2 + 2 * f * tn * 4 + f * tn * 2 + 2 * bm * tn * 2 + 2 * bm * tn * 4 + (8 << 20)
    return pl.pallas_call(
        functools.partial(_moe2_kernel, cfg, tn, n_tiles),
        grid_spec=grid_spec,
        out_shape=jax.ShapeDtypeStruct((rows, d // 2), U32),
        compiler_params=_params(("arbitrary", "arbitrary"), vmem),
        name="moe2",
    )(*sched, hmid, w2)


def _combine_kernel(cfg, tm, n_blk, start_ref, meta_ref, meta_nxt_ref, x_ref, gt2_ref, gate_ref, y_ref, o_ref,
                    buf_ref, sem):
    i = pl.program_id(0)
    slot = i % 2

    n_chunks = cfg.d_model // PACK_CHUNK
    rows_per_chunk = tm // n_chunks

    def start_row(m_ref, r, sl):
        for kk in range(2):
            src = y_ref.at[pl.ds(_dest_row(start_ref, m_ref, kk, r), 1)]
            pltpu.make_async_copy(src, buf_ref.at[sl, kk, pl.ds(r, 1)], sem.at[sl]).start(priority=kk)

    def wait_slot(sl):
        for kk in range(2):
            pltpu.make_async_copy(y_ref.at[pl.ds(0, tm)], buf_ref.at[sl, kk], sem.at[sl]).wait()

    @pl.when(i == 0)
    def _():
        def body(r, carry):
            start_row(meta_ref, r, slot)
            return carry
        lax.fori_loop(0, tm, body, 0, unroll=8)

    wait_slot(slot)

    g0 = gate_ref[:, 0:1]
    g1 = gate_ref[:, 1:2]
    for c in range(n_chunks):
        lo0, hi0 = _unpack_words(buf_ref[slot, 0, :, c * LANES:(c + 1) * LANES])
        lo1, hi1 = _unpack_words(buf_ref[slot, 1, :, c * LANES:(c + 1) * LANES])
        sl_lo = slice(c * PACK_CHUNK, c * PACK_CHUNK + LANES)
        sl_hi = slice(c * PACK_CHUNK + LANES, (c + 1) * PACK_CHUNK)
        o_ref[:, sl_lo] = x_ref[:, sl_lo] + gt2_ref[0, :, sl_lo] * (g0 * lo0 + g1 * lo1)
        o_ref[:, sl_hi] = x_ref[:, sl_hi] + gt2_ref[0, :, sl_hi] * (g0 * hi0 + g1 * hi1)
        for r in range(c * rows_per_chunk, (c + 1) * rows_per_chunk):
            start_row(meta_nxt_ref, r, 1 - slot)

    @pl.when(i == n_blk - 1)
    def _():
        wait_slot(1 - slot)


def _combine(cfg, k_idx, pad_start, meta, x1, mod3, gates, y, tm):
    b, s = cfg.streams[k_idx]
    off = cfg.tok_offsets[k_idx]
    d = cfg.d_model
    n = b * s
    ob = off // tm
    row0 = cfg.row_offsets[k_idx]
    n_blk = n // tm
    grid_spec = pltpu.PrefetchScalarGridSpec(
        num_scalar_prefetch=1,
        grid=(n_blk,),
        in_specs=[pl.BlockSpec((8, tm), lambda i, st: (0, ob + i), memory_space=pltpu.SMEM),
                  pl.BlockSpec((8, tm), lambda i, st: (0, ob + jnp.minimum(i + 1, n_blk - 1)),
                               memory_space=pltpu.SMEM),
                  pl.BlockSpec((tm, d), lambda i, st: (ob + i, 0)),
                  pl.BlockSpec((1, 1, d), lambda i, st: (row0 + (i * tm) // s, 0, 5)),
                  pl.BlockSpec((tm, LANES), lambda i, st: (ob + i, 0)),
                  pl.BlockSpec(memory_space=pl.ANY)],
        out_specs=pl.BlockSpec((tm, d), lambda i, st: (i, 0)),
        scratch_shapes=[pltpu.VMEM((2, 2, tm, d // 2), U32), pltpu.SemaphoreType.DMA((2,))],
    )
    vmem = 4 * tm * d * 4 + 4 * tm * d * 2 + 2 * tm * 128 * 4 + 6 * tm * 128 * 4 + (8 << 20)
    return pl.pallas_call(
        functools.partial(_combine_kernel, cfg, tm, n_blk),
        grid_spec=grid_spec,
        out_shape=jax.ShapeDtypeStruct((n, d), F32),
        compiler_params=_params(("arbitrary",), vmem),
        name=f"combine{k_idx}",
    )(pad_start, meta, meta, x1, mod3, gates, y)


def _rot_half(a, axis=-1):
    h = a.shape[axis] // 2
    lo = lax.slice_in_dim(a, 0, h, axis=axis)
    hi = lax.slice_in_dim(a, h, 2 * h, axis=axis)
    return jnp.concatenate([hi, lo], axis=axis)


def _rope_table(cfg, n_pos):
    half = cfg.rope // 2
    inv_freq = cfg.theta ** (-2.0 * jnp.arange(half, dtype=F32) / cfg.rope)
    ang = jnp.arange(n_pos, dtype=F32)[:, None] * inv_freq[None, :]
    cos, sin = jnp.cos(ang), jnp.sin(ang)
    return jnp.concatenate([cos, cos, -sin, sin], axis=-1)


def _dft(n, scale):
    idx = jnp.arange(n, dtype=I32)
    ang = ((idx[:, None] * idx[None, :]) % n).astype(F32) * (2.0 * math.pi / n)
    return jnp.cos(ang) * scale, jnp.sin(ang) * scale


def _dft_half(n, split=64):
    s = jnp.arange(n // 2, dtype=I32)

    def table(rows):
        ang = ((rows[:, None] * s[None, :]) % n).astype(F32) * (2.0 * math.pi / n)
        return jnp.cos(ang), jnp.sin(ang)

    ch, sh = table(jnp.arange(n // split, dtype=I32) * split)
    cl, sl = table(jnp.arange(split, dtype=I32))
    scale = n ** -0.5
    cos = (ch[:, None, :] * cl[None] - sh[:, None, :] * sl[None]) * scale
    msin = (sh[:, None, :] * cl[None] + ch[:, None, :] * sl[None]) * -scale
    return cos.reshape(n, n // 2).astype(BF16), msin.reshape(n, n // 2).astype(BF16)


def _layer(cfg, xs, cs_, w_ada, b_ada, g_attn, w_in, g_qa, w_uq, g_kva, w_ukv, g_qn, g_kn,
           w_fmix, w_o, g_ffn, w_group, w_route, w1, w3, w2):
    d, h = cfg.d_model, cfg.n_heads
    t = cfg.n_tokens
    nope, rope = cfg.nope, cfg.rope
    assert nope == LANES and cfg.v_dim == LANES and rope == 64 and cfg.f_gdim == PACK_CHUNK
    assert cfg.f_width % cfg.q_rank == 0 and (cfg.f_width + cfg.q_rank) % cfg.kv_rank == 0
    qr, kr = cfg.q_rank, cfg.kv_rank

    wq_in, wkv_in, wpe_in, wf_in = jnp.split(w_in, [qr, qr + kr, qr + kr + rope], axis=1)
    w_cat = jnp.concatenate([wf_in, wq_in, wkv_in, wpe_in, _rot_half(wpe_in)], axis=1)
    n_cat = w_cat.shape[1]
    w_cat = jnp.pad(w_cat, ((0, 0), (0, -n_cat % _in_tile(n_cat)))).astype(BF16)

    wq3 = w_uq.reshape(qr, h, nope + rope)
    w_q = jnp.concatenate([wq3, _rot_half(wq3[..., nope:])], axis=-1).transpose(1, 0, 2).astype(BF16)
    w_kv = w_ukv.reshape(kr, h, nope + cfg.v_dim).transpose(1, 0, 2).astype(BF16)
    q_const = math.sqrt(2.0 * cfg.qk_dim) * cfg.qk_dim ** -0.5 * math.log2(math.e)
    ga_q = g_qn[None, :nope] * q_const
    gb_q = jnp.concatenate([g_qn[nope:], _rot_half(g_qn[nope:])])[None] * q_const
    ga_k, gb_k = g_kn[None, :nope], jnp.concatenate([g_kn[nope:], _rot_half(g_kn[nope:])])[None]
    wa, wb = w_o[:cfg.attn_width].astype(BF16), w_o[cfg.attn_width:].astype(BF16)
    wr = jnp.concatenate([w_group, w_route.reshape(d, cfg.n_experts)], axis=1)
    wr = jnp.pad(wr, ((0, 0), (0, LANES - wr.shape[1])))
    wr_hi = wr.astype(BF16)
    wr_cat = jnp.concatenate([wr_hi, (wr - wr_hi.astype(F32)).astype(BF16)], axis=1)

    s_max = max(s for _, s in cfg.streams)
    cs_tab = _rope_table(cfg, s_max)
    cc, sc = _dft(cfg.f_gdim, cfg.f_gdim ** -0.5)

    c_all = jnp.concatenate(cs_, axis=0)
    rows = -(-c_all.shape[0] // 8) * 8
    c_pad = jnp.pad(c_all, ((0, rows - c_all.shape[0]), (0, 0)))
    mod3 = _ada(c_pad, w_ada, b_ada[None]).reshape(rows, 1, 6 * d)

    x2d = [x.reshape(-1, d) for x in xs]
    z = _in_proj(cfg, x2d, mod3, g_attn[None], w_cat)
    tm = _tile(min(s for _, s in cfg.streams), 512)
    q = _q_up(cfg, z, g_qa[None], w_q, ga_q, gb_q, cs_tab, tm)
    k, v = _kv_up(cfg, z, g_kva[None], w_kv, ga_k, gb_k, cs_tab, tm)
    o_attn = _attn(cfg, q, k, v)

    w_fold = _fold_fourier_weights(cfg, cc, sc, w_fmix)
    tm_f = _tile(min(s for _, s in cfg.streams) // 2, 512)
    idx = jnp.arange(tm_f)
    flip = (idx[:, None] + idx[None, :] == tm_f).astype(BF16)
    ab, mid = _four1(cfg, z, w_fold, flip, tm_f)
    o_four = _four2(cfg, [_dft_half(s) for _, s in cfg.streams], ab, mid)

    x1 = _out_proj(cfg, x2d, mod3, o_attn, o_four, wa, wb)

    tm_r = _tile(tm, 256)
    tri = (jnp.arange(tm_r)[:, None] < jnp.arange(tm_r)[None, :]).astype(BF16)
    h2p, meta, gates, counts = _router(cfg, x1, mod3, g_ffn[None], wr_cat, tri, tm_r)

    bm = cfg.moe_block
    n_blocks = (2 * t) // bm + cfg.n_experts
    pad_start, sched = _expert_schedule(cfg, counts[:, 0], n_blocks)

    xs_rows = _dispatch(cfg, pad_start, meta, h2p, jnp.zeros((n_blocks * bm, d // 2), U32), _tile(tm, 256))
    hmid = _moe1(cfg, sched, xs_rows, w1, w3)
    y = _moe2(cfg, sched, hmid, w2)

    outs = []
    for ki, (b, s) in enumerate(cfg.streams):
        o = _combine(cfg, ki, pad_start, meta, x1, mod3, gates, y, _tile(tm, 256))
        outs.append(o.reshape(b, s, d))
    return tuple(outs)


def kernel(x_prompt, x_sample, c_prompt, c_sample, w_ada, b_ada, g_attn, w_in, g_qa, w_uq, g_kva, w_ukv,
           g_qn, g_kn, w_fmix, w_o, g_ffn, w_group, w_route, w1, w3, w2):
    cfg = Cfg()
    xs = (x_prompt, x_sample)
    cs_ = (c_prompt, c_sample)
    for l in range(w_ada.shape[0]):
        xs = _layer(cfg, xs, cs_, w_ada[l], b_ada[l], g_attn[l], w_in[l], g_qa[l], w_uq[l], g_kva[l],
                    w_ukv[l], g_qn[l], g_kn[l], w_fmix[l], w_o[l], g_ffn[l], w_group[l], w_route[l],
                    w1[l], w3[l], w2[l])
    return xs
```

```python
import dataclasses
import functools
import math

import jax
import jax.numpy as jnp
from jax import lax
from jax.experimental import pallas as pl
from jax.experimental.pallas import tpu as pltpu

BF16 = jnp.bfloat16
F32 = jnp.float32
U32 = jnp.uint32
I32 = jnp.int32

LANES = 128
MXU_DIM = 256
VMEM_CAP = 60 << 20
PACK_CHUNK = 2 * LANES


@dataclasses.dataclass(frozen=True)
class Cfg:
    d_model: int = 4096
    streams: tuple = ((8, 2048), (4, 4096))
    n_heads: int = 16
    nope: int = 128
    rope: int = 64
    v_dim: int = 128
    q_rank: int = 1024
    kv_rank: int = 512
    f_gdim: int = 256
    n_groups: int = 8
    e_per_group: int = 8
    d_expert: int = 1024
    theta: float = 10000.0
    eps: float = 1e-6
    moe_block: int = 512
    moe_up_tile: int = 512
    moe_down_tile: int = 2048

    @property
    def attn_width(self):
        return self.n_heads * self.v_dim

    @property
    def f_width(self):
        return self.d_model - self.attn_width

    @property
    def f_groups(self):
        return self.f_width // self.f_gdim

    @property
    def qk_dim(self):
        return self.nope + self.rope

    @property
    def n_experts(self):
        return self.n_groups * self.e_per_group

    @property
    def tok_offsets(self):
        offs, t = [], 0
        for b, s in self.streams:
            offs.append(t)
            t += b * s
        return tuple(offs)

    @property
    def row_offsets(self):
        offs, r = [], 0
        for b, _ in self.streams:
            offs.append(r)
            r += b
        return tuple(offs)

    @property
    def n_tokens(self):
        return sum(b * s for b, s in self.streams)

    @property
    def n_rows(self):
        return sum(b for b, _ in self.streams)


def _tile(n, want):
    if n <= want:
        return n
    t = want
    while n % t:
        t -= 8
    return t


def _params(sem, vmem_bytes, flags=None):
    return pltpu.CompilerParams(dimension_semantics=sem,
                                vmem_limit_bytes=int(min(VMEM_CAP, vmem_bytes)), flags=flags)


def _mod_row(cfg, t0):
    row = None
    for k, (b, s) in enumerate(cfg.streams):
        expr = cfg.row_offsets[k] + (t0 - cfg.tok_offsets[k]) // s
        row = expr if row is None else jnp.where(t0 >= cfg.tok_offsets[k], expr, row)
    return row


def _pos_block(cfg, i, tm):
    t0 = i * tm
    blk = None
    for k, (b, s) in enumerate(cfg.streams):
        expr = ((t0 - cfg.tok_offsets[k]) % s) // tm
        blk = expr if blk is None else jnp.where(t0 >= cfg.tok_offsets[k], expr, blk)
    return blk


def _stream_block(cfg, k, i, tm):
    nb = cfg.streams[k][0] * cfg.streams[k][1] // tm
    return jnp.clip(i - cfg.tok_offsets[k] // tm, 0, nb - 1)


def _in_stream(cfg, k, i, tm):
    lo = cfg.tok_offsets[k] // tm
    hi = lo + cfg.streams[k][0] * cfg.streams[k][1] // tm
    return jnp.logical_and(i >= lo, i < hi)


def _rms(x, eps):
    return x * lax.rsqrt(jnp.mean(x * x, axis=-1, keepdims=True) + eps)


def _pack_words(a, b):
    wa = lax.bitcast_convert_type(a.astype(BF16).astype(F32), U32) >> 16
    wb = lax.bitcast_convert_type(b.astype(BF16).astype(F32), U32) & jnp.uint32(0xFFFF0000)
    return wa | wb


def _unpack_words(w):
    lo = lax.bitcast_convert_type(w << 16, F32)
    hi = lax.bitcast_convert_type(w & jnp.uint32(0xFFFF0000), F32)
    return lo, hi


def _ada_kernel(c_ref, w_ref, b_ref, o_ref):
    c = c_ref[...]
    s = (c * jax.nn.sigmoid(c)).astype(BF16)
    o_ref[...] = jnp.dot(s, w_ref[...].astype(BF16), preferred_element_type=F32) + b_ref[...]


def _ada(c_pad, w_ada, b_ada):
    r, d = c_pad.shape
    n = w_ada.shape[1]
    tn = _tile(n, 512)
    return pl.pallas_call(
        _ada_kernel,
        grid=(n // tn,),
        in_specs=[pl.BlockSpec((r, d), lambda j: (0, 0)),
                  pl.BlockSpec((d, tn), lambda j: (0, j)),
                  pl.BlockSpec((1, tn), lambda j: (0, j))],
        out_specs=pl.BlockSpec((r, tn), lambda j: (0, j)),
        out_shape=jax.ShapeDtypeStruct((r, n), F32),
        compiler_params=_params(("arbitrary",), 2 * d * tn * 4 + d * tn * 2 + (8 << 20)),
        name="ada",
    )(c_pad, w_ada, b_ada)


def _in_tile(n):
    return 768 if n > 768 else n


def _in_proj_kernel(cfg, tm, n_blk, *refs):
    ns = len(cfg.streams)
    x_refs = refs[:ns]
    sh_ref, sc_ref, g_ref, w_ref, o_ref, h_ref, xbuf_ref, sem = refs[ns:]
    i = pl.program_id(0)

    def fetch(blk, slot, start):
        for k in range(ns):
            @pl.when(_in_stream(cfg, k, blk, tm))
            def _(k=k):
                row = pl.multiple_of((blk - cfg.tok_offsets[k] // tm) * tm, tm)
                cp = pltpu.make_async_copy(x_refs[k].at[pl.ds(row, tm)], xbuf_ref.at[slot], sem.at[slot])
                if start:
                    cp.start()
                else:
                    cp.wait()

    @pl.when(pl.program_id(1) == 0)
    def _():
        slot = i % 2

        @pl.when(i == 0)
        def _():
            fetch(i, slot, True)

        fetch(i, slot, False)

        @pl.when(i + 1 < n_blk)
        def _():
            fetch(i + 1, 1 - slot, True)

        gain = g_ref[...] * (1.0 + sc_ref[0])
        h_ref[...] = (_rms(xbuf_ref[slot], cfg.eps) * gain + sh_ref[0]).astype(BF16)

    o_ref[...] = jnp.dot(h_ref[...], w_ref[...], preferred_element_type=F32).astype(o_ref.dtype)


def _in_proj(cfg, xs, mod3, g_attn, w_cat):
    d = cfg.d_model
    t = cfg.n_tokens
    n = w_cat.shape[1]
    tm = _tile(min(s for _, s in cfg.streams), 512)
    tn = _in_tile(n)
    in_specs = [pl.BlockSpec(memory_space=pl.ANY) for _ in xs] + [
        pl.BlockSpec((1, 1, d), lambda i, j: (_mod_row(cfg, i * tm), 0, 0)),
        pl.BlockSpec((1, 1, d), lambda i, j: (_mod_row(cfg, i * tm), 0, 1)),
        pl.BlockSpec((1, d), lambda i, j: (0, 0)),
        pl.BlockSpec((d, tn), lambda i, j: (0, j)),
    ]
    vmem = 2 * tm * d * 4 + tm * d * 2 + 2 * d * tn * 2 + 2 * tm * tn * 2 + tm * tn * 4 + 2 * tm * d * 4
    return pl.pallas_call(
        functools.partial(_in_proj_kernel, cfg, tm, t // tm),
        grid=(t // tm, n // tn),
        in_specs=in_specs,
        out_specs=pl.BlockSpec((tm, tn), lambda i, j: (i, j)),
        out_shape=jax.ShapeDtypeStruct((t, n), BF16),
        scratch_shapes=[pltpu.VMEM((tm, d), BF16), pltpu.VMEM((2, tm, d), F32), pltpu.SemaphoreType.DMA((2,))],
        compiler_params=_params(("arbitrary", "arbitrary"), vmem + (6 << 20)),
        name="in_proj",
    )(*xs, mod3, mod3, g_attn, w_cat)


def _rope_half(t):
    lane = lax.broadcasted_iota(I32, t.shape, 1)
    return jnp.where(lane < 64, t + pltpu.roll(t, 64, axis=1), 0.0)


def _q_up_kernel(cfg, c_ref, gl_ref, w_ref, ga_ref, gb_ref, cs_ref, o_ref):
    cn = (_rms(c_ref[...].astype(F32), cfg.eps) * gl_ref[...]).astype(BF16)
    ga = ga_ref[...]
    gbcs = gb_ref[...] * cs_ref[...]
    for h in range(cfg.n_heads):
        y = jnp.dot(cn, w_ref[h], preferred_element_type=F32)
        a = y[:, :LANES]
        b = y[:, LANES:]
        aa = a * a
        ssq2 = jnp.sum((aa + b * b) + aa, axis=-1, keepdims=True)
        s = lax.rsqrt(ssq2 + 2.0 * cfg.qk_dim * cfg.eps)
        t = b * gbcs
        o_ref[h, :, :LANES] = (a * ga * s).astype(o_ref.dtype)
        o_ref[h, :, LANES:] = ((t + pltpu.roll(t, 64, axis=1)) * s).astype(o_ref.dtype)


def _q_up(cfg, z, g_qa, w_q, ga, gb, cs, tm):
    t = cfg.n_tokens
    h = cfg.n_heads
    r = cfg.q_rank
    col = cfg.f_width // r
    vmem = 2 * tm * r * 2 + 2 * h * r * 256 * 2 + 4 * h * tm * 256 * 2 + tm * 128 * 8 + tm * r * 8 + (8 << 20)
    return pl.pallas_call(
        functools.partial(_q_up_kernel, cfg),
        grid=(t // tm,),
        in_specs=[pl.BlockSpec((tm, r), lambda i: (i, col)),
                  pl.BlockSpec((1, r), lambda i: (0, 0)),
                  pl.BlockSpec((h, r, 256), lambda i: (0, 0, 0)),
                  pl.BlockSpec((1, LANES), lambda i: (0, 0)),
                  pl.BlockSpec((1, LANES), lambda i: (0, 0)),
                  pl.BlockSpec((tm, LANES), lambda i: (_pos_block(cfg, i, tm), 0))],
        out_specs=pl.BlockSpec((h, tm, 256), lambda i: (0, i, 0)),
        out_shape=jax.ShapeDtypeStruct((h, t, 256), BF16),
        compiler_params=_params(("arbitrary",), vmem),
        name="q_up",
    )(z, g_qa, w_q, ga, gb, cs)


def _kv_up_kernel(cfg, c_ref, pe_ref, gl_ref, w_ref, ga_ref, gb_ref, cs_ref, k_ref, v_ref):
    cn = (_rms(c_ref[...].astype(F32), cfg.eps) * gl_ref[...]).astype(BF16)
    pe = pe_ref[...].astype(F32)
    lane = lax.broadcasted_iota(I32, pe.shape, 1)
    ssq_pe = jnp.sum(jnp.where(lane < 64, pe * pe, 0.0), axis=-1, keepdims=True)
    kr = _rope_half(pe * gb_ref[...] * cs_ref[...])
    ga = ga_ref[...]
    for h in range(cfg.n_heads):
        y = jnp.dot(cn, w_ref[h], preferred_element_type=F32)
        kn = y[:, :LANES]
        s = lax.rsqrt((jnp.sum(kn * kn, axis=-1, keepdims=True) + ssq_pe) * (1.0 / cfg.qk_dim) + cfg.eps)
        k_ref[h, :, :LANES] = (kn * ga * s).astype(k_ref.dtype)
        k_ref[h, :, LANES:] = (kr * s).astype(k_ref.dtype)
        v_ref[h, :, :LANES] = y[:, LANES:].astype(v_ref.dtype)
        v_ref[h, :, LANES:] = jnp.where(lane == 0, 1.0, 0.0).astype(v_ref.dtype)


def _kv_up(cfg, z, g_kva, w_kv, ga, gb, cs, tm):
    t = cfg.n_tokens
    h = cfg.n_heads
    r = cfg.kv_rank
    col_c = (cfg.f_width + cfg.q_rank) // r
    col_pe = (cfg.f_width + cfg.q_rank + cfg.kv_rank) // LANES
    vmem = 2 * tm * (r + 128) * 2 + 2 * h * r * 256 * 2 + 4 * h * tm * 512 * 2 + tm * r * 8 + (8 << 20)
    return pl.pallas_call(
        functools.partial(_kv_up_kernel, cfg),
        grid=(t // tm,),
        in_specs=[pl.BlockSpec((tm, r), lambda i: (i, col_c)),
                  pl.BlockSpec((tm, LANES), lambda i: (i, col_pe)),
                  pl.BlockSpec((1, r), lambda i: (0, 0)),
                  pl.BlockSpec((h, r, 256), lambda i: (0, 0, 0)),
                  pl.BlockSpec((1, LANES), lambda i: (0, 0)),
                  pl.BlockSpec((1, LANES), lambda i: (0, 0)),
                  pl.BlockSpec((tm, LANES), lambda i: (_pos_block(cfg, i, tm), 0))],
        out_specs=[pl.BlockSpec((h, tm, 256), lambda i: (0, i, 0)),
                   pl.BlockSpec((h, tm, 256), lambda i: (0, i, 0))],
        out_shape=[jax.ShapeDtypeStruct((h, t, 256), BF16),
                   jax.ShapeDtypeStruct((h, t, 256), BF16)],
        compiler_params=_params(("arbitrary",), vmem),
        name="kv_up",
    )(z, z, g_kva, w_kv, ga, gb, cs)


def _chunk_len(cfg):
    chunk = max(s for _, s in cfg.streams)
    for (b, s), off in zip(cfg.streams, cfg.tok_offsets):
        assert chunk % s == 0 and (b * s) % chunk == 0 and off % chunk == 0
    return chunk


def _in_stream_chunks(cfg, k, c, chunk):
    lo = cfg.tok_offsets[k] // chunk
    hi = lo + cfg.streams[k][0] * cfg.streams[k][1] // chunk
    return jnp.logical_and(c >= lo, c < hi)


def _attn_kernel(cfg, chunk, tq, tk, q_ref, k_ref, v_ref, o_ref):
    c = pl.program_id(0)
    qi = pl.program_id(2)
    for k, (_, s) in enumerate(cfg.streams):
        @pl.when(_in_stream_chunks(cfg, k, c, chunk))
        def _(s=s):
            start = ((qi * tq) // s) * s
            for hh in range(ATTN_HEADS_PER_STEP):
                q = q_ref[hh]
                m = jnp.full((tq, 1), -jnp.inf, F32)
                acc = jnp.zeros((tq, 2 * LANES), F32)
                for j in range(s // tk):
                    rows = pl.ds(pl.multiple_of(start + j * tk, tk), tk)
                    sc = lax.dot_general(q, k_ref[hh, rows, :], (((1,), (1,)), ((), ())),
                                         preferred_element_type=F32)
                    m_new = jnp.maximum(m, jnp.max(sc, axis=-1, keepdims=True))
                    p = jnp.exp2((sc - m_new).astype(BF16))
                    acc = jnp.exp2(m - m_new) * acc + jnp.dot(p, v_ref[hh, rows, :], preferred_element_type=F32)
                    m = m_new
                o_ref[:, hh * LANES:(hh + 1) * LANES] = (acc[:, :LANES] / acc[:, LANES:LANES + 1]).astype(o_ref.dtype)


ATTN_HEADS_PER_STEP = 2


def _attn(cfg, q, k, v):
    chunk = _chunk_len(cfg)
    h = cfg.n_heads
    t = cfg.n_tokens
    s_min = min(s for _, s in cfg.streams)
    tq = _tile(s_min, 1024)
    tk = _tile(s_min, 256)
    nq = chunk // tq
    hs = ATTN_HEADS_PER_STEP
    assert h % hs == 0
    vmem = hs * (4 * tq * 256 * 2 + 2 * chunk * 512 * 2 + 6 * tq * 256 * 4) + 8 * tq * tk * 4 + (8 << 20)
    return pl.pallas_call(
        functools.partial(_attn_kernel, cfg, chunk, tq, tk),
        grid=(t // chunk, h // hs, nq),
        in_specs=[pl.BlockSpec((hs, tq, 256), lambda ci, hi, qi: (hi, ci * nq + qi, 0)),
                  pl.BlockSpec((hs, chunk, 256), lambda ci, hi, qi: (hi, ci, 0)),
                  pl.BlockSpec((hs, chunk, 256), lambda ci, hi, qi: (hi, ci, 0))],
        out_specs=pl.BlockSpec((tq, hs * LANES), lambda ci, hi, qi: (ci * nq + qi, hi)),
        out_shape=jax.ShapeDtypeStruct((t, h * LANES), BF16),
        compiler_params=_params(("arbitrary", "arbitrary", "arbitrary"), vmem),
        name="attn",
    )(q, k, v)


def _fold_kernel(cc_ref, sc_ref, w_ref, o_ref):
    w = w_ref[0]
    o_ref[0, :, :256] = jnp.dot(cc_ref[...], w, preferred_element_type=F32,
                                precision=lax.Precision.HIGHEST).astype(o_ref.dtype)
    o_ref[0, :, 256:] = jnp.dot(sc_ref[...], w, preferred_element_type=F32,
                                precision=lax.Precision.HIGHEST).astype(o_ref.dtype)


def _fold_fourier_weights(cfg, cc, sc, w_fmix):
    g, c = cfg.f_groups, cfg.f_gdim
    return pl.pallas_call(
        _fold_kernel,
        grid=(g,),
        in_specs=[pl.BlockSpec((c, c), lambda i: (0, 0)),
                  pl.BlockSpec((c, c), lambda i: (0, 0)),
                  pl.BlockSpec((1, c, c), lambda i: (i, 0, 0))],
        out_specs=pl.BlockSpec((1, c, 2 * c), lambda i: (i, 0, 0)),
        out_shape=jax.ShapeDtypeStruct((g, c, 2 * c), BF16),
        compiler_params=_params(("arbitrary",), 16 << 20),
        name="fold_fourier",
    )(cc, sc, w_fmix)


MID_ROWS = 16


def _half_block(cfg, i, tm):
    hr0 = i * tm
    out = None
    for k, (b, s) in enumerate(cfg.streams):
        local = hr0 - cfg.tok_offsets[k] // 2
        seq = local // (s // 2)
        s0 = local % (s // 2)
        vals = (cfg.tok_offsets[k] + seq * s + s0, s0, cfg.row_offsets[k] + seq, s)
        if out is None:
            out = vals
        else:
            here = hr0 >= cfg.tok_offsets[k] // 2
            out = tuple(jnp.where(here, v, o) for v, o in zip(vals, out))
    return out


def _four1_kernel(cfg, tm, f_ref, r_ref, n_ref, m_ref, j_ref, w_ref, ab_ref, mid_ref):
    _, s0, _, _ = _half_block(cfg, pl.program_id(0), tm)
    c = cfg.f_gdim
    row = lax.broadcasted_iota(I32, (tm, 1), 0)
    for g in range(cfg.f_groups):
        sl = slice(g * c, (g + 1) * c)
        rev = jnp.dot(j_ref[...], r_ref[:, sl], preferred_element_type=F32)
        rev0 = jnp.where(s0 == 0, 0.0, n_ref[0:1, sl].astype(F32))
        rev = jnp.where(row == 0, rev0, rev)
        f = f_ref[:, sl].astype(F32)
        a = jnp.dot((f + rev).astype(BF16), w_ref[g, :, :c], preferred_element_type=F32)
        b = jnp.dot((f - rev).astype(BF16), w_ref[g, :, c:], preferred_element_type=F32)
        ab_ref[0, :, sl] = a.astype(ab_ref.dtype)
        ab_ref[1, :, sl] = b.astype(ab_ref.dtype)

    @pl.when(s0 == 0)
    def _():
        for g in range(cfg.f_groups):
            sl = slice(g * c, (g + 1) * c)
            mid_ref[:, sl] = jnp.dot(m_ref[:, sl], w_ref[g, :, :c], preferred_element_type=F32)


def _four1(cfg, z, w_fold, flip, tm):
    t = cfg.n_tokens
    fw = cfg.f_width
    g, c = cfg.f_groups, cfg.f_gdim

    def cur(i):
        return (_half_block(cfg, i, tm)[0] // tm, 0)

    def partner(i):
        row, s0, _, s = _half_block(cfg, i, tm)
        return ((row - s0 + s - s0) // tm - 1, 0)

    def after_partner(i):
        row, s0, _, s = _half_block(cfg, i, tm)
        return (jnp.minimum((row - s0 + s - s0) // MID_ROWS, t // MID_ROWS - 1), 0)

    def middle(i):
        row, s0, _, s = _half_block(cfg, i, tm)
        return ((row - s0 + s // 2) // MID_ROWS, 0)

    vmem = 4 * tm * fw * 2 + 2 * g * c * 2 * c * 2 + 4 * tm * fw * 2 + 6 * tm * c * 4 + (8 << 20)
    return pl.pallas_call(
        functools.partial(_four1_kernel, cfg, tm),
        grid=(t // 2 // tm,),
        in_specs=[pl.BlockSpec((tm, fw), cur),
                  pl.BlockSpec((tm, fw), partner),
                  pl.BlockSpec((MID_ROWS, fw), after_partner),
                  pl.BlockSpec((MID_ROWS, fw), middle),
                  pl.BlockSpec((tm, tm), lambda i: (0, 0)),
                  pl.BlockSpec((g, c, 2 * c), lambda i: (0, 0, 0))],
        out_specs=[pl.BlockSpec((2, tm, fw), lambda i: (0, i, 0)),
                   pl.BlockSpec((MID_ROWS, fw), lambda i: (_half_block(cfg, i, tm)[2], 0))],
        out_shape=[jax.ShapeDtypeStruct((2, t // 2, fw), BF16),
                   jax.ShapeDtypeStruct((cfg.n_rows * MID_ROWS, fw), F32)],
        compiler_params=_params(("arbitrary",), vmem),
        name="four1",
    )(z, z, z, z, flip, w_fold)


def _four2_kernel(cfg, chunk, tm, *refs):
    ns = len(cfg.streams)
    ab_ref, mid_ref, o_ref = refs[2 * ns:]
    c = pl.program_id(0)
    mi = pl.program_id(2)
    row = lax.broadcasted_iota(I32, (tm, 1), 0)
    for k, (_, s) in enumerate(cfg.streams):
        @pl.when(_in_stream_chunks(cfg, k, c, chunk))
        def _(k=k, s=s):
            half = s // 2
            start = pl.multiple_of(((mi * tm) // s) * half, half)
            y = jnp.dot(refs[2 * k][...], ab_ref[0, pl.ds(start, half), :], preferred_element_type=F32)
            y = y + jnp.dot(refs[2 * k + 1][...], ab_ref[1, pl.ds(start, half), :], preferred_element_type=F32)
            sign = jnp.where(row % 2 == 0, s ** -0.5, -(s ** -0.5))
            o_ref[...] = (y + sign * mid_ref[0:1, :]).astype(o_ref.dtype)


def _four2(cfg, dfts, ab, mid):
    chunk = _chunk_len(cfg)
    t = cfg.n_tokens
    fw = cfg.f_width
    tm = _tile(min(s for _, s in cfg.streams), 512)
    tn = _tile(fw, 1024)
    nm = chunk // tm
    assert tm % 2 == 0

    def d_spec(k):
        s = cfg.streams[k][1]

        def imap(ci, ni, mi):
            inside = _in_stream_chunks(cfg, k, ci, chunk)
            before = ci < cfg.tok_offsets[k] // chunk
            return (jnp.where(inside, ((mi * tm) % s) // tm, jnp.where(before, 0, s // tm - 1)), 0)
        return pl.BlockSpec((tm, s // 2), imap)

    in_specs, args, vmem = [], [], 0
    for k, (dc, ds) in enumerate(dfts):
        in_specs += [d_spec(k), d_spec(k)]
        args += [dc, ds]
        vmem += 4 * tm * (cfg.streams[k][1] // 2) * 2
    in_specs.append(pl.BlockSpec((2, chunk // 2, tn), lambda ci, ni, mi: (0, ci, ni)))
    in_specs.append(pl.BlockSpec((MID_ROWS, tn), lambda ci, ni, mi: (_mod_row(cfg, ci * chunk + mi * tm), ni)))
    vmem += 4 * (chunk // 2) * tn * 2 + 2 * tm * tn * 2 + 3 * tm * tn * 4 + (8 << 20)
    return pl.pallas_call(
        functools.partial(_four2_kernel, cfg, chunk, tm),
        grid=(t // chunk, fw // tn, nm),
        in_specs=in_specs,
        out_specs=pl.BlockSpec((tm, tn), lambda ci, ni, mi: (ci * nm + mi, ni)),
        out_shape=jax.ShapeDtypeStruct((t, fw), BF16),
        compiler_params=_params(("arbitrary", "arbitrary", "arbitrary"), vmem),
        name="four2",
    )(*args, ab, mid)


def _out_proj_kernel(cfg, tm, *refs):
    ns = len(cfg.streams)
    x_refs = refs[:ns]
    gt_ref, a_ref, f_ref, wa_ref, wb_ref, o_ref = refs[ns:]
    i = pl.program_id(0)
    mix = jnp.dot(a_ref[...], wa_ref[...], preferred_element_type=F32)
    mix = mix + jnp.dot(f_ref[...], wb_ref[...], preferred_element_type=F32)
    for k in range(ns):
        @pl.when(_in_stream(cfg, k, i, tm))
        def _(k=k):
            o_ref[...] = x_refs[k][...] + gt_ref[0] * mix


def _out_proj(cfg, xs, mod3, o_attn, o_four, wa, wb):
    d = cfg.d_model
    t = cfg.n_tokens
    aw, fw = cfg.attn_width, cfg.f_width
    tm = _tile(min(s for _, s in cfg.streams), 1024)
    tn = _tile(d, 512)
    nj = d // tn

    def x_spec(k):
        def imap(i, j):
            inside = _in_stream(cfg, k, i, tm)
            before = i < cfg.tok_offsets[k] // tm
            return (_stream_block(cfg, k, i, tm), jnp.where(inside, j, jnp.where(before, 0, nj - 1)))
        return pl.BlockSpec((tm, tn), imap)

    in_specs = [x_spec(k) for k in range(len(xs))] + [
        pl.BlockSpec((1, 1, tn), lambda i, j: (_mod_row(cfg, i * tm), 0, 2 * nj + j)),
        pl.BlockSpec((tm, aw), lambda i, j: (i, 0)),
        pl.BlockSpec((tm, fw), lambda i, j: (i, 0)),
        pl.BlockSpec((aw, tn), lambda i, j: (0, j)),
        pl.BlockSpec((fw, tn), lambda i, j: (0, j)),
    ]
    vmem = (2 * tm * tn * 4 * len(xs) + 2 * tm * (aw + fw) * 2 + 2 * (aw + fw) * tn * 2
            + 2 * tm * tn * 4 + 2 * tm * tn * 4 + (8 << 20))
    return pl.pallas_call(
        functools.partial(_out_proj_kernel, cfg, tm),
        grid=(t // tm, nj),
        in_specs=in_specs,
        out_specs=pl.BlockSpec((tm, tn), lambda i, j: (i, j)),
        out_shape=jax.ShapeDtypeStruct((t, d), F32),
        compiler_params=_params(("arbitrary", "arbitrary"), vmem),
        name="out_proj",
    )(*xs, mod3, o_attn, o_four, wa, wb)


def _router_kernel(cfg, tm, x_ref, sh_ref, sc_ref, g_ref, wr_ref, tri_ref,
                   hp_ref, mi_ref, gt_ref, cnt_ref, carry_ref):
    i = pl.program_id(0)
    ng, ne = cfg.n_groups, cfg.e_per_group
    n_exp = cfg.n_experts

    @pl.when(i == 0)
    def _():
        carry_ref[...] = jnp.zeros_like(carry_ref)

    h2 = _rms(x_ref[...], cfg.eps) * g_ref[...] * (1.0 + sc_ref[0]) + sh_ref[0]
    for c in range(cfg.d_model // PACK_CHUNK):
        lo = h2[:, c * PACK_CHUNK:c * PACK_CHUNK + LANES]
        hi = h2[:, c * PACK_CHUNK + LANES:(c + 1) * PACK_CHUNK]
        hp_ref[:, c * LANES:(c + 1) * LANES] = _pack_words(lo, hi)

    h_hi = h2.astype(BF16)
    h_lo = (h2 - h_hi.astype(F32)).astype(BF16)
    p1 = jnp.dot(h_hi, wr_ref[...], preferred_element_type=F32)
    p2 = jnp.dot(h_lo, wr_ref[:, :LANES], preferred_element_type=F32)
    logits = (p1[:, :LANES] + p1[:, LANES:] + p2).T

    gl = logits[0:ng]
    io = lax.broadcasted_iota(I32, (ng, tm), 0)
    gm = jnp.max(gl, axis=0, keepdims=True)
    p_group = 1.0 / jnp.sum(jnp.exp(gl - gm), axis=0, keepdims=True)
    gidx = jnp.min(jnp.where(gl == gm, io, ng), axis=0, keepdims=True)
    sel = jnp.zeros((ne, tm), F32)
    for g in range(ng):
        sel = jnp.where(gidx == g, logits[ng + g * ne:ng + (g + 1) * ne], sel)
    ie = lax.broadcasted_iota(I32, (ne, tm), 0)
    m1 = jnp.max(sel, axis=0, keepdims=True)
    i1 = jnp.min(jnp.where(sel == m1, ie, ne), axis=0, keepdims=True)
    sel2 = jnp.where(ie == i1, -jnp.inf, sel)
    m2 = jnp.max(sel2, axis=0, keepdims=True)
    i2 = jnp.min(jnp.where(sel2 == m2, ie, ne), axis=0, keepdims=True)
    e21 = jnp.exp(m2 - m1)
    gate0 = p_group / (1.0 + e21)
    gate1 = p_group * e21 / (1.0 + e21)
    e0 = gidx * ne + i1
    e1 = gidx * ne + i2

    ix = lax.broadcasted_iota(I32, (n_exp, tm), 0)
    hit0 = ix == e0
    hit1 = ix == e1
    member = jnp.logical_or(hit0, hit1).astype(F32)
    before = jnp.dot(member.astype(BF16), tri_ref[...], preferred_element_type=F32)
    total = before + carry_ref[:, 0:1]
    r0 = jnp.sum(jnp.where(hit0, total, 0.0), axis=0, keepdims=True).astype(I32)
    r1 = jnp.sum(jnp.where(hit1, total, 0.0), axis=0, keepdims=True).astype(I32)
    carry_ref[...] = carry_ref[...] + jnp.sum(member, axis=1, keepdims=True)
    cnt_ref[...] = carry_ref[...].astype(I32)

    i8 = lax.broadcasted_iota(I32, (8, tm), 0)
    mi_ref[...] = jnp.where(i8 == 0, e0, jnp.where(i8 == 1, e1, jnp.where(i8 == 2, r0, jnp.where(i8 == 3, r1, 0))))
    il = lax.broadcasted_iota(I32, (LANES, tm), 0)
    gt_ref[...] = jnp.where(il == 0, gate0, jnp.where(il == 1, gate1, 0.0)).T


def _router(cfg, x1, mod3, g_ffn, wr_cat, tri, tm):
    d = cfg.d_model
    t = cfg.n_tokens
    ne = cfg.n_experts
    vmem = 2 * tm * d * 4 + 2 * tm * d * 2 + 2 * d * 256 * 2 + 6 * tm * d * 4 + 2 * tm * tm * 2 + (8 << 20)
    return pl.pallas_call(
        functools.partial(_router_kernel, cfg, tm),
        grid=(t // tm,),
        in_specs=[pl.BlockSpec((tm, d), lambda i: (i, 0)),
                  pl.BlockSpec((1, 1, d), lambda i: (_mod_row(cfg, i * tm), 0, 3)),
                  pl.BlockSpec((1, 1, d), lambda i: (_mod_row(cfg, i * tm), 0, 4)),
                  pl.BlockSpec((1, d), lambda i: (0, 0)),
                  pl.BlockSpec((d, 256), lambda i: (0, 0)),
                  pl.BlockSpec((tm, tm), lambda i: (0, 0))],
        out_specs=[pl.BlockSpec((tm, d // 2), lambda i: (i, 0)),
                   pl.BlockSpec((8, tm), lambda i: (0, i)),
                   pl.BlockSpec((tm, LANES), lambda i: (i, 0)),
                   pl.BlockSpec((ne, LANES), lambda i: (0, 0))],
        out_shape=[jax.ShapeDtypeStruct((t, d // 2), U32),
                   jax.ShapeDtypeStruct((8, t), I32),
                   jax.ShapeDtypeStruct((t, LANES), F32),
                   jax.ShapeDtypeStruct((ne, LANES), I32)],
        scratch_shapes=[pltpu.VMEM((ne, LANES), F32)],
        compiler_params=_params(("arbitrary",), vmem),
        name="router",
    )(x1, mod3, mod3, g_ffn, wr_cat, tri)


def _dest_row(start_ref, meta_ref, kk, r):
    return start_ref[meta_ref[kk, r]] + meta_ref[2 + kk, r]


def _dispatch_kernel(tm, start_ref, meta_ref, h_ref, xs_in_ref, xs_ref, sem):
    del xs_in_ref

    def issue(r, carry):
        for kk in range(2):
            dst = xs_ref.at[pl.ds(_dest_row(start_ref, meta_ref, kk, r), 1)]
            pltpu.make_async_copy(h_ref.at[pl.ds(r, 1)], dst, sem).start(priority=kk)
        return carry

    lax.fori_loop(0, tm, issue, 0, unroll=8)
    for kk in range(2):
        pltpu.make_async_copy(h_ref, xs_ref.at[pl.ds(0, tm)], sem).wait()


def _dispatch(cfg, pad_start, meta, h2p, xs_init, tm):
    t = cfg.n_tokens
    w = h2p.shape[1]
    grid_spec = pltpu.PrefetchScalarGridSpec(
        num_scalar_prefetch=1,
        grid=(t // tm,),
        in_specs=[pl.BlockSpec((8, tm), lambda i, st: (0, i), memory_space=pltpu.SMEM),
                  pl.BlockSpec((tm, w), lambda i, st: (i, 0)),
                  pl.BlockSpec(memory_space=pl.ANY)],
        out_specs=pl.BlockSpec(memory_space=pl.ANY),
        scratch_shapes=[pltpu.SemaphoreType.DMA(())],
    )
    return pl.pallas_call(
        functools.partial(_dispatch_kernel, tm),
        grid_spec=grid_spec,
        out_shape=jax.ShapeDtypeStruct(xs_init.shape, xs_init.dtype),
        input_output_aliases={3: 0},
        compiler_params=_params(("arbitrary",), 4 * tm * w * 4 + (4 << 20)),
        name="dispatch",
    )(pad_start, meta, h2p, xs_init)


def _expert_schedule(cfg, cnt, n_blocks):
    bm = cfg.moe_block
    padded = (cnt + bm - 1) // bm * bm
    pad_end = jnp.cumsum(padded)
    pad_start = (pad_end - padded).astype(I32)
    blk = jnp.arange(n_blocks, dtype=I32)
    be = jnp.minimum(jnp.sum((pad_end[None, :] <= (blk * bm)[:, None]).astype(I32), axis=1), cfg.n_experts - 1)
    nv = (pad_end[-1] // bm).astype(I32)
    prev = jnp.concatenate([jnp.full((1,), -1, I32), be[:-1]])
    first = jnp.logical_and(blk < nv, be != prev)
    suffix = lax.cummin(jnp.where(first, blk, n_blocks), axis=0, reverse=True)
    next_first = jnp.concatenate([suffix[1:], jnp.full((1,), n_blocks, I32)])
    last = next_first >= n_blocks
    nxt = be[jnp.where(last, 0, next_first)]
    run = jnp.cumsum(first.astype(I32)) - 1
    n_runs = jnp.sum(first.astype(I32))
    half = (cnt[be] - (blk * bm - pad_start[be])) <= bm // 2
    tables = (be.astype(I32), first.astype(I32), nxt.astype(I32), last.astype(I32), run.astype(I32),
              half.astype(I32))
    return pad_start, tables + (n_runs[None], nv[None])


N_SCHED = 8


def _stage_expert_weights(sched, i, tile, n_tiles, copies):
    be_ref, first_ref, nxt_ref, last_ref, run_ref, _, n_runs_ref, nv_ref = sched
    slot = (tile * n_runs_ref[0] + run_ref[i]) % 2

    @pl.when(jnp.logical_and(i < nv_ref[0], first_ref[i] == 1))
    def _():
        @pl.when(jnp.logical_and(tile == 0, i == 0))
        def _():
            for cp in copies:
                cp(be_ref[i], tile, slot).start()

        for cp in copies:
            cp(be_ref[i], tile, slot).wait()
        is_last = last_ref[i] == 1

        @pl.when(jnp.logical_not(is_last))
        def _():
            for cp in copies:
                cp(nxt_ref[i], tile, 1 - slot).start()

        @pl.when(jnp.logical_and(is_last, tile + 1 < n_tiles))
        def _():
            for cp in copies:
                cp(nxt_ref[i], tile + 1, 1 - slot).start()

    return slot


def _moe1_kernel(cfg, tf, n_tiles, *refs):
    sched = refs[:N_SCHED]
    x_ref, w1_ref, w3_ref, o_ref, xb_ref, s1_ref, s3_ref, sem = refs[N_SCHED:]
    nv_ref = sched[-1]
    j = pl.program_id(0)
    i = pl.program_id(1)

    def copy(w_ref, stage_ref, k):
        def make(e, jj, slot):
            src = w_ref.at[e, :, pl.ds(pl.multiple_of(jj * tf, tf), tf)]
            return pltpu.make_async_copy(src, stage_ref.at[slot], sem.at[k, slot])
        return make

    slot = _stage_expert_weights(sched, i, j, n_tiles, [copy(w1_ref, s1_ref, 0), copy(w3_ref, s3_ref, 1)])
    bm = x_ref.shape[0]
    valid = i < nv_ref[0]
    half = sched[5][i] == 1

    def compute(m):
        for c in range(cfg.d_model // PACK_CHUNK):
            lo, hi = _unpack_words(x_ref[:m, c * LANES:(c + 1) * LANES])
            xb_ref[:m, c * PACK_CHUNK:c * PACK_CHUNK + LANES] = lo.astype(BF16)
            xb_ref[:m, c * PACK_CHUNK + LANES:(c + 1) * PACK_CHUNK] = hi.astype(BF16)
        x = xb_ref[:m, :]
        a = jnp.dot(x, s1_ref[slot].astype(BF16), preferred_element_type=F32)
        b = jnp.dot(x, s3_ref[slot].astype(BF16), preferred_element_type=F32)
        o_ref[:m, :] = (a * jax.nn.sigmoid(a) * b).astype(o_ref.dtype)
        if m < bm:
            o_ref[m:, :] = jnp.zeros((bm - m, o_ref.shape[1]), o_ref.dtype)

    @pl.when(jnp.logical_and(valid, jnp.logical_not(half)))
    def _():
        compute(bm)

    @pl.when(jnp.logical_and(valid, half))
    def _():
        compute(bm // 2)

    @pl.when(i >= nv_ref[0])
    def _():
        o_ref[...] = jnp.zeros_like(o_ref)


def _moe1(cfg, sched, xs, w1, w3):
    d, f = cfg.d_model, cfg.d_expert
    bm = cfg.moe_block
    rows = xs.shape[0]
    tf = _tile(f, cfg.moe_up_tile)
    n_tiles = f // tf

    def x_map(j, i, *sched):
        return (jnp.minimum(i, sched[-1][0] - 1), 0)

    grid_spec = pltpu.PrefetchScalarGridSpec(
        num_scalar_prefetch=N_SCHED,
        grid=(n_tiles, rows // bm),
        in_specs=[pl.BlockSpec((bm, d // 2), x_map),
                  pl.BlockSpec(memory_space=pl.ANY),
                  pl.BlockSpec(memory_space=pl.ANY)],
        out_specs=pl.BlockSpec((bm, tf), lambda j, i, *_: (i, j)),
        scratch_shapes=[pltpu.VMEM((bm, d), BF16),
                        pltpu.VMEM((2, d, tf), F32), pltpu.VMEM((2, d, tf), F32),
                        pltpu.SemaphoreType.DMA((2, 2))],
    )
    vmem = 2 * bm * d * 2 + bm * d * 2 + 4 * d * tf * 4 + 2 * d * tf * 2 + 2 * bm * tf * 2 + 4 * bm * tf * 4 + (4 << 20)
    return pl.pallas_call(
        functools.partial(_moe1_kernel, cfg, tf, n_tiles),
        grid_spec=grid_spec,
        out_shape=jax.ShapeDtypeStruct((rows, f), BF16),
        compiler_params=_params(("arbitrary", "arbitrary"), vmem),
        name="moe1",
    )(*sched, xs, w1, w3)


def _moe2_kernel(cfg, tn, n_tiles, *refs):
    sched = refs[:N_SCHED]
    h_ref, w2_ref, o_ref, s2_ref, sem = refs[N_SCHED:]
    nv_ref = sched[-1]
    n = pl.program_id(0)
    i = pl.program_id(1)

    def copy(e, nn, slot):
        src = w2_ref.at[e, :, pl.ds(pl.multiple_of(nn * tn, tn), tn)]
        return pltpu.make_async_copy(src, s2_ref.at[slot], sem.at[slot])

    slot = _stage_expert_weights(sched, i, n, n_tiles, [copy])
    bm = h_ref.shape[0]
    valid = i < nv_ref[0]
    half = sched[5][i] == 1

    def compute(m):
        y = jnp.dot(h_ref[:m, :], s2_ref[slot].astype(BF16), preferred_element_type=F32)
        for c in range(tn // PACK_CHUNK):
            o_ref[:m, c * LANES:(c + 1) * LANES] = _pack_words(
                y[:, c * PACK_CHUNK:c * PACK_CHUNK + LANES], y[:, c * PACK_CHUNK + LANES:(c + 1) * PACK_CHUNK])
        if m < bm:
            o_ref[m:, :] = jnp.zeros((bm - m, o_ref.shape[1]), o_ref.dtype)

    @pl.when(jnp.logical_and(valid, jnp.logical_not(half)))
    def _():
        compute(bm)

    @pl.when(jnp.logical_and(valid, half))
    def _():
        compute(bm // 2)

    @pl.when(i >= nv_ref[0])
    def _():
        o_ref[...] = jnp.zeros_like(o_ref)


def _moe2(cfg, sched, hmid, w2):
    d, f = cfg.d_model, cfg.d_expert
    bm = cfg.moe_block
    rows = hmid.shape[0]
    tn = _tile(d, cfg.moe_down_tile)
    assert tn % PACK_CHUNK == 0
    n_tiles = d // tn

    def h_map(n, i, *sched):
        return (jnp.minimum(i, sched[-1][0] - 1), 0)

    grid_spec = pltpu.PrefetchScalarGridSpec(
        num_scalar_prefetch=N_SCHED,
        grid=(n_tiles, rows // bm),
        in_specs=[pl.BlockSpec((bm, f), h_map),
                  pl.BlockSpec(memory_space=pl.ANY)],
        out_specs=pl.BlockSpec((bm, tn // 2), lambda n, i, *_: (i, n)),
        scratch_shapes=[pltpu.VMEM((2, f, tn), F32), pltpu.SemaphoreType.DMA((2,))],
    )
    vmem = 2 * bm * f * 2 + 2 * f * tn * 4 + f * tn * 2 + 2 * bm * tn * 2 + 2 * bm * tn * 4 + (8 << 20)
    return pl.pallas_call(
        functools.partial(_moe2_kernel, cfg, tn, n_tiles),
        grid_spec=grid_spec,
        out_shape=jax.ShapeDtypeStruct((rows, d // 2), U32),
        compiler_params=_params(("arbitrary", "arbitrary"), vmem),
        name="moe2",
    )(*sched, hmid, w2)


def _combine_kernel(cfg, tm, n_blk, start_ref, meta_ref, meta_nxt_ref, x_ref, gt2_ref, gate_ref, y_ref, o_ref,
                    buf_ref, sem):
    i = pl.program_id(0)
    slot = i % 2

    n_chunks = cfg.d_model // PACK_CHUNK
    rows_per_chunk = tm // n_chunks

    def start_row(m_ref, r, sl):
        for kk in range(2):
            src = y_ref.at[pl.ds(_dest_row(start_ref, m_ref, kk, r), 1)]
            pltpu.make_async_copy(src, buf_ref.at[sl, kk, pl.ds(r, 1)], sem.at[sl]).start(priority=kk)

    def wait_slot(sl):
        for kk in range(2):
            pltpu.make_async_copy(y_ref.at[pl.ds(0, tm)], buf_ref.at[sl, kk], sem.at[sl]).wait()

    @pl.when(i == 0)
    def _():
        def body(r, carry):
            start_row(meta_ref, r, slot)
            return carry
        lax.fori_loop(0, tm, body, 0, unroll=8)

    wait_slot(slot)

    g0 = gate_ref[:, 0:1]
    g1 = gate_ref[:, 1:2]
    for c in range(n_chunks):
        lo0, hi0 = _unpack_words(buf_ref[slot, 0, :, c * LANES:(c + 1) * LANES])
        lo1, hi1 = _unpack_words(buf_ref[slot, 1, :, c * LANES:(c + 1) * LANES])
        sl_lo = slice(c * PACK_CHUNK, c * PACK_CHUNK + LANES)
        sl_hi = slice(c * PACK_CHUNK + LANES, (c + 1) * PACK_CHUNK)
        o_ref[:, sl_lo] = x_ref[:, sl_lo] + gt2_ref[0, :, sl_lo] * (g0 * lo0 + g1 * lo1)
        o_ref[:, sl_hi] = x_ref[:, sl_hi] + gt2_ref[0, :, sl_hi] * (g0 * hi0 + g1 * hi1)
        for r in range(c * rows_per_chunk, (c + 1) * rows_per_chunk):
            start_row(meta_nxt_ref, r, 1 - slot)

    @pl.when(i == n_blk - 1)
    def _():
        wait_slot(1 - slot)


def _combine(cfg, k_idx, pad_start, meta, x1, mod3, gates, y, tm):
    b, s = cfg.streams[k_idx]
    off = cfg.tok_offsets[k_idx]
    d = cfg.d_model
    n = b * s
    ob = off // tm
    row0 = cfg.row_offsets[k_idx]
    n_blk = n // tm
    grid_spec = pltpu.PrefetchScalarGridSpec(
        num_scalar_prefetch=1,
        grid=(n_blk,),
        in_specs=[pl.BlockSpec((8, tm), lambda i, st: (0, ob + i), memory_space=pltpu.SMEM),
                  pl.BlockSpec((8, tm), lambda i, st: (0, ob + jnp.minimum(i + 1, n_blk - 1)),
                               memory_space=pltpu.SMEM),
                  pl.BlockSpec((tm, d), lambda i, st: (ob + i, 0)),
                  pl.BlockSpec((1, 1, d), lambda i, st: (row0 + (i * tm) // s, 0, 5)),
                  pl.BlockSpec((tm, LANES), lambda i, st: (ob + i, 0)),
                  pl.BlockSpec(memory_space=pl.ANY)],
        out_specs=pl.BlockSpec((tm, d), lambda i, st: (i, 0)),
        scratch_shapes=[pltpu.VMEM((2, 2, tm, d // 2), U32), pltpu.SemaphoreType.DMA((2,))],
    )
    vmem = 4 * tm * d * 4 + 4 * tm * d * 2 + 2 * tm * 128 * 4 + 6 * tm * 128 * 4 + (8 << 20)
    return pl.pallas_call(
        functools.partial(_combine_kernel, cfg, tm, n_blk),
        grid_spec=grid_spec,
        out_shape=jax.ShapeDtypeStruct((n, d), F32),
        compiler_params=_params(("arbitrary",), vmem),
        name=f"combine{k_idx}",
    )(pad_start, meta, meta, x1, mod3, gates, y)


def _rot_half(a, axis=-1):
    h = a.shape[axis] // 2
    lo = lax.slice_in_dim(a, 0, h, axis=axis)
    hi = lax.slice_in_dim(a, h, 2 * h, axis=axis)
    return jnp.concatenate([hi, lo], axis=axis)


def _rope_table(cfg, n_pos):
    half = cfg.rope // 2
    inv_freq = cfg.theta ** (-2.0 * jnp.arange(half, dtype=F32) / cfg.rope)
    ang = jnp.arange(n_pos, dtype=F32)[:, None] * inv_freq[None, :]
    cos, sin = jnp.cos(ang), jnp.sin(ang)
    return jnp.concatenate([cos, cos, -sin, sin], axis=-1)


def _dft(n, scale):
    idx = jnp.arange(n, dtype=I32)
    ang = ((idx[:, None] * idx[None, :]) % n).astype(F32) * (2.0 * math.pi / n)
    return jnp.cos(ang) * scale, jnp.sin(ang) * scale


def _dft_half(n, split=64):
    s = jnp.arange(n // 2, dtype=I32)

    def table(rows):
        ang = ((rows[:, None] * s[None, :]) % n).astype(F32) * (2.0 * math.pi / n)
        return jnp.cos(ang), jnp.sin(ang)

    ch, sh = table(jnp.arange(n // split, dtype=I32) * split)
    cl, sl = table(jnp.arange(split, dtype=I32))
    scale = n ** -0.5
    cos = (ch[:, None, :] * cl[None] - sh[:, None, :] * sl[None]) * scale
    msin = (sh[:, None, :] * cl[None] + ch[:, None, :] * sl[None]) * -scale
    return cos.reshape(n, n // 2).astype(BF16), msin.reshape(n, n // 2).astype(BF16)


def _layer(cfg, xs, cs_, w_ada, b_ada, g_attn, w_in, g_qa, w_uq, g_kva, w_ukv, g_qn, g_kn,
           w_fmix, w_o, g_ffn, w_group, w_route, w1, w3, w2):
    d, h = cfg.d_model, cfg.n_heads
    t = cfg.n_tokens
    nope, rope = cfg.nope, cfg.rope
    assert nope == LANES and cfg.v_dim == LANES and rope == 64 and cfg.f_gdim == PACK_CHUNK
    assert cfg.f_width % cfg.q_rank == 0 and (cfg.f_width + cfg.q_rank) % cfg.kv_rank == 0
    qr, kr = cfg.q_rank, cfg.kv_rank

    w16 = w_in.astype(BF16)
    half = rope // 2
    c0 = qr + kr
    n_cat = w16.shape[1] + rope
    w_cat = jnp.concatenate(
        [w16[:, c0 + rope:], w16[:, :c0 + rope], w16[:, c0 + half:c0 + rope], w16[:, c0:c0 + half],
         jnp.zeros((d, -n_cat % _in_tile(n_cat)), BF16)], axis=1)

    wq3 = w_uq.reshape(qr, h, nope + rope)
    w_q = jnp.concatenate([wq3, _rot_half(wq3[..., nope:])], axis=-1).transpose(1, 0, 2).astype(BF16)
    w_kv = w_ukv.reshape(kr, h, nope + cfg.v_dim).transpose(1, 0, 2).astype(BF16)
    q_const = math.sqrt(2.0 * cfg.qk_dim) * cfg.qk_dim ** -0.5 * math.log2(math.e)
    ga_q = g_qn[None, :nope] * q_const
    gb_q = jnp.concatenate([g_qn[nope:], _rot_half(g_qn[nope:])])[None] * q_const
    ga_k, gb_k = g_kn[None, :nope], jnp.concatenate([g_kn[nope:], _rot_half(g_kn[nope:])])[None]
    wa, wb = w_o[:cfg.attn_width].astype(BF16), w_o[cfg.attn_width:].astype(BF16)
    wr = jnp.concatenate([w_group, w_route.reshape(d, cfg.n_experts)], axis=1)
    wr = jnp.pad(wr, ((0, 0), (0, LANES - wr.shape[1])))
    wr_hi = wr.astype(BF16)
    wr_cat = jnp.concatenate([wr_hi, (wr - wr_hi.astype(F32)).astype(BF16)], axis=1)

    s_max = max(s for _, s in cfg.streams)
    cs_tab = _rope_table(cfg, s_max)
    cc, sc = _dft(cfg.f_gdim, cfg.f_gdim ** -0.5)

    c_all = jnp.concatenate(cs_, axis=0)
    rows = -(-c_all.shape[0] // 8) * 8
    c_pad = jnp.pad(c_all, ((0, rows - c_all.shape[0]), (0, 0)))
    mod3 = _ada(c_pad, w_ada, b_ada[None]).reshape(rows, 1, 6 * d)

    x2d = [x.reshape(-1, d) for x in xs]
    z = _in_proj(cfg, x2d, mod3, g_attn[None], w_cat)
    tm = _tile(min(s for _, s in cfg.streams), 512)
    q = _q_up(cfg, z, g_qa[None], w_q, ga_q, gb_q, cs_tab, tm)
    k, v = _kv_up(cfg, z, g_kva[None], w_kv, ga_k, gb_k, cs_tab, tm)
    o_attn = _attn(cfg, q, k, v)

    w_fold = _fold_fourier_weights(cfg, cc, sc, w_fmix)
    tm_f = _tile(min(s for _, s in cfg.streams) // 2, 512)
    idx = jnp.arange(tm_f)
    flip = (idx[:, None] + idx[None, :] == tm_f).astype(BF16)
    ab, mid = _four1(cfg, z, w_fold, flip, tm_f)
    o_four = _four2(cfg, [_dft_half(s) for _, s in cfg.streams], ab, mid)

    x1 = _out_proj(cfg, x2d, mod3, o_attn, o_four, wa, wb)

    tm_r = _tile(tm, 256)
    tri = (jnp.arange(tm_r)[:, None] < jnp.arange(tm_r)[None, :]).astype(BF16)
    h2p, meta, gates, counts = _router(cfg, x1, mod3, g_ffn[None], wr_cat, tri, tm_r)

    bm = cfg.moe_block
    n_blocks = (2 * t) // bm + cfg.n_experts
    pad_start, sched = _expert_schedule(cfg, counts[:, 0], n_blocks)

    xs_rows = _dispatch(cfg, pad_start, meta, h2p, jnp.zeros((n_blocks * bm, d // 2), U32), _tile(tm, 256))
    hmid = _moe1(cfg, sched, xs_rows, w1, w3)
    y = _moe2(cfg, sched, hmid, w2)

    outs = []
    for ki, (b, s) in enumerate(cfg.streams):
        o = _combine(cfg, ki, pad_start, meta, x1, mod3, gates, y, _tile(tm, 256))
        outs.append(o.reshape(b, s, d))
    return tuple(outs)


def kernel(x_prompt, x_sample, c_prompt, c_sample, w_ada, b_ada, g_attn, w_in, g_qa, w_uq, g_kva, w_ukv,
           g_qn, g_kn, w_fmix, w_o, g_ffn, w_group, w_route, w1, w3, w2):
    cfg = Cfg()
    xs = (x_prompt, x_sample)
    cs_ = (c_prompt, c_sample)
    for l in range(w_ada.shape[0]):
        xs = _layer(cfg, xs, cs_, w_ada[l], b_ada[l], g_attn[l], w_in[l], g_qa[l], w_uq[l], g_kva[l],
                    w_ukv[l], g_qn[l], g_kn[l], w_fmix[l], w_o[l], g_ffn[l], w_group[l], w_route[l],
                    w1[l], w3[l], w2[l])
    return xs
```

```python
import dataclasses
import functools
import math

import jax
import jax.numpy as jnp
from jax import lax
from jax.experimental import pallas as pl
from jax.experimental.pallas import tpu as pltpu

BF16 = jnp.bfloat16
F32 = jnp.float32
U32 = jnp.uint32
I32 = jnp.int32

LANES = 128
MXU_DIM = 256
VMEM_CAP = 60 << 20
PACK_CHUNK = 2 * LANES


@dataclasses.dataclass(frozen=True)
class Cfg:
    d_model: int = 4096
    streams: tuple = ((8, 2048), (4, 4096))
    n_heads: int = 16
    nope: int = 128
    rope: int = 64
    v_dim: int = 128
    q_rank: int = 1024
    kv_rank: int = 512
    f_gdim: int = 256
    n_groups: int = 8
    e_per_group: int = 8
    d_expert: int = 1024
    theta: float = 10000.0
    eps: float = 1e-6
    moe_block: int = 512
    moe_up_tile: int = 512
    moe_down_tile: int = 2048

    @property
    def attn_width(self):
        return self.n_heads * self.v_dim

    @property
    def f_width(self):
        return self.d_model - self.attn_width

    @property
    def f_groups(self):
        return self.f_width // self.f_gdim

    @property
    def qk_dim(self):
        return self.nope + self.rope

    @property
    def n_experts(self):
        return self.n_groups * self.e_per_group

    @property
    def tok_offsets(self):
        offs, t = [], 0
        for b, s in self.streams:
            offs.append(t)
            t += b * s
        return tuple(offs)

    @property
    def row_offsets(self):
        offs, r = [], 0
        for b, _ in self.streams:
            offs.append(r)
            r += b
        return tuple(offs)

    @property
    def n_tokens(self):
        return sum(b * s for b, s in self.streams)

    @property
    def n_rows(self):
        return sum(b for b, _ in self.streams)


def _tile(n, want):
    if n <= want:
        return n
    t = want
    while n % t:
        t -= 8
    return t


def _params(sem, vmem_bytes, flags=None):
    return pltpu.CompilerParams(dimension_semantics=sem,
                                vmem_limit_bytes=int(min(VMEM_CAP, vmem_bytes)), flags=flags)


def _mod_row(cfg, t0):
    row = None
    for k, (b, s) in enumerate(cfg.streams):
        expr = cfg.row_offsets[k] + (t0 - cfg.tok_offsets[k]) // s
        row = expr if row is None else jnp.where(t0 >= cfg.tok_offsets[k], expr, row)
    return row


def _pos_block(cfg, i, tm):
    t0 = i * tm
    blk = None
    for k, (b, s) in enumerate(cfg.streams):
        expr = ((t0 - cfg.tok_offsets[k]) % s) // tm
        blk = expr if blk is None else jnp.where(t0 >= cfg.tok_offsets[k], expr, blk)
    return blk


def _stream_block(cfg, k, i, tm):
    nb = cfg.streams[k][0] * cfg.streams[k][1] // tm
    return jnp.clip(i - cfg.tok_offsets[k] // tm, 0, nb - 1)


def _in_stream(cfg, k, i, tm):
    lo = cfg.tok_offsets[k] // tm
    hi = lo + cfg.streams[k][0] * cfg.streams[k][1] // tm
    return jnp.logical_and(i >= lo, i < hi)


def _rms(x, eps):
    return x * lax.rsqrt(jnp.mean(x * x, axis=-1, keepdims=True) + eps)


def _pack_words(a, b):
    wa = lax.bitcast_convert_type(a.astype(BF16).astype(F32), U32) >> 16
    wb = lax.bitcast_convert_type(b.astype(BF16).astype(F32), U32)
    return wa | wb


def _unpack_words(w):
    lo = lax.bitcast_convert_type(w << 16, F32)
    hi = lax.bitcast_convert_type(w & jnp.uint32(0xFFFF0000), F32)
    return lo, hi


def _ada_kernel(c_ref, w_ref, b_ref, o_ref):
    c = c_ref[...]
    s = (c * jax.nn.sigmoid(c)).astype(BF16)
    o_ref[...] = jnp.dot(s, w_ref[...].astype(BF16), preferred_element_type=F32) + b_ref[...]


def _ada(c_pad, w_ada, b_ada):
    r, d = c_pad.shape
    n = w_ada.shape[1]
    tn = _tile(n, 512)
    return pl.pallas_call(
        _ada_kernel,
        grid=(n // tn,),
        in_specs=[pl.BlockSpec((r, d), lambda j: (0, 0)),
                  pl.BlockSpec((d, tn), lambda j: (0, j)),
                  pl.BlockSpec((1, tn), lambda j: (0, j))],
        out_specs=pl.BlockSpec((r, tn), lambda j: (0, j)),
        out_shape=jax.ShapeDtypeStruct((r, n), F32),
        compiler_params=_params(("arbitrary",), 2 * d * tn * 4 + d * tn * 2 + (8 << 20)),
        name="ada",
    )(c_pad, w_ada, b_ada)


def _in_tile(n):
    return 768 if n > 768 else n


def _in_proj_kernel(cfg, tm, n_blk, *refs):
    ns = len(cfg.streams)
    x_refs = refs[:ns]
    sh_ref, sc_ref, g_ref, w_ref, o_ref, h_ref, xbuf_ref, sem = refs[ns:]
    i = pl.program_id(0)

    def fetch(blk, slot, start):
        for k in range(ns):
            @pl.when(_in_stream(cfg, k, blk, tm))
            def _(k=k):
                row = pl.multiple_of((blk - cfg.tok_offsets[k] // tm) * tm, tm)
                cp = pltpu.make_async_copy(x_refs[k].at[pl.ds(row, tm)], xbuf_ref.at[slot], sem.at[slot])
                if start:
                    cp.start()
                else:
                    cp.wait()

    @pl.when(pl.program_id(1) == 0)
    def _():
        slot = i % 2

        @pl.when(i == 0)
        def _():
            fetch(i, slot, True)

        fetch(i, slot, False)

        @pl.when(i + 1 < n_blk)
        def _():
            fetch(i + 1, 1 - slot, True)

        gain = g_ref[...] * (1.0 + sc_ref[0])
        h_ref[...] = (_rms(xbuf_ref[slot], cfg.eps) * gain + sh_ref[0]).astype(BF16)

    o_ref[...] = jnp.dot(h_ref[...], w_ref[...], preferred_element_type=F32).astype(o_ref.dtype)


def _in_proj(cfg, xs, mod3, g_attn, w_cat):
    d = cfg.d_model
    t = cfg.n_tokens
    n = w_cat.shape[1]
    tm = _tile(min(s for _, s in cfg.streams), 512)
    tn = _in_tile(n)
    in_specs = [pl.BlockSpec(memory_space=pl.ANY) for _ in xs] + [
        pl.BlockSpec((1, 1, d), lambda i, j: (_mod_row(cfg, i * tm), 0, 0)),
        pl.BlockSpec((1, 1, d), lambda i, j: (_mod_row(cfg, i * tm), 0, 1)),
        pl.BlockSpec((1, d), lambda i, j: (0, 0)),
        pl.BlockSpec((d, tn), lambda i, j: (0, j)),
    ]
    vmem = 2 * tm * d * 4 + tm * d * 2 + 2 * d * tn * 2 + 2 * tm * tn * 2 + tm * tn * 4 + 2 * tm * d * 4
    return pl.pallas_call(
        functools.partial(_in_proj_kernel, cfg, tm, t // tm),
        grid=(t // tm, n // tn),
        in_specs=in_specs,
        out_specs=pl.BlockSpec((tm, tn), lambda i, j: (i, j)),
        out_shape=jax.ShapeDtypeStruct((t, n), BF16),
        scratch_shapes=[pltpu.VMEM((tm, d), BF16), pltpu.VMEM((2, tm, d), F32), pltpu.SemaphoreType.DMA((2,))],
        compiler_params=_params(("arbitrary", "arbitrary"), vmem + (6 << 20)),
        name="in_proj",
    )(*xs, mod3, mod3, g_attn, w_cat)


def _rope_half(t):
    lane = lax.broadcasted_iota(I32, t.shape, 1)
    return jnp.where(lane < 64, t + pltpu.roll(t, 64, axis=1), 0.0)


def _q_up_kernel(cfg, c_ref, gl_ref, w_ref, ga_ref, gb_ref, cs_ref, o_ref):
    cn = (_rms(c_ref[...].astype(F32), cfg.eps) * gl_ref[...]).astype(BF16)
    ga = ga_ref[...]
    gbcs = gb_ref[...] * cs_ref[...]
    for h in range(cfg.n_heads):
        y = jnp.dot(cn, w_ref[h], preferred_element_type=F32)
        a = y[:, :LANES]
        b = y[:, LANES:]
        aa = a * a
        ssq2 = jnp.sum((aa + b * b) + aa, axis=-1, keepdims=True)
        s = lax.rsqrt(ssq2 + 2.0 * cfg.qk_dim * cfg.eps)
        t = b * gbcs
        o_ref[h, :, :LANES] = (a * ga * s).astype(o_ref.dtype)
        o_ref[h, :, LANES:] = ((t + pltpu.roll(t, 64, axis=1)) * s).astype(o_ref.dtype)


def _q_up(cfg, z, g_qa, w_q, ga, gb, cs, tm):
    t = cfg.n_tokens
    h = cfg.n_heads
    r = cfg.q_rank
    col = cfg.f_width // r
    vmem = 2 * tm * r * 2 + 2 * h * r * 256 * 2 + 4 * h * tm * 256 * 2 + tm * 128 * 8 + tm * r * 8 + (8 << 20)
    return pl.pallas_call(
        functools.partial(_q_up_kernel, cfg),
        grid=(t // tm,),
        in_specs=[pl.BlockSpec((tm, r), lambda i: (i, col)),
                  pl.BlockSpec((1, r), lambda i: (0, 0)),
                  pl.BlockSpec((h, r, 256), lambda i: (0, 0, 0)),
                  pl.BlockSpec((1, LANES), lambda i: (0, 0)),
                  pl.BlockSpec((1, LANES), lambda i: (0, 0)),
                  pl.BlockSpec((tm, LANES), lambda i: (_pos_block(cfg, i, tm), 0))],
        out_specs=pl.BlockSpec((h, tm, 256), lambda i: (0, i, 0)),
        out_shape=jax.ShapeDtypeStruct((h, t, 256), BF16),
        compiler_params=_params(("arbitrary",), vmem),
        name="q_up",
    )(z, g_qa, w_q, ga, gb, cs)


def _kv_up_kernel(cfg, c_ref, pe_ref, gl_ref, w_ref, ga_ref, gb_ref, cs_ref, k_ref, v_ref):
    cn = (_rms(c_ref[...].astype(F32), cfg.eps) * gl_ref[...]).astype(BF16)
    pe = pe_ref[...].astype(F32)
    lane = lax.broadcasted_iota(I32, pe.shape, 1)
    ssq_pe = jnp.sum(jnp.where(lane < 64, pe * pe, 0.0), axis=-1, keepdims=True)
    kr = _rope_half(pe * gb_ref[...] * cs_ref[...])
    ga = ga_ref[...]
    for h in range(cfg.n_heads):
        y = jnp.dot(cn, w_ref[h], preferred_element_type=F32)
        kn = y[:, :LANES]
        s = lax.rsqrt((jnp.sum(kn * kn, axis=-1, keepdims=True) + ssq_pe) * (1.0 / cfg.qk_dim) + cfg.eps)
        k_ref[h, :, :LANES] = (kn * ga * s).astype(k_ref.dtype)
        k_ref[h, :, LANES:] = (kr * s).astype(k_ref.dtype)
        v_ref[h, :, :LANES] = y[:, LANES:].astype(v_ref.dtype)
        v_ref[h, :, LANES:] = jnp.where(lane == 0, 1.0, 0.0).astype(v_ref.dtype)


def _kv_up(cfg, z, g_kva, w_kv, ga, gb, cs, tm):
    t = cfg.n_tokens
    h = cfg.n_heads
    r = cfg.kv_rank
    col_c = (cfg.f_width + cfg.q_rank) // r
    col_pe = (cfg.f_width + cfg.q_rank + cfg.kv_rank) // LANES
    vmem = 2 * tm * (r + 128) * 2 + 2 * h * r * 256 * 2 + 4 * h * tm * 512 * 2 + tm * r * 8 + (8 << 20)
    return pl.pallas_call(
        functools.partial(_kv_up_kernel, cfg),
        grid=(t // tm,),
        in_specs=[pl.BlockSpec((tm, r), lambda i: (i, col_c)),
                  pl.BlockSpec((tm, LANES), lambda i: (i, col_pe)),
                  pl.BlockSpec((1, r), lambda i: (0, 0)),
                  pl.BlockSpec((h, r, 256), lambda i: (0, 0, 0)),
                  pl.BlockSpec((1, LANES), lambda i: (0, 0)),
                  pl.BlockSpec((1, LANES), lambda i: (0, 0)),
                  pl.BlockSpec((tm, LANES), lambda i: (_pos_block(cfg, i, tm), 0))],
        out_specs=[pl.BlockSpec((h, tm, 256), lambda i: (0, i, 0)),
                   pl.BlockSpec((h, tm, 256), lambda i: (0, i, 0))],
        out_shape=[jax.ShapeDtypeStruct((h, t, 256), BF16),
                   jax.ShapeDtypeStruct((h, t, 256), BF16)],
        compiler_params=_params(("arbitrary",), vmem),
        name="kv_up",
    )(z, z, g_kva, w_kv, ga, gb, cs)


def _chunk_len(cfg):
    chunk = max(s for _, s in cfg.streams)
    for (b, s), off in zip(cfg.streams, cfg.tok_offsets):
        assert chunk % s == 0 and (b * s) % chunk == 0 and off % chunk == 0
    return chunk


def _in_stream_chunks(cfg, k, c, chunk):
    lo = cfg.tok_offsets[k] // chunk
    hi = lo + cfg.streams[k][0] * cfg.streams[k][1] // chunk
    return jnp.logical_and(c >= lo, c < hi)


def _attn_kernel(cfg, chunk, tq, tk, q_ref, k_ref, v_ref, o_ref):
    c = pl.program_id(0)
    qi = pl.program_id(2)
    for k, (_, s) in enumerate(cfg.streams):
        @pl.when(_in_stream_chunks(cfg, k, c, chunk))
        def _(s=s):
            start = ((qi * tq) // s) * s
            for hh in range(ATTN_HEADS_PER_STEP):
                q = q_ref[hh]
                m = jnp.full((tq, 1), -jnp.inf, F32)
                acc = jnp.zeros((tq, 2 * LANES), F32)
                for j in range(s // tk):
                    rows = pl.ds(pl.multiple_of(start + j * tk, tk), tk)
                    sc = lax.dot_general(q, k_ref[hh, rows, :], (((1,), (1,)), ((), ())),
                                         preferred_element_type=F32)
                    m_new = jnp.maximum(m, jnp.max(sc, axis=-1, keepdims=True))
                    p = jnp.exp2((sc - m_new).astype(BF16))
                    acc = jnp.exp2(m - m_new) * acc + jnp.dot(p, v_ref[hh, rows, :], preferred_element_type=F32)
                    m = m_new
                o_ref[:, hh * LANES:(hh + 1) * LANES] = (acc[:, :LANES] / acc[:, LANES:LANES + 1]).astype(o_ref.dtype)


ATTN_HEADS_PER_STEP = 2


def _attn(cfg, q, k, v):
    chunk = _chunk_len(cfg)
    h = cfg.n_heads
    t = cfg.n_tokens
    s_min = min(s for _, s in cfg.streams)
    tq = _tile(s_min, 1024)
    tk = _tile(s_min, 256)
    nq = chunk // tq
    hs = ATTN_HEADS_PER_STEP
    assert h % hs == 0
    vmem = hs * (4 * tq * 256 * 2 + 2 * chunk * 512 * 2 + 6 * tq * 256 * 4) + 8 * tq * tk * 4 + (8 << 20)
    return pl.pallas_call(
        functools.partial(_attn_kernel, cfg, chunk, tq, tk),
        grid=(t // chunk, h // hs, nq),
        in_specs=[pl.BlockSpec((hs, tq, 256), lambda ci, hi, qi: (hi, ci * nq + qi, 0)),
                  pl.BlockSpec((hs, chunk, 256), lambda ci, hi, qi: (hi, ci, 0)),
                  pl.BlockSpec((hs, chunk, 256), lambda ci, hi, qi: (hi, ci, 0))],
        out_specs=pl.BlockSpec((tq, hs * LANES), lambda ci, hi, qi: (ci * nq + qi, hi)),
        out_shape=jax.ShapeDtypeStruct((t, h * LANES), BF16),
        compiler_params=_params(("arbitrary", "arbitrary", "arbitrary"), vmem),
        name="attn",
    )(q, k, v)


def _fold_kernel(cc_ref, sc_ref, w_ref, o_ref):
    w = w_ref[0]
    o_ref[0, :, :256] = jnp.dot(cc_ref[...], w, preferred_element_type=F32,
                                precision=lax.Precision.HIGHEST).astype(o_ref.dtype)
    o_ref[0, :, 256:] = jnp.dot(sc_ref[...], w, preferred_element_type=F32,
                                precision=lax.Precision.HIGHEST).astype(o_ref.dtype)


def _fold_fourier_weights(cfg, cc, sc, w_fmix):
    g, c = cfg.f_groups, cfg.f_gdim
    return pl.pallas_call(
        _fold_kernel,
        grid=(g,),
        in_specs=[pl.BlockSpec((c, c), lambda i: (0, 0)),
                  pl.BlockSpec((c, c), lambda i: (0, 0)),
                  pl.BlockSpec((1, c, c), lambda i: (i, 0, 0))],
        out_specs=pl.BlockSpec((1, c, 2 * c), lambda i: (i, 0, 0)),
        out_shape=jax.ShapeDtypeStruct((g, c, 2 * c), BF16),
        compiler_params=_params(("arbitrary",), 16 << 20),
        name="fold_fourier",
    )(cc, sc, w_fmix)


MID_ROWS = 16


def _half_block(cfg, i, tm):
    hr0 = i * tm
    out = None
    for k, (b, s) in enumerate(cfg.streams):
        local = hr0 - cfg.tok_offsets[k] // 2
        seq = local // (s // 2)
        s0 = local % (s // 2)
        vals = (cfg.tok_offsets[k] + seq * s + s0, s0, cfg.row_offsets[k] + seq, s)
        if out is None:
            out = vals
        else:
            here = hr0 >= cfg.tok_offsets[k] // 2
            out = tuple(jnp.where(here, v, o) for v, o in zip(vals, out))
    return out


def _four1_kernel(cfg, tm, f_ref, r_ref, n_ref, m_ref, j_ref, w_ref, ab_ref, mid_ref):
    _, s0, _, _ = _half_block(cfg, pl.program_id(0), tm)
    c = cfg.f_gdim
    row = lax.broadcasted_iota(I32, (tm, 1), 0)
    for g in range(cfg.f_groups):
        sl = slice(g * c, (g + 1) * c)
        rev = jnp.dot(j_ref[...], r_ref[:, sl], preferred_element_type=F32)
        rev0 = jnp.where(s0 == 0, 0.0, n_ref[0:1, sl].astype(F32))
        rev = jnp.where(row == 0, rev0, rev)
        f = f_ref[:, sl].astype(F32)
        a = jnp.dot((f + rev).astype(BF16), w_ref[g, :, :c], preferred_element_type=F32)
        b = jnp.dot((f - rev).astype(BF16), w_ref[g, :, c:], preferred_element_type=F32)
        ab_ref[0, :, sl] = a.astype(ab_ref.dtype)
        ab_ref[1, :, sl] = b.astype(ab_ref.dtype)

    @pl.when(s0 == 0)
    def _():
        for g in range(cfg.f_groups):
            sl = slice(g * c, (g + 1) * c)
            mid_ref[:, sl] = jnp.dot(m_ref[:, sl], w_ref[g, :, :c], preferred_element_type=F32)


def _four1(cfg, z, w_fold, flip, tm):
    t = cfg.n_tokens
    fw = cfg.f_width
    g, c = cfg.f_groups, cfg.f_gdim

    def cur(i):
        return (_half_block(cfg, i, tm)[0] // tm, 0)

    def partner(i):
        row, s0, _, s = _half_block(cfg, i, tm)
        return ((row - s0 + s - s0) // tm - 1, 0)

    def after_partner(i):
        row, s0, _, s = _half_block(cfg, i, tm)
        return (jnp.minimum((row - s0 + s - s0) // MID_ROWS, t // MID_ROWS - 1), 0)

    def middle(i):
        row, s0, _, s = _half_block(cfg, i, tm)
        return ((row - s0 + s // 2) // MID_ROWS, 0)

    vmem = 4 * tm * fw * 2 + 2 * g * c * 2 * c * 2 + 4 * tm * fw * 2 + 6 * tm * c * 4 + (8 << 20)
    return pl.pallas_call(
        functools.partial(_four1_kernel, cfg, tm),
        grid=(t // 2 // tm,),
        in_specs=[pl.BlockSpec((tm, fw), cur),
                  pl.BlockSpec((tm, fw), partner),
                  pl.BlockSpec((MID_ROWS, fw), after_partner),
                  pl.BlockSpec((MID_ROWS, fw), middle),
                  pl.BlockSpec((tm, tm), lambda i: (0, 0)),
                  pl.BlockSpec((g, c, 2 * c), lambda i: (0, 0, 0))],
        out_specs=[pl.BlockSpec((2, tm, fw), lambda i: (0, i, 0)),
                   pl.BlockSpec((MID_ROWS, fw), lambda i: (_half_block(cfg, i, tm)[2], 0))],
        out_shape=[jax.ShapeDtypeStruct((2, t // 2, fw), BF16),
                   jax.ShapeDtypeStruct((cfg.n_rows * MID_ROWS, fw), F32)],
        compiler_params=_params(("arbitrary",), vmem),
        name="four1",
    )(z, z, z, z, flip, w_fold)


def _four2_kernel(cfg, chunk, tm, *refs):
    ns = len(cfg.streams)
    ab_ref, mid_ref, o_ref = refs[2 * ns:]
    c = pl.program_id(0)
    mi = pl.program_id(2)
    row = lax.broadcasted_iota(I32, (tm, 1), 0)
    for k, (_, s) in enumerate(cfg.streams):
        @pl.when(_in_stream_chunks(cfg, k, c, chunk))
        def _(k=k, s=s):
            half = s // 2
            start = pl.multiple_of(((mi * tm) // s) * half, half)
            y = jnp.dot(refs[2 * k][...], ab_ref[0, pl.ds(start, half), :], preferred_element_type=F32)
            y = y + jnp.dot(refs[2 * k + 1][...], ab_ref[1, pl.ds(start, half), :], preferred_element_type=F32)
            sign = jnp.where(row % 2 == 0, s ** -0.5, -(s ** -0.5))
            o_ref[...] = (y + sign * mid_ref[0:1, :]).astype(o_ref.dtype)


def _four2(cfg, dfts, ab, mid):
    chunk = _chunk_len(cfg)
    t = cfg.n_tokens
    fw = cfg.f_width
    tm = _tile(min(s for _, s in cfg.streams), 512)
    tn = _tile(fw, 1024)
    nm = chunk // tm
    assert tm % 2 == 0

    def d_spec(k):
        s = cfg.streams[k][1]

        def imap(ci, ni, mi):
            inside = _in_stream_chunks(cfg, k, ci, chunk)
            before = ci < cfg.tok_offsets[k] // chunk
            return (jnp.where(inside, ((mi * tm) % s) // tm, jnp.where(before, 0, s // tm - 1)), 0)
        return pl.BlockSpec((tm, s // 2), imap)

    in_specs, args, vmem = [], [], 0
    for k, (dc, ds) in enumerate(dfts):
        in_specs += [d_spec(k), d_spec(k)]
        args += [dc, ds]
        vmem += 4 * tm * (cfg.streams[k][1] // 2) * 2
    in_specs.append(pl.BlockSpec((2, chunk // 2, tn), lambda ci, ni, mi: (0, ci, ni)))
    in_specs.append(pl.BlockSpec((MID_ROWS, tn), lambda ci, ni, mi: (_mod_row(cfg, ci * chunk + mi * tm), ni)))
    vmem += 4 * (chunk // 2) * tn * 2 + 2 * tm * tn * 2 + 3 * tm * tn * 4 + (8 << 20)
    return pl.pallas_call(
        functools.partial(_four2_kernel, cfg, chunk, tm),
        grid=(t // chunk, fw // tn, nm),
        in_specs=in_specs,
        out_specs=pl.BlockSpec((tm, tn), lambda ci, ni, mi: (ci * nm + mi, ni)),
        out_shape=jax.ShapeDtypeStruct((t, fw), BF16),
        compiler_params=_params(("arbitrary", "arbitrary", "arbitrary"), vmem),
        name="four2",
    )(*args, ab, mid)


def _out_proj_kernel(cfg, tm, *refs):
    ns = len(cfg.streams)
    x_refs = refs[:ns]
    gt_ref, a_ref, f_ref, wa_ref, wb_ref, o_ref = refs[ns:]
    i = pl.program_id(0)
    mix = jnp.dot(a_ref[...], wa_ref[...], preferred_element_type=F32)
    mix = mix + jnp.dot(f_ref[...], wb_ref[...], preferred_element_type=F32)
    for k in range(ns):
        @pl.when(_in_stream(cfg, k, i, tm))
        def _(k=k):
            o_ref[...] = x_refs[k][...] + gt_ref[0] * mix


def _out_proj(cfg, xs, mod3, o_attn, o_four, wa, wb):
    d = cfg.d_model
    t = cfg.n_tokens
    aw, fw = cfg.attn_width, cfg.f_width
    tm = _tile(min(s for _, s in cfg.streams), 1024)
    tn = _tile(d, 512)
    nj = d // tn

    def x_spec(k):
        def imap(i, j):
            inside = _in_stream(cfg, k, i, tm)
            before = i < cfg.tok_offsets[k] // tm
            return (_stream_block(cfg, k, i, tm), jnp.where(inside, j, jnp.where(before, 0, nj - 1)))
        return pl.BlockSpec((tm, tn), imap)

    in_specs = [x_spec(k) for k in range(len(xs))] + [
        pl.BlockSpec((1, 1, tn), lambda i, j: (_mod_row(cfg, i * tm), 0, 2 * nj + j)),
        pl.BlockSpec((tm, aw), lambda i, j: (i, 0)),
        pl.BlockSpec((tm, fw), lambda i, j: (i, 0)),
        pl.BlockSpec((aw, tn), lambda i, j: (0, j)),
        pl.BlockSpec((fw, tn), lambda i, j: (0, j)),
    ]
    vmem = (2 * tm * tn * 4 * len(xs) + 2 * tm * (aw + fw) * 2 + 2 * (aw + fw) * tn * 2
            + 2 * tm * tn * 4 + 2 * tm * tn * 4 + (8 << 20))
    return pl.pallas_call(
        functools.partial(_out_proj_kernel, cfg, tm),
        grid=(t // tm, nj),
        in_specs=in_specs,
        out_specs=pl.BlockSpec((tm, tn), lambda i, j: (i, j)),
        out_shape=jax.ShapeDtypeStruct((t, d), F32),
        compiler_params=_params(("arbitrary", "arbitrary"), vmem),
        name="out_proj",
    )(*xs, mod3, o_attn, o_four, wa, wb)


def _router_kernel(cfg, tm, x_ref, sh_ref, sc_ref, g_ref, wr_ref, tri_ref,
                   hp_ref, mi_ref, gt_ref, cnt_ref, carry_ref):
    i = pl.program_id(0)
    ng, ne = cfg.n_groups, cfg.e_per_group
    n_exp = cfg.n_experts

    @pl.when(i == 0)
    def _():
        carry_ref[...] = jnp.zeros_like(carry_ref)

    h2 = _rms(x_ref[...], cfg.eps) * g_ref[...] * (1.0 + sc_ref[0]) + sh_ref[0]
    for c in range(cfg.d_model // PACK_CHUNK):
        lo = h2[:, c * PACK_CHUNK:c * PACK_CHUNK + LANES]
        hi = h2[:, c * PACK_CHUNK + LANES:(c + 1) * PACK_CHUNK]
        hp_ref[:, c * LANES:(c + 1) * LANES] = _pack_words(lo, hi)

    h_hi = h2.astype(BF16)
    h_lo = (h2 - h_hi.astype(F32)).astype(BF16)
    p1 = jnp.dot(h_hi, wr_ref[...], preferred_element_type=F32)
    p2 = jnp.dot(h_lo, wr_ref[:, :LANES], preferred_element_type=F32)
    logits = (p1[:, :LANES] + p1[:, LANES:] + p2).T

    gl = logits[0:ng]
    io = lax.broadcasted_iota(I32, (ng, tm), 0)
    gm = jnp.max(gl, axis=0, keepdims=True)
    p_group = 1.0 / jnp.sum(jnp.exp(gl - gm), axis=0, keepdims=True)
    gidx = jnp.min(jnp.where(gl == gm, io, ng), axis=0, keepdims=True)
    sel = jnp.zeros((ne, tm), F32)
    for g in range(ng):
        sel = jnp.where(gidx == g, logits[ng + g * ne:ng + (g + 1) * ne], sel)
    ie = lax.broadcasted_iota(I32, (ne, tm), 0)
    m1 = jnp.max(sel, axis=0, keepdims=True)
    i1 = jnp.min(jnp.where(sel == m1, ie, ne), axis=0, keepdims=True)
    sel2 = jnp.where(ie == i1, -jnp.inf, sel)
    m2 = jnp.max(sel2, axis=0, keepdims=True)
    i2 = jnp.min(jnp.where(sel2 == m2, ie, ne), axis=0, keepdims=True)
    e21 = jnp.exp(m2 - m1)
    gate0 = p_group / (1.0 + e21)
    gate1 = p_group * e21 / (1.0 + e21)
    e0 = gidx * ne + i1
    e1 = gidx * ne + i2

    ix = lax.broadcasted_iota(I32, (n_exp, tm), 0)
    hit0 = ix == e0
    hit1 = ix == e1
    member = jnp.logical_or(hit0, hit1).astype(F32)
    before = jnp.dot(member.astype(BF16), tri_ref[...], preferred_element_type=F32)
    total = before + carry_ref[:, 0:1]
    r0 = jnp.sum(jnp.where(hit0, total, 0.0), axis=0, keepdims=True).astype(I32)
    r1 = jnp.sum(jnp.where(hit1, total, 0.0), axis=0, keepdims=True).astype(I32)
    carry_ref[...] = carry_ref[...] + jnp.sum(member, axis=1, keepdims=True)
    cnt_ref[...] = carry_ref[...].astype(I32)

    i8 = lax.broadcasted_iota(I32, (8, tm), 0)
    mi_ref[...] = jnp.where(i8 == 0, e0, jnp.where(i8 == 1, e1, jnp.where(i8 == 2, r0, jnp.where(i8 == 3, r1, 0))))
    il = lax.broadcasted_iota(I32, (LANES, tm), 0)
    gt_ref[...] = jnp.where(il == 0, gate0, jnp.where(il == 1, gate1, 0.0)).T


def _router(cfg, x1, mod3, g_ffn, wr_cat, tri, tm):
    d = cfg.d_model
    t = cfg.n_tokens
    ne = cfg.n_experts
    vmem = 2 * tm * d * 4 + 2 * tm * d * 2 + 2 * d * 256 * 2 + 6 * tm * d * 4 + 2 * tm * tm * 2 + (8 << 20)
    return pl.pallas_call(
        functools.partial(_router_kernel, cfg, tm),
        grid=(t // tm,),
        in_specs=[pl.BlockSpec((tm, d), lambda i: (i, 0)),
                  pl.BlockSpec((1, 1, d), lambda i: (_mod_row(cfg, i * tm), 0, 3)),
                  pl.BlockSpec((1, 1, d), lambda i: (_mod_row(cfg, i * tm), 0, 4)),
                  pl.BlockSpec((1, d), lambda i: (0, 0)),
                  pl.BlockSpec((d, 256), lambda i: (0, 0)),
                  pl.BlockSpec((tm, tm), lambda i: (0, 0))],
        out_specs=[pl.BlockSpec((tm, d // 2), lambda i: (i, 0)),
                   pl.BlockSpec((8, tm), lambda i: (0, i)),
                   pl.BlockSpec((tm, LANES), lambda i: (i, 0)),
                   pl.BlockSpec((ne, LANES), lambda i: (0, 0))],
        out_shape=[jax.ShapeDtypeStruct((t, d // 2), U32),
                   jax.ShapeDtypeStruct((8, t), I32),
                   jax.ShapeDtypeStruct((t, LANES), F32),
                   jax.ShapeDtypeStruct((ne, LANES), I32)],
        scratch_shapes=[pltpu.VMEM((ne, LANES), F32)],
        compiler_params=_params(("arbitrary",), vmem),
        name="router",
    )(x1, mod3, mod3, g_ffn, wr_cat, tri)


def _dest_row(start_ref, meta_ref, kk, r):
    return start_ref[meta_ref[kk, r]] + meta_ref[2 + kk, r]


def _dispatch_kernel(tm, bm, n_zero, start_ref, zero_ref, meta_ref, h_ref, xs_ref, zbuf_ref, sem, zsem):
    @pl.when(pl.program_id(0) == 0)
    def _():
        zbuf_ref[...] = jnp.zeros_like(zbuf_ref)

        def fill(k):
            dst = xs_ref.at[pl.ds(pl.multiple_of(zero_ref[k] * bm, bm), bm)]
            return pltpu.make_async_copy(zbuf_ref, dst, zsem)

        def start(k, carry):
            @pl.when(zero_ref[k] >= 0)
            def _():
                fill(k).start()
            return carry

        def finish(k, carry):
            @pl.when(zero_ref[k] >= 0)
            def _():
                fill(k).wait()
            return carry

        lax.fori_loop(0, n_zero, start, 0)
        lax.fori_loop(0, n_zero, finish, 0)

    def issue(r, carry):
        for kk in range(2):
            dst = xs_ref.at[pl.ds(_dest_row(start_ref, meta_ref, kk, r), 1)]
            pltpu.make_async_copy(h_ref.at[pl.ds(r, 1)], dst, sem).start(priority=kk)
        return carry

    lax.fori_loop(0, tm, issue, 0, unroll=8)
    for kk in range(2):
        pltpu.make_async_copy(h_ref, xs_ref.at[pl.ds(0, tm)], sem).wait()


def _dispatch(cfg, pad_start, zero_blocks, meta, h2p, n_rows, tm):
    t = cfg.n_tokens
    w = h2p.shape[1]
    bm = cfg.moe_block
    grid_spec = pltpu.PrefetchScalarGridSpec(
        num_scalar_prefetch=2,
        grid=(t // tm,),
        in_specs=[pl.BlockSpec((8, tm), lambda i, st, zb: (0, i), memory_space=pltpu.SMEM),
                  pl.BlockSpec((tm, w), lambda i, st, zb: (i, 0))],
        out_specs=pl.BlockSpec(memory_space=pl.ANY),
        scratch_shapes=[pltpu.VMEM((bm, w), U32), pltpu.SemaphoreType.DMA(()), pltpu.SemaphoreType.DMA(())],
    )
    return pl.pallas_call(
        functools.partial(_dispatch_kernel, tm, bm, zero_blocks.shape[0]),
        grid_spec=grid_spec,
        out_shape=jax.ShapeDtypeStruct((n_rows, w), U32),
        compiler_params=_params(("arbitrary",), 4 * tm * w * 4 + bm * w * 4 + (4 << 20)),
        name="dispatch",
    )(pad_start, zero_blocks, meta, h2p)


def _expert_schedule(cfg, cnt, n_blocks):
    bm = cfg.moe_block
    padded = (cnt + bm - 1) // bm * bm
    pad_end = jnp.cumsum(padded)
    pad_start = (pad_end - padded).astype(I32)
    blk = jnp.arange(n_blocks, dtype=I32)
    be = jnp.minimum(jnp.sum((pad_end[None, :] <= (blk * bm)[:, None]).astype(I32), axis=1), cfg.n_experts - 1)
    nv = (pad_end[-1] // bm).astype(I32)
    prev = jnp.concatenate([jnp.full((1,), -1, I32), be[:-1]])
    first = jnp.logical_and(blk < nv, be != prev)
    suffix = lax.cummin(jnp.where(first, blk, n_blocks), axis=0, reverse=True)
    next_first = jnp.concatenate([suffix[1:], jnp.full((1,), n_blocks, I32)])
    last = next_first >= n_blocks
    nxt = be[jnp.where(last, 0, next_first)]
    run = jnp.cumsum(first.astype(I32)) - 1
    n_runs = jnp.sum(first.astype(I32))
    half = (cnt[be] - (blk * bm - pad_start[be])) <= bm // 2
    tables = (be.astype(I32), first.astype(I32), nxt.astype(I32), last.astype(I32), run.astype(I32),
              half.astype(I32))
    zero_blocks = jnp.concatenate([jnp.where(padded > 0, pad_end // bm - 1, -1),
                                   jnp.where(blk >= nv, blk, -1)]).astype(I32)
    return pad_start, zero_blocks, tables + (n_runs[None], nv[None])


N_SCHED = 8


def _stage_expert_weights(sched, i, tile, n_tiles, copies):
    be_ref, first_ref, nxt_ref, last_ref, run_ref, _, n_runs_ref, nv_ref = sched
    slot = (tile * n_runs_ref[0] + run_ref[i]) % 2

    @pl.when(jnp.logical_and(i < nv_ref[0], first_ref[i] == 1))
    def _():
        @pl.when(jnp.logical_and(tile == 0, i == 0))
        def _():
            for cp in copies:
                cp(be_ref[i], tile, slot).start()

        for cp in copies:
            cp(be_ref[i], tile, slot).wait()
        is_last = last_ref[i] == 1

        @pl.when(jnp.logical_not(is_last))
        def _():
            for cp in copies:
                cp(nxt_ref[i], tile, 1 - slot).start()

        @pl.when(jnp.logical_and(is_last, tile + 1 < n_tiles))
        def _():
            for cp in copies:
                cp(nxt_ref[i], tile + 1, 1 - slot).start()

    return slot


def _moe1_kernel(cfg, tf, n_tiles, *refs):
    sched = refs[:N_SCHED]
    x_ref, w1_ref, w3_ref, o_ref, xb_ref, s1_ref, s3_ref, sem = refs[N_SCHED:]
    nv_ref = sched[-1]
    j = pl.program_id(0)
    i = pl.program_id(1)

    def copy(w_ref, stage_ref, k):
        def make(e, jj, slot):
            src = w_ref.at[e, :, pl.ds(pl.multiple_of(jj * tf, tf), tf)]
            return pltpu.make_async_copy(src, stage_ref.at[slot], sem.at[k, slot])
        return make

    slot = _stage_expert_weights(sched, i, j, n_tiles, [copy(w1_ref, s1_ref, 0), copy(w3_ref, s3_ref, 1)])
    bm = x_ref.shape[0]
    valid = i < nv_ref[0]
    half = sched[5][i] == 1

    def compute(m):
        for c in range(cfg.d_model // PACK_CHUNK):
            lo, hi = _unpack_words(x_ref[:m, c * LANES:(c + 1) * LANES])
            xb_ref[:m, c * PACK_CHUNK:c * PACK_CHUNK + LANES] = lo.astype(BF16)
            xb_ref[:m, c * PACK_CHUNK + LANES:(c + 1) * PACK_CHUNK] = hi.astype(BF16)
        x = xb_ref[:m, :]
        a = jnp.dot(x, s1_ref[slot].astype(BF16), preferred_element_type=F32)
        b = jnp.dot(x, s3_ref[slot].astype(BF16), preferred_element_type=F32)
        o_ref[:m, :] = (a * jax.nn.sigmoid(a) * b).astype(o_ref.dtype)
        if m < bm:
            o_ref[m:, :] = jnp.zeros((bm - m, o_ref.shape[1]), o_ref.dtype)

    @pl.when(jnp.logical_and(valid, jnp.logical_not(half)))
    def _():
        compute(bm)

    @pl.when(jnp.logical_and(valid, half))
    def _():
        compute(bm // 2)

    @pl.when(i >= nv_ref[0])
    def _():
        o_ref[...] = jnp.zeros_like(o_ref)


def _moe1(cfg, sched, xs, w1, w3):
    d, f = cfg.d_model, cfg.d_expert
    bm = cfg.moe_block
    rows = xs.shape[0]
    tf = _tile(f, cfg.moe_up_tile)
    n_tiles = f // tf

    def x_map(j, i, *sched):
        return (jnp.minimum(i, sched[-1][0] - 1), 0)

    grid_spec = pltpu.PrefetchScalarGridSpec(
        num_scalar_prefetch=N_SCHED,
        grid=(n_tiles, rows // bm),
        in_specs=[pl.BlockSpec((bm, d // 2), x_map),
                  pl.BlockSpec(memory_space=pl.ANY),
                  pl.BlockSpec(memory_space=pl.ANY)],
        out_specs=pl.BlockSpec((bm, tf), lambda j, i, *_: (i, j)),
        scratch_shapes=[pltpu.VMEM((bm, d), BF16),
                        pltpu.VMEM((2, d, tf), F32), pltpu.VMEM((2, d, tf), F32),
                        pltpu.SemaphoreType.DMA((2, 2))],
    )
    vmem = 2 * bm * d * 2 + bm * d * 2 + 4 * d * tf * 4 + 2 * d * tf * 2 + 2 * bm * tf * 2 + 4 * bm * tf * 4 + (4 << 20)
    return pl.pallas_call(
        functools.partial(_moe1_kernel, cfg, tf, n_tiles),
        grid_spec=grid_spec,
        out_shape=jax.ShapeDtypeStruct((rows, f), BF16),
        compiler_params=_params(("arbitrary", "arbitrary"), vmem),
        name="moe1",
    )(*sched, xs, w1, w3)


def _moe2_kernel(cfg, tn, n_tiles, *refs):
    sched = refs[:N_SCHED]
    h_ref, w2_ref, o_ref, s2_ref, sem = refs[N_SCHED:]
    nv_ref = sched[-1]
    n = pl.program_id(0)
    i = pl.program_id(1)

    def copy(e, nn, slot):
        src = w2_ref.at[e, :, pl.ds(pl.multiple_of(nn * tn, tn), tn)]
        return pltpu.make_async_copy(src, s2_ref.at[slot], sem.at[slot])

    slot = _stage_expert_weights(sched, i, n, n_tiles, [copy])
    bm = h_ref.shape[0]
    valid = i < nv_ref[0]
    half = sched[5][i] == 1

    def compute(m):
        y = jnp.dot(h_ref[:m, :], s2_ref[slot].astype(BF16), preferred_element_type=F32)
        for c in range(tn // PACK_CHUNK):
            o_ref[:m, c * LANES:(c + 1) * LANES] = _pack_words(
                y[:, c * PACK_CHUNK:c * PACK_CHUNK + LANES], y[:, c * PACK_CHUNK + LANES:(c + 1) * PACK_CHUNK])
        if m < bm:
            o_ref[m:, :] = jnp.zeros((bm - m, o_ref.shape[1]), o_ref.dtype)

    @pl.when(jnp.logical_and(valid, jnp.logical_not(half)))
    def _():
        compute(bm)

    @pl.when(jnp.logical_and(valid, half))
    def _():
        compute(bm // 2)

    @pl.when(i >= nv_ref[0])
    def _():
        o_ref[...] = jnp.zeros_like(o_ref)


def _moe2(cfg, sched, hmid, w2):
    d, f = cfg.d_model, cfg.d_expert
    bm = cfg.moe_block
    rows = hmid.shape[0]
    tn = _tile(d, cfg.moe_down_tile)
    assert tn % PACK_CHUNK == 0
    n_tiles = d // tn

    def h_map(n, i, *sched):
        return (jnp.minimum(i, sched[-1][0] - 1), 0)

    grid_spec = pltpu.PrefetchScalarGridSpec(
        num_scalar_prefetch=N_SCHED,
        grid=(n_tiles, rows // bm),
        in_specs=[pl.BlockSpec((bm, f), h_map),
                  pl.BlockSpec(memory_space=pl.ANY)],
        out_specs=pl.BlockSpec((bm, tn // 2), lambda n, i, *_: (i, n)),
        scratch_shapes=[pltpu.VMEM((2, f, tn), F32), pltpu.SemaphoreType.DMA((2,))],
    )
    vmem = 2 * bm * f * 2 + 2 * f * tn * 4 + f * tn * 2 + 2 * bm * tn * 2 + 2 * bm * tn * 4 + (8 << 20)
    return pl.pallas_call(
        functools.partial(_moe2_kernel, cfg, tn, n_tiles),
        grid_spec=grid_spec,
        out_shape=jax.ShapeDtypeStruct((rows, d // 2), U32),
        compiler_params=_params(("arbitrary", "arbitrary"), vmem),
        name="moe2",
    )(*sched, hmid, w2)


def _combine_kernel(cfg, tm, n_blk, start_ref, meta_ref, meta_nxt_ref, x_ref, gt2_ref, gate_ref, y_ref, o_ref,
                    buf_ref, sem):
    i = pl.program_id(0)
    slot = i % 2

    n_chunks = cfg.d_model // PACK_CHUNK
    rows_per_chunk = tm // n_chunks

    def start_row(m_ref, r, sl):
        for kk in range(2):
            src = y_ref.at[pl.ds(_dest_row(start_ref, m_ref, kk, r), 1)]
            pltpu.make_async_copy(src, buf_ref.at[sl, kk, pl.ds(r, 1)], sem.at[sl]).start(priority=kk)

    def wait_slot(sl):
        for kk in range(2):
            pltpu.make_async_copy(y_ref.at[pl.ds(0, tm)], buf_ref.at[sl, kk], sem.at[sl]).wait()

    @pl.when(i == 0)
    def _():
        def body(r, carry):
            start_row(meta_ref, r, slot)
            return carry
        lax.fori_loop(0, tm, body, 0, unroll=8)

    wait_slot(slot)

    g0 = gate_ref[:, 0:1]
    g1 = gate_ref[:, 1:2]
    for c in range(n_chunks):
        lo0, hi0 = _unpack_words(buf_ref[slot, 0, :, c * LANES:(c + 1) * LANES])
        lo1, hi1 = _unpack_words(buf_ref[slot, 1, :, c * LANES:(c + 1) * LANES])
        sl_lo = slice(c * PACK_CHUNK, c * PACK_CHUNK + LANES)
        sl_hi = slice(c * PACK_CHUNK + LANES, (c + 1) * PACK_CHUNK)
        o_ref[:, sl_lo] = x_ref[:, sl_lo] + gt2_ref[0, :, sl_lo] * (g0 * lo0 + g1 * lo1)
        o_ref[:, sl_hi] = x_ref[:, sl_hi] + gt2_ref[0, :, sl_hi] * (g0 * hi0 + g1 * hi1)
        for r in range(c * rows_per_chunk, (c + 1) * rows_per_chunk):
            start_row(meta_nxt_ref, r, 1 - slot)

    @pl.when(i == n_blk - 1)
    def _():
        wait_slot(1 - slot)


def _combine(cfg, k_idx, pad_start, meta, x1, mod3, gates, y, tm):
    b, s = cfg.streams[k_idx]
    off = cfg.tok_offsets[k_idx]
    d = cfg.d_model
    n = b * s
    ob = off // tm
    row0 = cfg.row_offsets[k_idx]
    n_blk = n // tm
    grid_spec = pltpu.PrefetchScalarGridSpec(
        num_scalar_prefetch=1,
        grid=(n_blk,),
        in_specs=[pl.BlockSpec((8, tm), lambda i, st: (0, ob + i), memory_space=pltpu.SMEM),
                  pl.BlockSpec((8, tm), lambda i, st: (0, ob + jnp.minimum(i + 1, n_blk - 1)),
                               memory_space=pltpu.SMEM),
                  pl.BlockSpec((tm, d), lambda i, st: (ob + i, 0)),
                  pl.BlockSpec((1, 1, d), lambda i, st: (row0 + (i * tm) // s, 0, 5)),
                  pl.BlockSpec((tm, LANES), lambda i, st: (ob + i, 0)),
                  pl.BlockSpec(memory_space=pl.ANY)],
        out_specs=pl.BlockSpec((tm, d), lambda i, st: (i, 0)),
        scratch_shapes=[pltpu.VMEM((2, 2, tm, d // 2), U32), pltpu.SemaphoreType.DMA((2,))],
    )
    vmem = 4 * tm * d * 4 + 4 * tm * d * 2 + 2 * tm * 128 * 4 + 6 * tm * 128 * 4 + (8 << 20)
    return pl.pallas_call(
        functools.partial(_combine_kernel, cfg, tm, n_blk),
        grid_spec=grid_spec,
        out_shape=jax.ShapeDtypeStruct((n, d), F32),
        compiler_params=_params(("arbitrary",), vmem),
        name=f"combine{k_idx}",
    )(pad_start, meta, meta, x1, mod3, gates, y)


def _rot_half(a, axis=-1):
    h = a.shape[axis] // 2
    lo = lax.slice_in_dim(a, 0, h, axis=axis)
    hi = lax.slice_in_dim(a, h, 2 * h, axis=axis)
    return jnp.concatenate([hi, lo], axis=axis)


def _rope_table(cfg, n_pos):
    half = cfg.rope // 2
    inv_freq = cfg.theta ** (-2.0 * jnp.arange(half, dtype=F32) / cfg.rope)
    ang = jnp.arange(n_pos, dtype=F32)[:, None] * inv_freq[None, :]
    cos, sin = jnp.cos(ang), jnp.sin(ang)
    return jnp.concatenate([cos, cos, -sin, sin], axis=-1)


def _dft(n, scale):
    idx = jnp.arange(n, dtype=I32)
    ang = ((idx[:, None] * idx[None, :]) % n).astype(F32) * (2.0 * math.pi / n)
    return jnp.cos(ang) * scale, jnp.sin(ang) * scale


def _dft_half(n, split=64):
    s = jnp.arange(n // 2, dtype=I32)

    def table(rows):
        ang = ((rows[:, None] * s[None, :]) % n).astype(F32) * (2.0 * math.pi / n)
        return jnp.cos(ang), jnp.sin(ang)

    ch, sh = table(jnp.arange(n // split, dtype=I32) * split)
    cl, sl = table(jnp.arange(split, dtype=I32))
    scale = n ** -0.5
    cos = (ch[:, None, :] * cl[None] - sh[:, None, :] * sl[None]) * scale
    msin = (sh[:, None, :] * cl[None] + ch[:, None, :] * sl[None]) * -scale
    return cos.reshape(n, n // 2).astype(BF16), msin.reshape(n, n // 2).astype(BF16)


def _layer(cfg, xs, cs_, w_ada, b_ada, g_attn, w_in, g_qa, w_uq, g_kva, w_ukv, g_qn, g_kn,
           w_fmix, w_o, g_ffn, w_group, w_route, w1, w3, w2):
    d, h = cfg.d_model, cfg.n_heads
    t = cfg.n_tokens
    nope, rope = cfg.nope, cfg.rope
    assert nope == LANES and cfg.v_dim == LANES and rope == 64 and cfg.f_gdim == PACK_CHUNK
    assert cfg.f_width % cfg.q_rank == 0 and (cfg.f_width + cfg.q_rank) % cfg.kv_rank == 0
    qr, kr = cfg.q_rank, cfg.kv_rank

    w16 = w_in.astype(BF16)
    half = rope // 2
    c0 = qr + kr
    n_cat = w16.shape[1] + rope
    w_cat = jnp.concatenate(
        [w16[:, c0 + rope:], w16[:, :c0 + rope], w16[:, c0 + half:c0 + rope], w16[:, c0:c0 + half],
         jnp.zeros((d, -n_cat % _in_tile(n_cat)), BF16)], axis=1)

    wq3 = w_uq.reshape(qr, h, nope + rope)
    w_q = jnp.concatenate([wq3, _rot_half(wq3[..., nope:])], axis=-1).transpose(1, 0, 2).astype(BF16)
    w_kv = w_ukv.reshape(kr, h, nope + cfg.v_dim).transpose(1, 0, 2).astype(BF16)
    q_const = math.sqrt(2.0 * cfg.qk_dim) * cfg.qk_dim ** -0.5 * math.log2(math.e)
    ga_q = g_qn[None, :nope] * q_const
    gb_q = jnp.concatenate([g_qn[nope:], _rot_half(g_qn[nope:])])[None] * q_const
    ga_k, gb_k = g_kn[None, :nope], jnp.concatenate([g_kn[nope:], _rot_half(g_kn[nope:])])[None]
    wa, wb = w_o[:cfg.attn_width].astype(BF16), w_o[cfg.attn_width:].astype(BF16)
    wr = jnp.concatenate([w_group, w_route.reshape(d, cfg.n_experts)], axis=1)
    wr = jnp.pad(wr, ((0, 0), (0, LANES - wr.shape[1])))
    wr_hi = wr.astype(BF16)
    wr_cat = jnp.concatenate([wr_hi, (wr - wr_hi.astype(F32)).astype(BF16)], axis=1)

    s_max = max(s for _, s in cfg.streams)
    cs_tab = _rope_table(cfg, s_max)
    cc, sc = _dft(cfg.f_gdim, cfg.f_gdim ** -0.5)

    c_all = jnp.concatenate(cs_, axis=0)
    rows = -(-c_all.shape[0] // 8) * 8
    c_pad = jnp.pad(c_all, ((0, rows - c_all.shape[0]), (0, 0)))
    mod3 = _ada(c_pad, w_ada, b_ada[None]).reshape(rows, 1, 6 * d)

    x2d = [x.reshape(-1, d) for x in xs]
    z = _in_proj(cfg, x2d, mod3, g_attn[None], w_cat)
    tm = _tile(min(s for _, s in cfg.streams), 512)
    q = _q_up(cfg, z, g_qa[None], w_q, ga_q, gb_q, cs_tab, tm)
    k, v = _kv_up(cfg, z, g_kva[None], w_kv, ga_k, gb_k, cs_tab, tm)
    o_attn = _attn(cfg, q, k, v)

    w_fold = _fold_fourier_weights(cfg, cc, sc, w_fmix)
    tm_f = _tile(min(s for _, s in cfg.streams) // 2, 512)
    idx = jnp.arange(tm_f)
    flip = (idx[:, None] + idx[None, :] == tm_f).astype(BF16)
    ab, mid = _four1(cfg, z, w_fold, flip, tm_f)
    o_four = _four2(cfg, [_dft_half(s) for _, s in cfg.streams], ab, mid)

    x1 = _out_proj(cfg, x2d, mod3, o_attn, o_four, wa, wb)

    tm_r = _tile(tm, 256)
    tri = (jnp.arange(tm_r)[:, None] < jnp.arange(tm_r)[None, :]).astype(BF16)
    h2p, meta, gates, counts = _router(cfg, x1, mod3, g_ffn[None], wr_cat, tri, tm_r)

    bm = cfg.moe_block
    n_blocks = (2 * t) // bm + cfg.n_experts
    pad_start, zero_blocks, sched = _expert_schedule(cfg, counts[:, 0], n_blocks)

    xs_rows = _dispatch(cfg, pad_start, zero_blocks, meta, h2p, n_blocks * bm, _tile(tm, 256))
    hmid = _moe1(cfg, sched, xs_rows, w1, w3)
    y = _moe2(cfg, sched, hmid, w2)

    outs = []
    for ki, (b, s) in enumerate(cfg.streams):
        o = _combine(cfg, ki, pad_start, meta, x1, mod3, gates, y, _tile(tm, 256))
        outs.append(o.reshape(b, s, d))
    return tuple(outs)


def kernel(x_prompt, x_sample, c_prompt, c_sample, w_ada, b_ada, g_attn, w_in, g_qa, w_uq, g_kva, w_ukv,
           g_qn, g_kn, w_fmix, w_o, g_ffn, w_group, w_route, w1, w3, w2):
    cfg = Cfg()
    xs = (x_prompt, x_sample)
    cs_ = (c_prompt, c_sample)
    for l in range(w_ada.shape[0]):
        xs = _layer(cfg, xs, cs_, w_ada[l], b_ada[l], g_attn[l], w_in[l], g_qa[l], w_uq[l], g_kva[l],
                    w_ukv[l], g_qn[l], g_kn[l], w_fmix[l], w_o[l], g_ffn[l], w_group[l], w_route[l],
                    w1[l], w3[l], w2[l])
    return xs
```

```python
import dataclasses
import functools
import math

import jax
import jax.numpy as jnp
from jax import lax
from jax.experimental import pallas as pl
from jax.experimental.pallas import tpu as pltpu

BF16 = jnp.bfloat16
F32 = jnp.float32
U32 = jnp.uint32
I32 = jnp.int32

LANES = 128
MXU_DIM = 256
VMEM_CAP = 60 << 20
PACK_CHUNK = 2 * LANES


@dataclasses.dataclass(frozen=True)
class Cfg:
    d_model: int = 4096
    streams: tuple = ((8, 2048), (4, 4096))
    n_heads: int = 16
    nope: int = 128
    rope: int = 64
    v_dim: int = 128
    q_rank: int = 1024
    kv_rank: int = 512
    f_gdim: int = 256
    n_groups: int = 8
    e_per_group: int = 8
    d_expert: int = 1024
    theta: float = 10000.0
    eps: float = 1e-6
    moe_block: int = 512
    moe_up_tile: int = 512
    moe_down_tile: int = 2048

    @property
    def attn_width(self):
        return self.n_heads * self.v_dim

    @property
    def f_width(self):
        return self.d_model - self.attn_width

    @property
    def f_groups(self):
        return self.f_width // self.f_gdim

    @property
    def qk_dim(self):
        return self.nope + self.rope

    @property
    def n_experts(self):
        return self.n_groups * self.e_per_group

    @property
    def tok_offsets(self):
        offs, t = [], 0
        for b, s in self.streams:
            offs.append(t)
            t += b * s
        return tuple(offs)

    @property
    def row_offsets(self):
        offs, r = [], 0
        for b, _ in self.streams:
            offs.append(r)
            r += b
        return tuple(offs)

    @property
    def n_tokens(self):
        return sum(b * s for b, s in self.streams)

    @property
    def n_rows(self):
        return sum(b for b, _ in self.streams)


def _tile(n, want):
    if n <= want:
        return n
    t = want
    while n % t:
        t -= 8
    return t


def _params(sem, vmem_bytes, flags=None):
    return pltpu.CompilerParams(dimension_semantics=sem,
                                vmem_limit_bytes=int(min(VMEM_CAP, vmem_bytes)), flags=flags)


def _mod_row(cfg, t0):
    row = None
    for k, (b, s) in enumerate(cfg.streams):
        expr = cfg.row_offsets[k] + (t0 - cfg.tok_offsets[k]) // s
        row = expr if row is None else jnp.where(t0 >= cfg.tok_offsets[k], expr, row)
    return row


def _pos_block(cfg, i, tm):
    t0 = i * tm
    blk = None
    for k, (b, s) in enumerate(cfg.streams):
        expr = ((t0 - cfg.tok_offsets[k]) % s) // tm
        blk = expr if blk is None else jnp.where(t0 >= cfg.tok_offsets[k], expr, blk)
    return blk


def _stream_block(cfg, k, i, tm):
    nb = cfg.streams[k][0] * cfg.streams[k][1] // tm
    return jnp.clip(i - cfg.tok_offsets[k] // tm, 0, nb - 1)


def _in_stream(cfg, k, i, tm):
    lo = cfg.tok_offsets[k] // tm
    hi = lo + cfg.streams[k][0] * cfg.streams[k][1] // tm
    return jnp.logical_and(i >= lo, i < hi)


def _rms(x, eps):
    return x * lax.rsqrt(jnp.mean(x * x, axis=-1, keepdims=True) + eps)


def _pack_words(a, b):
    wa = lax.bitcast_convert_type(a.astype(BF16).astype(F32), U32) >> 16
    wb = lax.bitcast_convert_type(b.astype(BF16).astype(F32), U32)
    return wa | wb


def _unpack_words(w):
    lo = lax.bitcast_convert_type(w << 16, F32)
    hi = lax.bitcast_convert_type(w & jnp.uint32(0xFFFF0000), F32)
    return lo, hi


def _ada_kernel(c_ref, w_ref, b_ref, o_ref):
    c = c_ref[...]
    s = (c * jax.nn.sigmoid(c)).astype(BF16)
    o_ref[...] = jnp.dot(s, w_ref[...].astype(BF16), preferred_element_type=F32) + b_ref[...]


def _ada(c_pad, w_ada, b_ada):
    r, d = c_pad.shape
    n = w_ada.shape[1]
    tn = _tile(n, 512)
    return pl.pallas_call(
        _ada_kernel,
        grid=(n // tn,),
        in_specs=[pl.BlockSpec((r, d), lambda j: (0, 0)),
                  pl.BlockSpec((d, tn), lambda j: (0, j)),
                  pl.BlockSpec((1, tn), lambda j: (0, j))],
        out_specs=pl.BlockSpec((r, tn), lambda j: (0, j)),
        out_shape=jax.ShapeDtypeStruct((r, n), F32),
        compiler_params=_params(("arbitrary",), 2 * d * tn * 4 + d * tn * 2 + (8 << 20)),
        name="ada",
    )(c_pad, w_ada, b_ada)


def _in_tile(n):
    return 768 if n > 768 else n


def _in_proj_kernel(cfg, tm, n_blk, *refs):
    ns = len(cfg.streams)
    x_refs = refs[:ns]
    sh_ref, sc_ref, g_ref, w_ref, o_ref, h_ref, xbuf_ref, sem = refs[ns:]
    i = pl.program_id(0)

    def fetch(blk, slot, start):
        for k in range(ns):
            @pl.when(_in_stream(cfg, k, blk, tm))
            def _(k=k):
                row = pl.multiple_of((blk - cfg.tok_offsets[k] // tm) * tm, tm)
                cp = pltpu.make_async_copy(x_refs[k].at[pl.ds(row, tm)], xbuf_ref.at[slot], sem.at[slot])
                if start:
                    cp.start()
                else:
                    cp.wait()

    @pl.when(pl.program_id(1) == 0)
    def _():
        slot = i % 2

        @pl.when(i == 0)
        def _():
            fetch(i, slot, True)

        fetch(i, slot, False)

        @pl.when(i + 1 < n_blk)
        def _():
            fetch(i + 1, 1 - slot, True)

        gain = g_ref[...] * (1.0 + sc_ref[0])
        h_ref[...] = (_rms(xbuf_ref[slot], cfg.eps) * gain + sh_ref[0]).astype(BF16)

    o_ref[...] = jnp.dot(h_ref[...], w_ref[...], preferred_element_type=F32).astype(o_ref.dtype)


def _in_proj(cfg, xs, mod3, g_attn, w_cat):
    d = cfg.d_model
    t = cfg.n_tokens
    n = w_cat.shape[1]
    tm = _tile(min(s for _, s in cfg.streams), 512)
    tn = _in_tile(n)
    in_specs = [pl.BlockSpec(memory_space=pl.ANY) for _ in xs] + [
        pl.BlockSpec((1, 1, d), lambda i, j: (_mod_row(cfg, i * tm), 0, 0)),
        pl.BlockSpec((1, 1, d), lambda i, j: (_mod_row(cfg, i * tm), 0, 1)),
        pl.BlockSpec((1, d), lambda i, j: (0, 0)),
        pl.BlockSpec((d, tn), lambda i, j: (0, j)),
    ]
    vmem = 2 * tm * d * 4 + tm * d * 2 + 2 * d * tn * 2 + 2 * tm * tn * 2 + tm * tn * 4 + 2 * tm * d * 4
    return pl.pallas_call(
        functools.partial(_in_proj_kernel, cfg, tm, t // tm),
        grid=(t // tm, n // tn),
        in_specs=in_specs,
        out_specs=pl.BlockSpec((tm, tn), lambda i, j: (i, j)),
        out_shape=jax.ShapeDtypeStruct((t, n), BF16),
        scratch_shapes=[pltpu.VMEM((tm, d), BF16), pltpu.VMEM((2, tm, d), F32), pltpu.SemaphoreType.DMA((2,))],
        compiler_params=_params(("arbitrary", "arbitrary"), vmem + (6 << 20)),
        name="in_proj",
    )(*xs, mod3, mod3, g_attn, w_cat)


def _rope_half(t):
    return t + pltpu.roll(t, 64, axis=1)


def _q_up_kernel(cfg, c_ref, gl_ref, w_ref, ga_ref, gb_ref, cs_ref, o_ref):
    cn = (_rms(c_ref[...].astype(F32), cfg.eps) * gl_ref[...]).astype(BF16)
    ga = ga_ref[...]
    gbcs = gb_ref[...] * cs_ref[...]
    for h in range(cfg.n_heads):
        y = jnp.dot(cn, w_ref[h], preferred_element_type=F32)
        a = y[:, :LANES]
        b = y[:, LANES:]
        aa = a * a
        ssq2 = jnp.sum((aa + b * b) + aa, axis=-1, keepdims=True)
        s = lax.rsqrt(ssq2 + 2.0 * cfg.qk_dim * cfg.eps)
        o_ref[h, :, :LANES] = (a * ga * s).astype(o_ref.dtype)
        o_ref[h, :, LANES:] = (b * gbcs * s).astype(o_ref.dtype)


def _q_up(cfg, z, g_qa, w_q, ga, gb, cs, tm):
    t = cfg.n_tokens
    h = cfg.n_heads
    r = cfg.q_rank
    col = cfg.f_width // r
    vmem = 2 * tm * r * 2 + 2 * h * r * 256 * 2 + 4 * h * tm * 256 * 2 + tm * 128 * 8 + tm * r * 8 + (8 << 20)
    return pl.pallas_call(
        functools.partial(_q_up_kernel, cfg),
        grid=(t // tm,),
        in_specs=[pl.BlockSpec((tm, r), lambda i: (i, col)),
                  pl.BlockSpec((1, r), lambda i: (0, 0)),
                  pl.BlockSpec((h, r, 256), lambda i: (0, 0, 0)),
                  pl.BlockSpec((1, LANES), lambda i: (0, 0)),
                  pl.BlockSpec((1, LANES), lambda i: (0, 0)),
                  pl.BlockSpec((tm, LANES), lambda i: (_pos_block(cfg, i, tm), 0))],
        out_specs=pl.BlockSpec((h, tm, 256), lambda i: (0, i, 0)),
        out_shape=jax.ShapeDtypeStruct((h, t, 256), BF16),
        compiler_params=_params(("arbitrary",), vmem),
        name="q_up",
    )(z, g_qa, w_q, ga, gb, cs)


def _kv_up_kernel(cfg, c_ref, pe_ref, gl_ref, w_ref, ga_ref, gb_ref, cs_ref, k_ref, v_ref):
    cn = (_rms(c_ref[...].astype(F32), cfg.eps) * gl_ref[...]).astype(BF16)
    pe = pe_ref[...].astype(F32)
    lane = lax.broadcasted_iota(I32, pe.shape, 1)
    ssq_pe = jnp.sum(jnp.where(lane < 64, pe * pe, 0.0), axis=-1, keepdims=True)
    kr = _rope_half(pe * gb_ref[...] * cs_ref[...])
    ga = ga_ref[...]
    for h in range(cfg.n_heads):
        y = jnp.dot(cn, w_ref[h], preferred_element_type=F32)
        kn = y[:, :LANES]
        s = lax.rsqrt((jnp.sum(kn * kn, axis=-1, keepdims=True) + ssq_pe) * (1.0 / cfg.qk_dim) + cfg.eps)
        k_ref[h, :, :LANES] = (kn * ga * s).astype(k_ref.dtype)
        k_ref[h, :, LANES:] = (kr * s).astype(k_ref.dtype)
        v_ref[h, :, :LANES] = y[:, LANES:].astype(v_ref.dtype)
        v_ref[h, :, LANES:] = jnp.where(lane == 0, 1.0, 0.0).astype(v_ref.dtype)


def _kv_up(cfg, z, g_kva, w_kv, ga, gb, cs, tm):
    t = cfg.n_tokens
    h = cfg.n_heads
    r = cfg.kv_rank
    col_c = (cfg.f_width + cfg.q_rank) // r
    col_pe = (cfg.f_width + cfg.q_rank + cfg.kv_rank) // LANES
    vmem = 2 * tm * (r + 128) * 2 + 2 * h * r * 256 * 2 + 4 * h * tm * 512 * 2 + tm * r * 8 + (8 << 20)
    return pl.pallas_call(
        functools.partial(_kv_up_kernel, cfg),
        grid=(t // tm,),
        in_specs=[pl.BlockSpec((tm, r), lambda i: (i, col_c)),
                  pl.BlockSpec((tm, LANES), lambda i: (i, col_pe)),
                  pl.BlockSpec((1, r), lambda i: (0, 0)),
                  pl.BlockSpec((h, r, 256), lambda i: (0, 0, 0)),
                  pl.BlockSpec((1, LANES), lambda i: (0, 0)),
                  pl.BlockSpec((1, LANES), lambda i: (0, 0)),
                  pl.BlockSpec((tm, LANES), lambda i: (_pos_block(cfg, i, tm), 0))],
        out_specs=[pl.BlockSpec((h, tm, 256), lambda i: (0, i, 0)),
                   pl.BlockSpec((h, tm, 256), lambda i: (0, i, 0))],
        out_shape=[jax.ShapeDtypeStruct((h, t, 256), BF16),
                   jax.ShapeDtypeStruct((h, t, 256), BF16)],
        compiler_params=_params(("arbitrary",), vmem),
        name="kv_up",
    )(z, z, g_kva, w_kv, ga, gb, cs)


def _chunk_len(cfg):
    chunk = max(s for _, s in cfg.streams)
    for (b, s), off in zip(cfg.streams, cfg.tok_offsets):
        assert chunk % s == 0 and (b * s) % chunk == 0 and off % chunk == 0
    return chunk


def _in_stream_chunks(cfg, k, c, chunk):
    lo = cfg.tok_offsets[k] // chunk
    hi = lo + cfg.streams[k][0] * cfg.streams[k][1] // chunk
    return jnp.logical_and(c >= lo, c < hi)


def _attn_kernel(cfg, chunk, tq, tk, q_ref, k_ref, v_ref, o_ref):
    c = pl.program_id(0)
    qi = pl.program_id(2)
    for k, (_, s) in enumerate(cfg.streams):
        @pl.when(_in_stream_chunks(cfg, k, c, chunk))
        def _(s=s):
            start = ((qi * tq) // s) * s
            for hh in range(ATTN_HEADS_PER_STEP):
                q = q_ref[hh]
                m = jnp.full((tq, 1), -jnp.inf, F32)
                acc = jnp.zeros((tq, 2 * LANES), F32)
                for j in range(s // tk):
                    rows = pl.ds(pl.multiple_of(start + j * tk, tk), tk)
                    sc = lax.dot_general(q, k_ref[hh, rows, :], (((1,), (1,)), ((), ())),
                                         preferred_element_type=F32)
                    m_new = jnp.maximum(m, jnp.max(sc, axis=-1, keepdims=True))
                    p = jnp.exp2((sc - m_new).astype(BF16))
                    acc = jnp.exp2(m - m_new) * acc + jnp.dot(p, v_ref[hh, rows, :], preferred_element_type=F32)
                    m = m_new
                o_ref[:, hh * LANES:(hh + 1) * LANES] = (acc[:, :LANES] / acc[:, LANES:LANES + 1]).astype(o_ref.dtype)


ATTN_HEADS_PER_STEP = 2


def _attn(cfg, q, k, v):
    chunk = _chunk_len(cfg)
    h = cfg.n_heads
    t = cfg.n_tokens
    s_min = min(s for _, s in cfg.streams)
    tq = _tile(s_min, 1024)
    tk = _tile(s_min, 256)
    nq = chunk // tq
    hs = ATTN_HEADS_PER_STEP
    assert h % hs == 0
    vmem = hs * (4 * tq * 256 * 2 + 2 * chunk * 512 * 2 + 6 * tq * 256 * 4) + 8 * tq * tk * 4 + (8 << 20)
    return pl.pallas_call(
        functools.partial(_attn_kernel, cfg, chunk, tq, tk),
        grid=(t // chunk, h // hs, nq),
        in_specs=[pl.BlockSpec((hs, tq, 256), lambda ci, hi, qi: (hi, ci * nq + qi, 0)),
                  pl.BlockSpec((hs, chunk, 256), lambda ci, hi, qi: (hi, ci, 0)),
                  pl.BlockSpec((hs, chunk, 256), lambda ci, hi, qi: (hi, ci, 0))],
        out_specs=pl.BlockSpec((tq, hs * LANES), lambda ci, hi, qi: (ci * nq + qi, hi)),
        out_shape=jax.ShapeDtypeStruct((t, h * LANES), BF16),
        compiler_params=_params(("arbitrary", "arbitrary", "arbitrary"), vmem),
        name="attn",
    )(q, k, v)


def _fold_kernel(cc_ref, sc_ref, w_ref, o_ref):
    w = w_ref[0]
    o_ref[0, :, :256] = jnp.dot(cc_ref[...], w, preferred_element_type=F32,
                                precision=lax.Precision.HIGHEST).astype(o_ref.dtype)
    o_ref[0, :, 256:] = jnp.dot(sc_ref[...], w, preferred_element_type=F32,
                                precision=lax.Precision.HIGHEST).astype(o_ref.dtype)


def _fold_fourier_weights(cfg, cc, sc, w_fmix):
    g, c = cfg.f_groups, cfg.f_gdim
    return pl.pallas_call(
        _fold_kernel,
        grid=(g,),
        in_specs=[pl.BlockSpec((c, c), lambda i: (0, 0)),
                  pl.BlockSpec((c, c), lambda i: (0, 0)),
                  pl.BlockSpec((1, c, c), lambda i: (i, 0, 0))],
        out_specs=pl.BlockSpec((1, c, 2 * c), lambda i: (i, 0, 0)),
        out_shape=jax.ShapeDtypeStruct((g, c, 2 * c), BF16),
        compiler_params=_params(("arbitrary",), 16 << 20),
        name="fold_fourier",
    )(cc, sc, w_fmix)


MID_ROWS = 16


def _half_block(cfg, i, tm):
    hr0 = i * tm
    out = None
    for k, (b, s) in enumerate(cfg.streams):
        local = hr0 - cfg.tok_offsets[k] // 2
        seq = local // (s // 2)
        s0 = local % (s // 2)
        vals = (cfg.tok_offsets[k] + seq * s + s0, s0, cfg.row_offsets[k] + seq, s)
        if out is None:
            out = vals
        else:
            here = hr0 >= cfg.tok_offsets[k] // 2
            out = tuple(jnp.where(here, v, o) for v, o in zip(vals, out))
    return out


def _four1_kernel(cfg, tm, f_ref, r_ref, n_ref, m_ref, j_ref, w_ref, ab_ref, mid_ref):
    _, s0, _, _ = _half_block(cfg, pl.program_id(0), tm)
    c = cfg.f_gdim
    row = lax.broadcasted_iota(I32, (tm, 1), 0)
    for g in range(cfg.f_groups):
        sl = slice(g * c, (g + 1) * c)
        rev = jnp.dot(j_ref[...], r_ref[:, sl], preferred_element_type=F32)
        rev0 = jnp.where(s0 == 0, 0.0, n_ref[0:1, sl].astype(F32))
        rev = jnp.where(row == 0, rev0, rev)
        f = f_ref[:, sl].astype(F32)
        a = jnp.dot((f + rev).astype(BF16), w_ref[g, :, :c], preferred_element_type=F32)
        b = jnp.dot((f - rev).astype(BF16), w_ref[g, :, c:], preferred_element_type=F32)
        ab_ref[0, :, sl] = a.astype(ab_ref.dtype)
        ab_ref[1, :, sl] = b.astype(ab_ref.dtype)

    @pl.when(s0 == 0)
    def _():
        for g in range(cfg.f_groups):
            sl = slice(g * c, (g + 1) * c)
            mid_ref[:, sl] = jnp.dot(m_ref[:, sl], w_ref[g, :, :c], preferred_element_type=F32)


def _four1(cfg, z, w_fold, flip, tm):
    t = cfg.n_tokens
    fw = cfg.f_width
    g, c = cfg.f_groups, cfg.f_gdim

    def cur(i):
        return (_half_block(cfg, i, tm)[0] // tm, 0)

    def partner(i):
        row, s0, _, s = _half_block(cfg, i, tm)
        return ((row - s0 + s - s0) // tm - 1, 0)

    def after_partner(i):
        row, s0, _, s = _half_block(cfg, i, tm)
        return (jnp.minimum((row - s0 + s - s0) // MID_ROWS, t // MID_ROWS - 1), 0)

    def middle(i):
        row, s0, _, s = _half_block(cfg, i, tm)
        return ((row - s0 + s // 2) // MID_ROWS, 0)

    vmem = 4 * tm * fw * 2 + 2 * g * c * 2 * c * 2 + 4 * tm * fw * 2 + 6 * tm * c * 4 + (8 << 20)
    return pl.pallas_call(
        functools.partial(_four1_kernel, cfg, tm),
        grid=(t // 2 // tm,),
        in_specs=[pl.BlockSpec((tm, fw), cur),
                  pl.BlockSpec((tm, fw), partner),
                  pl.BlockSpec((MID_ROWS, fw), after_partner),
                  pl.BlockSpec((MID_ROWS, fw), middle),
                  pl.BlockSpec((tm, tm), lambda i: (0, 0)),
                  pl.BlockSpec((g, c, 2 * c), lambda i: (0, 0, 0))],
        out_specs=[pl.BlockSpec((2, tm, fw), lambda i: (0, i, 0)),
                   pl.BlockSpec((MID_ROWS, fw), lambda i: (_half_block(cfg, i, tm)[2], 0))],
        out_shape=[jax.ShapeDtypeStruct((2, t // 2, fw), BF16),
                   jax.ShapeDtypeStruct((cfg.n_rows * MID_ROWS, fw), F32)],
        compiler_params=_params(("arbitrary",), vmem),
        name="four1",
    )(z, z, z, z, flip, w_fold)


def _four2_kernel(cfg, chunk, tm, *refs):
    ns = len(cfg.streams)
    ab_ref, mid_ref, o_ref = refs[2 * ns:]
    c = pl.program_id(0)
    mi = pl.program_id(2)
    row = lax.broadcasted_iota(I32, (tm, 1), 0)
    for k, (_, s) in enumerate(cfg.streams):
        @pl.when(_in_stream_chunks(cfg, k, c, chunk))
        def _(k=k, s=s):
            half = s // 2
            start = pl.multiple_of(((mi * tm) // s) * half, half)
            y = jnp.dot(refs[2 * k][...], ab_ref[0, pl.ds(start, half), :], preferred_element_type=F32)
            y = y + jnp.dot(refs[2 * k + 1][...], ab_ref[1, pl.ds(start, half), :], preferred_element_type=F32)
            sign = jnp.where(row % 2 == 0, s ** -0.5, -(s ** -0.5))
            o_ref[...] = (y + sign * mid_ref[0:1, :]).astype(o_ref.dtype)


def _four2(cfg, dfts, ab, mid):
    chunk = _chunk_len(cfg)
    t = cfg.n_tokens
    fw = cfg.f_width
    tm = _tile(min(s for _, s in cfg.streams), 512)
    tn = _tile(fw, 1024)
    nm = chunk // tm
    assert tm % 2 == 0

    def d_spec(k):
        s = cfg.streams[k][1]

        def imap(ci, ni, mi):
            inside = _in_stream_chunks(cfg, k, ci, chunk)
            before = ci < cfg.tok_offsets[k] // chunk
            return (jnp.where(inside, ((mi * tm) % s) // tm, jnp.where(before, 0, s // tm - 1)), 0)
        return pl.BlockSpec((tm, s // 2), imap)

    in_specs, args, vmem = [], [], 0
    for k, (dc, ds) in enumerate(dfts):
        in_specs += [d_spec(k), d_spec(k)]
        args += [dc, ds]
        vmem += 4 * tm * (cfg.streams[k][1] // 2) * 2
    in_specs.append(pl.BlockSpec((2, chunk // 2, tn), lambda ci, ni, mi: (0, ci, ni)))
    in_specs.append(pl.BlockSpec((MID_ROWS, tn), lambda ci, ni, mi: (_mod_row(cfg, ci * chunk + mi * tm), ni)))
    vmem += 4 * (chunk // 2) * tn * 2 + 2 * tm * tn * 2 + 3 * tm * tn * 4 + (8 << 20)
    return pl.pallas_call(
        functools.partial(_four2_kernel, cfg, chunk, tm),
        grid=(t // chunk, fw // tn, nm),
        in_specs=in_specs,
        out_specs=pl.BlockSpec((tm, tn), lambda ci, ni, mi: (ci * nm + mi, ni)),
        out_shape=jax.ShapeDtypeStruct((t, fw), BF16),
        compiler_params=_params(("arbitrary", "arbitrary", "arbitrary"), vmem),
        name="four2",
    )(*args, ab, mid)


def _out_proj_kernel(cfg, tm, *refs):
    ns = len(cfg.streams)
    x_refs = refs[:ns]
    gt_ref, a_ref, f_ref, wa_ref, wb_ref, o_ref = refs[ns:]
    i = pl.program_id(0)
    mix = jnp.dot(a_ref[...], wa_ref[...], preferred_element_type=F32)
    mix = mix + jnp.dot(f_ref[...], wb_ref[...], preferred_element_type=F32)
    for k in range(ns):
        @pl.when(_in_stream(cfg, k, i, tm))
        def _(k=k):
            o_ref[...] = x_refs[k][...] + gt_ref[0] * mix


def _out_proj(cfg, xs, mod3, o_attn, o_four, wa, wb):
    d = cfg.d_model
    t = cfg.n_tokens
    aw, fw = cfg.attn_width, cfg.f_width
    tm = _tile(min(s for _, s in cfg.streams), 1024)
    tn = _tile(d, 512)
    nj = d // tn

    def x_spec(k):
        def imap(i, j):
            inside = _in_stream(cfg, k, i, tm)
            before = i < cfg.tok_offsets[k] // tm
            return (_stream_block(cfg, k, i, tm), jnp.where(inside, j, jnp.where(before, 0, nj - 1)))
        return pl.BlockSpec((tm, tn), imap)

    in_specs = [x_spec(k) for k in range(len(xs))] + [
        pl.BlockSpec((1, 1, tn), lambda i, j: (_mod_row(cfg, i * tm), 0, 2 * nj + j)),
        pl.BlockSpec((tm, aw), lambda i, j: (i, 0)),
        pl.BlockSpec((tm, fw), lambda i, j: (i, 0)),
        pl.BlockSpec((aw, tn), lambda i, j: (0, j)),
        pl.BlockSpec((fw, tn), lambda i, j: (0, j)),
    ]
    vmem = (2 * tm * tn * 4 * len(xs) + 2 * tm * (aw + fw) * 2 + 2 * (aw + fw) * tn * 2
            + 2 * tm * tn * 4 + 2 * tm * tn * 4 + (8 << 20))
    return pl.pallas_call(
        functools.partial(_out_proj_kernel, cfg, tm),
        grid=(t // tm, nj),
        in_specs=in_specs,
        out_specs=pl.BlockSpec((tm, tn), lambda i, j: (i, j)),
        out_shape=jax.ShapeDtypeStruct((t, d), F32),
        compiler_params=_params(("arbitrary", "arbitrary"), vmem),
        name="out_proj",
    )(*xs, mod3, o_attn, o_four, wa, wb)


def _router_kernel(cfg, tm, x_ref, sh_ref, sc_ref, g_ref, wr_ref, tri_ref,
                   hp_ref, mi_ref, gt_ref, cnt_ref, carry_ref):
    i = pl.program_id(0)
    ng, ne = cfg.n_groups, cfg.e_per_group
    n_exp = cfg.n_experts

    @pl.when(i == 0)
    def _():
        carry_ref[...] = jnp.zeros_like(carry_ref)

    h2 = _rms(x_ref[...], cfg.eps) * g_ref[...] * (1.0 + sc_ref[0]) + sh_ref[0]
    for c in range(cfg.d_model // PACK_CHUNK):
        lo = h2[:, c * PACK_CHUNK:c * PACK_CHUNK + LANES]
        hi = h2[:, c * PACK_CHUNK + LANES:(c + 1) * PACK_CHUNK]
        hp_ref[:, c * LANES:(c + 1) * LANES] = _pack_words(lo, hi)

    h_hi = h2.astype(BF16)
    h_lo = (h2 - h_hi.astype(F32)).astype(BF16)
    p1 = jnp.dot(h_hi, wr_ref[...], preferred_element_type=F32)
    p2 = jnp.dot(h_lo, wr_ref[:, :LANES], preferred_element_type=F32)
    logits = (p1[:, :LANES] + p1[:, LANES:] + p2).T

    gl = logits[0:ng]
    io = lax.broadcasted_iota(I32, (ng, tm), 0)
    gm = jnp.max(gl, axis=0, keepdims=True)
    p_group = 1.0 / jnp.sum(jnp.exp(gl - gm), axis=0, keepdims=True)
    gidx = jnp.min(jnp.where(gl == gm, io, ng), axis=0, keepdims=True)
    sel = jnp.zeros((ne, tm), F32)
    for g in range(ng):
        sel = jnp.where(gidx == g, logits[ng + g * ne:ng + (g + 1) * ne], sel)
    ie = lax.broadcasted_iota(I32, (ne, tm), 0)
    m1 = jnp.max(sel, axis=0, keepdims=True)
    i1 = jnp.min(jnp.where(sel == m1, ie, ne), axis=0, keepdims=True)
    sel2 = jnp.where(ie == i1, -jnp.inf, sel)
    m2 = jnp.max(sel2, axis=0, keepdims=True)
    i2 = jnp.min(jnp.where(sel2 == m2, ie, ne), axis=0, keepdims=True)
    e21 = jnp.exp(m2 - m1)
    gate0 = p_group / (1.0 + e21)
    gate1 = p_group * e21 / (1.0 + e21)
    e0 = gidx * ne + i1
    e1 = gidx * ne + i2

    ix = lax.broadcasted_iota(I32, (n_exp, tm), 0)
    hit0 = ix == e0
    hit1 = ix == e1
    member = jnp.logical_or(hit0, hit1).astype(F32)
    before = jnp.dot(member.astype(BF16), tri_ref[...], preferred_element_type=F32)
    total = before + carry_ref[:, 0:1]
    r0 = jnp.sum(jnp.where(hit0, total, 0.0), axis=0, keepdims=True).astype(I32)
    r1 = jnp.sum(jnp.where(hit1, total, 0.0), axis=0, keepdims=True).astype(I32)
    carry_ref[...] = carry_ref[...] + jnp.sum(member, axis=1, keepdims=True)
    cnt_ref[...] = carry_ref[...].astype(I32)

    i8 = lax.broadcasted_iota(I32, (8, tm), 0)
    mi_ref[...] = jnp.where(i8 == 0, e0, jnp.where(i8 == 1, e1, jnp.where(i8 == 2, r0, jnp.where(i8 == 3, r1, 0))))
    il = lax.broadcasted_iota(I32, (LANES, tm), 0)
    gt_ref[...] = jnp.where(il == 0, gate0, jnp.where(il == 1, gate1, 0.0)).T


def _router(cfg, x1, mod3, g_ffn, wr_cat, tri, tm):
    d = cfg.d_model
    t = cfg.n_tokens
    ne = cfg.n_experts
    vmem = 2 * tm * d * 4 + 2 * tm * d * 2 + 2 * d * 256 * 2 + 6 * tm * d * 4 + 2 * tm * tm * 2 + (8 << 20)
    return pl.pallas_call(
        functools.partial(_router_kernel, cfg, tm),
        grid=(t // tm,),
        in_specs=[pl.BlockSpec((tm, d), lambda i: (i, 0)),
                  pl.BlockSpec((1, 1, d), lambda i: (_mod_row(cfg, i * tm), 0, 3)),
                  pl.BlockSpec((1, 1, d), lambda i: (_mod_row(cfg, i * tm), 0, 4)),
                  pl.BlockSpec((1, d), lambda i: (0, 0)),
                  pl.BlockSpec((d, 256), lambda i: (0, 0)),
                  pl.BlockSpec((tm, tm), lambda i: (0, 0))],
        out_specs=[pl.BlockSpec((tm, d // 2), lambda i: (i, 0)),
                   pl.BlockSpec((8, tm), lambda i: (0, i)),
                   pl.BlockSpec((tm, LANES), lambda i: (i, 0)),
                   pl.BlockSpec((ne, LANES), lambda i: (0, 0))],
        out_shape=[jax.ShapeDtypeStruct((t, d // 2), U32),
                   jax.ShapeDtypeStruct((8, t), I32),
                   jax.ShapeDtypeStruct((t, LANES), F32),
                   jax.ShapeDtypeStruct((ne, LANES), I32)],
        scratch_shapes=[pltpu.VMEM((ne, LANES), F32)],
        compiler_params=_params(("arbitrary",), vmem),
        name="router",
    )(x1, mod3, mod3, g_ffn, wr_cat, tri)


def _dest_row(start_ref, meta_ref, kk, r):
    return start_ref[meta_ref[kk, r]] + meta_ref[2 + kk, r]


def _dispatch_kernel(tm, bm, n_zero, start_ref, zero_ref, meta_ref, h_ref, xs_ref, zbuf_ref, sem, zsem):
    @pl.when(pl.program_id(0) == 0)
    def _():
        zbuf_ref[...] = jnp.zeros_like(zbuf_ref)

        def fill(k):
            dst = xs_ref.at[pl.ds(pl.multiple_of(zero_ref[k] * bm, bm), bm)]
            return pltpu.make_async_copy(zbuf_ref, dst, zsem)

        def start(k, carry):
            @pl.when(zero_ref[k] >= 0)
            def _():
                fill(k).start()
            return carry

        def finish(k, carry):
            @pl.when(zero_ref[k] >= 0)
            def _():
                fill(k).wait()
            return carry

        lax.fori_loop(0, n_zero, start, 0)
        lax.fori_loop(0, n_zero, finish, 0)

    def issue(r, carry):
        for kk in range(2):
            dst = xs_ref.at[pl.ds(_dest_row(start_ref, meta_ref, kk, r), 1)]
            pltpu.make_async_copy(h_ref.at[pl.ds(r, 1)], dst, sem).start(priority=kk)
        return carry

    lax.fori_loop(0, tm, issue, 0, unroll=8)
    for kk in range(2):
        pltpu.make_async_copy(h_ref, xs_ref.at[pl.ds(0, tm)], sem).wait()


def _dispatch(cfg, pad_start, zero_blocks, meta, h2p, n_rows, tm):
    t = cfg.n_tokens
    w = h2p.shape[1]
    bm = cfg.moe_block
    grid_spec = pltpu.PrefetchScalarGridSpec(
        num_scalar_prefetch=2,
        grid=(t // tm,),
        in_specs=[pl.BlockSpec((8, tm), lambda i, st, zb: (0, i), memory_space=pltpu.SMEM),
                  pl.BlockSpec((tm, w), lambda i, st, zb: (i, 0))],
        out_specs=pl.BlockSpec(memory_space=pl.ANY),
        scratch_shapes=[pltpu.VMEM((bm, w), U32), pltpu.SemaphoreType.DMA(()), pltpu.SemaphoreType.DMA(())],
    )
    return pl.pallas_call(
        functools.partial(_dispatch_kernel, tm, bm, zero_blocks.shape[0]),
        grid_spec=grid_spec,
        out_shape=jax.ShapeDtypeStruct((n_rows, w), U32),
        compiler_params=_params(("arbitrary",), 4 * tm * w * 4 + bm * w * 4 + (4 << 20)),
        name="dispatch",
    )(pad_start, zero_blocks, meta, h2p)


def _expert_schedule(cfg, cnt, n_blocks):
    bm = cfg.moe_block
    padded = (cnt + bm - 1) // bm * bm
    pad_end = jnp.cumsum(padded)
    pad_start = (pad_end - padded).astype(I32)
    blk = jnp.arange(n_blocks, dtype=I32)
    be = jnp.minimum(jnp.sum((pad_end[None, :] <= (blk * bm)[:, None]).astype(I32), axis=1), cfg.n_experts - 1)
    nv = (pad_end[-1] // bm).astype(I32)
    prev = jnp.concatenate([jnp.full((1,), -1, I32), be[:-1]])
    first = jnp.logical_and(blk < nv, be != prev)
    suffix = lax.cummin(jnp.where(first, blk, n_blocks), axis=0, reverse=True)
    next_first = jnp.concatenate([suffix[1:], jnp.full((1,), n_blocks, I32)])
    last = next_first >= n_blocks
    nxt = be[jnp.where(last, 0, next_first)]
    run = jnp.cumsum(first.astype(I32)) - 1
    n_runs = jnp.sum(first.astype(I32))
    half = (cnt[be] - (blk * bm - pad_start[be])) <= bm // 2
    tables = (be.astype(I32), first.astype(I32), nxt.astype(I32), last.astype(I32), run.astype(I32),
              half.astype(I32))
    zero_blocks = jnp.concatenate([jnp.where(padded > 0, pad_end // bm - 1, -1),
                                   jnp.where(blk >= nv, blk, -1)]).astype(I32)
    return pad_start, zero_blocks, tables + (n_runs[None], nv[None])


N_SCHED = 8


def _stage_expert_weights(sched, i, tile, n_tiles, copies):
    be_ref, first_ref, nxt_ref, last_ref, run_ref, _, n_runs_ref, nv_ref = sched
    slot = (tile * n_runs_ref[0] + run_ref[i]) % 2

    @pl.when(jnp.logical_and(i < nv_ref[0], first_ref[i] == 1))
    def _():
        @pl.when(jnp.logical_and(tile == 0, i == 0))
        def _():
            for cp in copies:
                cp(be_ref[i], tile, slot).start()

        for cp in copies:
            cp(be_ref[i], tile, slot).wait()
        is_last = last_ref[i] == 1

        @pl.when(jnp.logical_not(is_last))
        def _():
            for cp in copies:
                cp(nxt_ref[i], tile, 1 - slot).start()

        @pl.when(jnp.logical_and(is_last, tile + 1 < n_tiles))
        def _():
            for cp in copies:
                cp(nxt_ref[i], tile + 1, 1 - slot).start()

    return slot


def _moe1_kernel(cfg, tf, n_tiles, *refs):
    sched = refs[:N_SCHED]
    x_ref, w1_ref, w3_ref, o_ref, xb_ref, s1_ref, s3_ref, sem = refs[N_SCHED:]
    nv_ref = sched[-1]
    j = pl.program_id(0)
    i = pl.program_id(1)

    def copy(w_ref, stage_ref, k):
        def make(e, jj, slot):
            src = w_ref.at[e, :, pl.ds(pl.multiple_of(jj * tf, tf), tf)]
            return pltpu.make_async_copy(src, stage_ref.at[slot], sem.at[k, slot])
        return make

    slot = _stage_expert_weights(sched, i, j, n_tiles, [copy(w1_ref, s1_ref, 0), copy(w3_ref, s3_ref, 1)])
    bm = x_ref.shape[0]
    valid = i < nv_ref[0]
    half = sched[5][i] == 1

    def compute(m):
        for c in range(cfg.d_model // PACK_CHUNK):
            lo, hi = _unpack_words(x_ref[:m, c * LANES:(c + 1) * LANES])
            xb_ref[:m, c * PACK_CHUNK:c * PACK_CHUNK + LANES] = lo.astype(BF16)
            xb_ref[:m, c * PACK_CHUNK + LANES:(c + 1) * PACK_CHUNK] = hi.astype(BF16)
        x = xb_ref[:m, :]
        a = jnp.dot(x, s1_ref[slot].astype(BF16), preferred_element_type=F32)
        b = jnp.dot(x, s3_ref[slot].astype(BF16), preferred_element_type=F32)
        o_ref[:m, :] = (a * jax.nn.sigmoid(a) * b).astype(o_ref.dtype)
        if m < bm:
            o_ref[m:, :] = jnp.zeros((bm - m, o_ref.shape[1]), o_ref.dtype)

    @pl.when(jnp.logical_and(valid, jnp.logical_not(half)))
    def _():
        compute(bm)

    @pl.when(jnp.logical_and(valid, half))
    def _():
        compute(bm // 2)

    @pl.when(i >= nv_ref[0])
    def _():
        o_ref[...] = jnp.zeros_like(o_ref)


def _moe1(cfg, sched, xs, w1, w3):
    d, f = cfg.d_model, cfg.d_expert
    bm = cfg.moe_block
    rows = xs.shape[0]
    tf = _tile(f, cfg.moe_up_tile)
    n_tiles = f // tf

    def x_map(j, i, *sched):
        return (jnp.minimum(i, sched[-1][0] - 1), 0)

    grid_spec = pltpu.PrefetchScalarGridSpec(
        num_scalar_prefetch=N_SCHED,
        grid=(n_tiles, rows // bm),
        in_specs=[pl.BlockSpec((bm, d // 2), x_map),
                  pl.BlockSpec(memory_space=pl.ANY),
                  pl.BlockSpec(memory_space=pl.ANY)],
        out_specs=pl.BlockSpec((bm, tf), lambda j, i, *_: (i, j)),
        scratch_shapes=[pltpu.VMEM((bm, d), BF16),
                        pltpu.VMEM((2, d, tf), F32), pltpu.VMEM((2, d, tf), F32),
                        pltpu.SemaphoreType.DMA((2, 2))],
    )
    vmem = 2 * bm * d * 2 + bm * d * 2 + 4 * d * tf * 4 + 2 * d * tf * 2 + 2 * bm * tf * 2 + 4 * bm * tf * 4 + (4 << 20)
    return pl.pallas_call(
        functools.partial(_moe1_kernel, cfg, tf, n_tiles),
        grid_spec=grid_spec,
        out_shape=jax.ShapeDtypeStruct((rows, f), BF16),
        compiler_params=_params(("arbitrary", "arbitrary"), vmem),
        name="moe1",
    )(*sched, xs, w1, w3)


def _moe2_kernel(cfg, tn, n_tiles, *refs):
    sched = refs[:N_SCHED]
    h_ref, w2_ref, o_ref, s2_ref, sem = refs[N_SCHED:]
    nv_ref = sched[-1]
    n = pl.program_id(0)
    i = pl.program_id(1)

    def copy(e, nn, slot):
        src = w2_ref.at[e, :, pl.ds(pl.multiple_of(nn * tn, tn), tn)]
        return pltpu.make_async_copy(src, s2_ref.at[slot], sem.at[slot])

    slot = _stage_expert_weights(sched, i, n, n_tiles, [copy])
    bm = h_ref.shape[0]
    valid = i < nv_ref[0]
    half = sched[5][i] == 1

    def compute(m):
        y = jnp.dot(h_ref[:m, :], s2_ref[slot].astype(BF16), preferred_element_type=F32)
        for c in range(tn // PACK_CHUNK):
            o_ref[:m, c * LANES:(c + 1) * LANES] = _pack_words(
                y[:, c * PACK_CHUNK:c * PACK_CHUNK + LANES], y[:, c * PACK_CHUNK + LANES:(c + 1) * PACK_CHUNK])
        if m < bm:
            o_ref[m:, :] = jnp.zeros((bm - m, o_ref.shape[1]), o_ref.dtype)

    @pl.when(jnp.logical_and(valid, jnp.logical_not(half)))
    def _():
        compute(bm)

    @pl.when(jnp.logical_and(valid, half))
    def _():
        compute(bm // 2)

    @pl.when(i >= nv_ref[0])
    def _():
        o_ref[...] = jnp.zeros_like(o_ref)


def _moe2(cfg, sched, hmid, w2):
    d, f = cfg.d_model, cfg.d_expert
    bm = cfg.moe_block
    rows = hmid.shape[0]
    tn = _tile(d, cfg.moe_down_tile)
    assert tn % PACK_CHUNK == 0
    n_tiles = d // tn

    def h_map(n, i, *sched):
        return (jnp.minimum(i, sched[-1][0] - 1), 0)

    grid_spec = pltpu.PrefetchScalarGridSpec(
        num_scalar_prefetch=N_SCHED,
        grid=(n_tiles, rows // bm),
        in_specs=[pl.BlockSpec((bm, f), h_map),
                  pl.BlockSpec(memory_space=pl.ANY)],
        out_specs=pl.BlockSpec((bm, tn // 2), lambda n, i, *_: (i, n)),
        scratch_shapes=[pltpu.VMEM((2, f, tn), F32), pltpu.SemaphoreType.DMA((2,))],
    )
    vmem = 2 * bm * f * 2 + 2 * f * tn * 4 + f * tn * 2 + 2 * bm * tn * 2 + 2 * bm * tn * 4 + (8 << 20)
    return pl.pallas_call(
        functools.partial(_moe2_kernel, cfg, tn, n_tiles),
        grid_spec=grid_spec,
        out_shape=jax.ShapeDtypeStruct((rows, d // 2), U32),
        compiler_params=_params(("arbitrary", "arbitrary"), vmem),
        name="moe2",
    )(*sched, hmid, w2)


def _combine_kernel(cfg, tm, n_blk, start_ref, meta_ref, meta_nxt_ref, x_ref, gt2_ref, gate_ref, y_ref, o_ref,
                    buf_ref, sem):
    i = pl.program_id(0)
    slot = i % 2

    n_chunks = cfg.d_model // PACK_CHUNK
    rows_per_chunk = tm // n_chunks

    def start_row(m_ref, r, sl):
        for kk in range(2):
            src = y_ref.at[pl.ds(_dest_row(start_ref, m_ref, kk, r), 1)]
            pltpu.make_async_copy(src, buf_ref.at[sl, kk, pl.ds(r, 1)], sem.at[sl]).start(priority=kk)

    def wait_slot(sl):
        for kk in range(2):
            pltpu.make_async_copy(y_ref.at[pl.ds(0, tm)], buf_ref.at[sl, kk], sem.at[sl]).wait()

    @pl.when(i == 0)
    def _():
        def body(r, carry):
            start_row(meta_ref, r, slot)
            return carry
        lax.fori_loop(0, tm, body, 0, unroll=8)

    wait_slot(slot)

    g0 = gate_ref[:, 0:1]
    g1 = gate_ref[:, 1:2]
    for c in range(n_chunks):
        lo0, hi0 = _unpack_words(buf_ref[slot, 0, :, c * LANES:(c + 1) * LANES])
        lo1, hi1 = _unpack_words(buf_ref[slot, 1, :, c * LANES:(c + 1) * LANES])
        sl_lo = slice(c * PACK_CHUNK, c * PACK_CHUNK + LANES)
        sl_hi = slice(c * PACK_CHUNK + LANES, (c + 1) * PACK_CHUNK)
        o_ref[:, sl_lo] = x_ref[:, sl_lo] + gt2_ref[0, :, sl_lo] * (g0 * lo0 + g1 * lo1)
        o_ref[:, sl_hi] = x_ref[:, sl_hi] + gt2_ref[0, :, sl_hi] * (g0 * hi0 + g1 * hi1)
        for r in range(c * rows_per_chunk, (c + 1) * rows_per_chunk):
            start_row(meta_nxt_ref, r, 1 - slot)

    @pl.when(i == n_blk - 1)
    def _():
        wait_slot(1 - slot)


def _combine(cfg, k_idx, pad_start, meta, x1, mod3, gates, y, tm):
    b, s = cfg.streams[k_idx]
    off = cfg.tok_offsets[k_idx]
    d = cfg.d_model
    n = b * s
    ob = off // tm
    row0 = cfg.row_offsets[k_idx]
    n_blk = n // tm
    grid_spec = pltpu.PrefetchScalarGridSpec(
        num_scalar_prefetch=1,
        grid=(n_blk,),
        in_specs=[pl.BlockSpec((8, tm), lambda i, st: (0, ob + i), memory_space=pltpu.SMEM),
                  pl.BlockSpec((8, tm), lambda i, st: (0, ob + jnp.minimum(i + 1, n_blk - 1)),
                               memory_space=pltpu.SMEM),
                  pl.BlockSpec((tm, d), lambda i, st: (ob + i, 0)),
                  pl.BlockSpec((1, 1, d), lambda i, st: (row0 + (i * tm) // s, 0, 5)),
                  pl.BlockSpec((tm, LANES), lambda i, st: (ob + i, 0)),
                  pl.BlockSpec(memory_space=pl.ANY)],
        out_specs=pl.BlockSpec((tm, d), lambda i, st: (i, 0)),
        scratch_shapes=[pltpu.VMEM((2, 2, tm, d // 2), U32), pltpu.SemaphoreType.DMA((2,))],
    )
    vmem = 4 * tm * d * 4 + 4 * tm * d * 2 + 2 * tm * 128 * 4 + 6 * tm * 128 * 4 + (8 << 20)
    return pl.pallas_call(
        functools.partial(_combine_kernel, cfg, tm, n_blk),
        grid_spec=grid_spec,
        out_shape=jax.ShapeDtypeStruct((n, d), F32),
        compiler_params=_params(("arbitrary",), vmem),
        name=f"combine{k_idx}",
    )(pad_start, meta, meta, x1, mod3, gates, y)


def _rot_half(a, axis=-1):
    h = a.shape[axis] // 2
    lo = lax.slice_in_dim(a, 0, h, axis=axis)
    hi = lax.slice_in_dim(a, h, 2 * h, axis=axis)
    return jnp.concatenate([hi, lo], axis=axis)


def _rope_table(cfg, n_pos):
    half = cfg.rope // 2
    inv_freq = cfg.theta ** (-2.0 * jnp.arange(half, dtype=F32) / cfg.rope)
    ang = jnp.arange(n_pos, dtype=F32)[:, None] * inv_freq[None, :]
    cos, sin = jnp.cos(ang), jnp.sin(ang)
    return jnp.concatenate([cos, cos, -sin, sin], axis=-1)


def _dft(n, scale):
    idx = jnp.arange(n, dtype=I32)
    ang = ((idx[:, None] * idx[None, :]) % n).astype(F32) * (2.0 * math.pi / n)
    return jnp.cos(ang) * scale, jnp.sin(ang) * scale


def _dft_half(n, split=64):
    s = jnp.arange(n // 2, dtype=I32)

    def table(rows):
        ang = ((rows[:, None] * s[None, :]) % n).astype(F32) * (2.0 * math.pi / n)
        return jnp.cos(ang), jnp.sin(ang)

    ch, sh = table(jnp.arange(n // split, dtype=I32) * split)
    cl, sl = table(jnp.arange(split, dtype=I32))
    scale = n ** -0.5
    cos = (ch[:, None, :] * cl[None] - sh[:, None, :] * sl[None]) * scale
    msin = (sh[:, None, :] * cl[None] + ch[:, None, :] * sl[None]) * -scale
    return cos.reshape(n, n // 2).astype(BF16), msin.reshape(n, n // 2).astype(BF16)


def _layer(cfg, xs, cs_, w_ada, b_ada, g_attn, w_in, g_qa, w_uq, g_kva, w_ukv, g_qn, g_kn,
           w_fmix, w_o, g_ffn, w_group, w_route, w1, w3, w2):
    d, h = cfg.d_model, cfg.n_heads
    t = cfg.n_tokens
    nope, rope = cfg.nope, cfg.rope
    assert nope == LANES and cfg.v_dim == LANES and rope == 64 and cfg.f_gdim == PACK_CHUNK
    assert cfg.f_width % cfg.q_rank == 0 and (cfg.f_width + cfg.q_rank) % cfg.kv_rank == 0
    qr, kr = cfg.q_rank, cfg.kv_rank

    w16 = w_in.astype(BF16)
    half = rope // 2
    c0 = qr + kr
    n_cat = w16.shape[1] + rope
    w_cat = jnp.concatenate(
        [w16[:, c0 + rope:], w16[:, :c0 + rope], w16[:, c0 + half:c0 + rope], w16[:, c0:c0 + half],
         jnp.zeros((d, -n_cat % _in_tile(n_cat)), BF16)], axis=1)

    wq3 = w_uq.reshape(qr, h, nope + rope)
    w_q = jnp.concatenate([wq3, _rot_half(wq3[..., nope:])], axis=-1).transpose(1, 0, 2).astype(BF16)
    w_kv = w_ukv.reshape(kr, h, nope + cfg.v_dim).transpose(1, 0, 2).astype(BF16)
    q_const = math.sqrt(2.0 * cfg.qk_dim) * cfg.qk_dim ** -0.5 * math.log2(math.e)
    ga_q = g_qn[None, :nope] * q_const
    gb_q = jnp.concatenate([g_qn[nope:], _rot_half(g_qn[nope:])])[None] * q_const
    ga_k, gb_k = g_kn[None, :nope], jnp.concatenate([g_kn[nope:], _rot_half(g_kn[nope:])])[None]
    wa, wb = w_o[:cfg.attn_width].astype(BF16), w_o[cfg.attn_width:].astype(BF16)
    wr = jnp.concatenate([w_group, w_route.reshape(d, cfg.n_experts)], axis=1)
    wr = jnp.pad(wr, ((0, 0), (0, LANES - wr.shape[1])))
    wr_hi = wr.astype(BF16)
    wr_cat = jnp.concatenate([wr_hi, (wr - wr_hi.astype(F32)).astype(BF16)], axis=1)

    s_max = max(s for _, s in cfg.streams)
    cs_tab = _rope_table(cfg, s_max)
    cc, sc = _dft(cfg.f_gdim, cfg.f_gdim ** -0.5)

    c_all = jnp.concatenate(cs_, axis=0)
    rows = -(-c_all.shape[0] // 8) * 8
    c_pad = jnp.pad(c_all, ((0, rows - c_all.shape[0]), (0, 0)))
    mod3 = _ada(c_pad, w_ada, b_ada[None]).reshape(rows, 1, 6 * d)

    x2d = [x.reshape(-1, d) for x in xs]
    z = _in_proj(cfg, x2d, mod3, g_attn[None], w_cat)
    tm = _tile(min(s for _, s in cfg.streams), 512)
    q = _q_up(cfg, z, g_qa[None], w_q, ga_q, gb_q, cs_tab, tm)
    k, v = _kv_up(cfg, z, g_kva[None], w_kv, ga_k, gb_k, cs_tab, tm)
    o_attn = _attn(cfg, q, k, v)

    w_fold = _fold_fourier_weights(cfg, cc, sc, w_fmix)
    tm_f = _tile(min(s for _, s in cfg.streams) // 2, 512)
    idx = jnp.arange(tm_f)
    flip = (idx[:, None] + idx[None, :] == tm_f).astype(BF16)
    ab, mid = _four1(cfg, z, w_fold, flip, tm_f)
    o_four = _four2(cfg, [_dft_half(s) for _, s in cfg.streams], ab, mid)

    x1 = _out_proj(cfg, x2d, mod3, o_attn, o_four, wa, wb)

    tm_r = _tile(tm, 256)
    tri = (jnp.arange(tm_r)[:, None] < jnp.arange(tm_r)[None, :]).astype(BF16)
    h2p, meta, gates, counts = _router(cfg, x1, mod3, g_ffn[None], wr_cat, tri, tm_r)

    bm = cfg.moe_block
    n_blocks = (2 * t) // bm + cfg.n_experts
    pad_start, zero_blocks, sched = _expert_schedule(cfg, counts[:, 0], n_blocks)

    xs_rows = _dispatch(cfg, pad_start, zero_blocks, meta, h2p, n_blocks * bm, _tile(tm, 256))
    hmid = _moe1(cfg, sched, xs_rows, w1, w3)
    y = _moe2(cfg, sched, hmid, w2)

    outs = []
    for ki, (b, s) in enumerate(cfg.streams):
        o = _combine(cfg, ki, pad_start, meta, x1, mod3, gates, y, _tile(tm, 256))
        outs.append(o.reshape(b, s, d))
    return tuple(outs)


def kernel(x_prompt, x_sample, c_prompt, c_sample, w_ada, b_ada, g_attn, w_in, g_qa, w_uq, g_kva, w_ukv,
           g_qn, g_kn, w_fmix, w_o, g_ffn, w_group, w_route, w1, w3, w2):
    cfg = Cfg()
    xs = (x_prompt, x_sample)
    cs_ = (c_prompt, c_sample)
    for l in range(w_ada.shape[0]):
        xs = _layer(cfg, xs, cs_, w_ada[l], b_ada[l], g_attn[l], w_in[l], g_qa[l], w_uq[l], g_kva[l],
                    w_ukv[l], g_qn[l], g_kn[l], w_fmix[l], w_o[l], g_ffn[l], w_group[l], w_route[l],
                    w1[l], w3[l], w2[l])
    return xs
```
